```python
import math
import jax, jax.numpy as jnp
from jax import lax
import numpy as np

D_MODEL = 1024
BATCH = 8
SEQ = 2048
DEPTH = 4

GRID_W = 64
CTX_LEN = 256

MLA_HEADS = 8
MLA_Q_LORA = 384
MLA_KV_LORA = 256
MLA_NOPE = 64
MLA_ROPE = 32
MLA_V = 64
MLA_SCALE = (MLA_NOPE + MLA_ROPE) ** -0.5
ROPE_BASE = 10000.0
ROPE_FREQS = MLA_ROPE // 4
Q_BLOCK = 128

HG_HEADS = 4
HG_DK = 128
HG_DV = 128
HG_KW = HG_HEADS * HG_DK
HG_VW = HG_HEADS * HG_DV

GDN_HEADS = 4
GDN_DK = 128
GDN_DV = 128
GDN_KW = GDN_HEADS * GDN_DK
GDN_VW = GDN_HEADS * GDN_DV
GDN_CONV_W = 2 * GDN_KW + GDN_VW
CONV_K = 5

CHUNK = 64

N_BRANCH = 3
BRANCH_W = MLA_HEADS * MLA_V

N_EXPERTS = 64
TOP_K = 8
N_GROUPS = 8
TOPK_GROUPS = 4
EXPERT_FF = 256
SHARED_FF = 256
ROUTED_SCALE = 2.5
MOE_BLOCK = 128

DEEPNORM_ALPHA = (2 * DEPTH) ** 0.25
DEEPNORM_BETA = (8 * DEPTH) ** -0.25
LN_EPS = 1e-6
RMS_EPS = 1e-6

IN_SIZES = (MLA_Q_LORA, MLA_KV_LORA + MLA_ROPE,
            HG_KW, HG_VW, HG_KW, HG_KW, HG_VW,
            GDN_CONV_W, GDN_VW, 2 * GDN_HEADS, 2 * GDN_HEADS,
            N_BRANCH * D_MODEL)
IN_WIDTH = sum(IN_SIZES)

kernel_name = "hybrid_mla_hgrn2_gdn_moe_dit"


def in_offsets():
    return [int(o) for o in np.cumsum(IN_SIZES)[:-1]]


def layer_norm(x, g, b):
    xf = x.astype(jnp.float32)
    mu = jnp.mean(xf, -1, keepdims=True)
    var = jnp.mean(jnp.square(xf - mu), -1, keepdims=True)
    return ((xf - mu) * lax.rsqrt(var + LN_EPS) * g + b).astype(x.dtype)


def rms_norm(x, g):
    xf = x.astype(jnp.float32)
    return (xf * lax.rsqrt(jnp.mean(xf * xf, -1, keepdims=True) + RMS_EPS) * g).astype(x.dtype)


def l2norm(x):
    xf = x.astype(jnp.float32)
    return xf * lax.rsqrt(jnp.sum(xf * xf, -1, keepdims=True) + 1e-6)


def modulate(x, shift, scale):
    return x * (1 + scale) + shift


def axial_rope_tables(n_tokens):
    rows = n_tokens // GRID_W
    row = jnp.broadcast_to(jnp.arange(rows, dtype=jnp.float32)[:, None], (rows, GRID_W)).reshape(-1)
    col = jnp.broadcast_to(jnp.arange(GRID_W, dtype=jnp.float32)[None, :], (rows, GRID_W)).reshape(-1)
    inv = ROPE_BASE ** (-jnp.arange(ROPE_FREQS, dtype=jnp.float32) / ROPE_FREQS)
    ang = jnp.stack([row[:, None] * inv, col[:, None] * inv], axis=1)
    return jnp.cos(ang), jnp.sin(ang)


def rope2d(x, cos, sin):
    xr = x.reshape(x.shape[:-1] + (2, 2, ROPE_FREQS))
    x1, x2 = xr[..., 0, :], xr[..., 1, :]
    cos = cos.astype(x.dtype)
    sin = sin.astype(x.dtype)
    return jnp.stack([x1 * cos - x2 * sin, x2 * cos + x1 * sin], axis=-2).reshape(x.shape)


def centred_dwconv(x, w):
    return lax.conv_general_dilated(x, w[:, None, :], window_strides=(1,),
                                    padding=[(CONV_K // 2, CONV_K // 2)],
                                    dimension_numbers=('NWC', 'WIO', 'NWC'),
                                    feature_group_count=x.shape[-1])


def to_chunks(t):
    B, L, H, d = t.shape
    return t.reshape(B, L // CHUNK, CHUNK, H, d).transpose(1, 0, 3, 2, 4)


def from_chunks(t):
    N, B, H, C, d = t.shape
    return t.transpose(1, 0, 3, 2, 4).reshape(B, N * C, H, d)


def gla_scan(q, k, v, log_f, s0):
    f32 = jnp.float32
    idx = jnp.arange(CHUNK)
    incl = (idx[:, None] >= idx[None, :])[:, :, None]

    def step(S, inp):
        qc, kc, vc, gc = inp
        cg = jnp.cumsum(gc, axis=-2)
        gl = cg[:, :, -1:, :]
        decay = jnp.exp(jnp.where(incl, cg[:, :, :, None, :] - cg[:, :, None, :, :], -jnp.inf))
        a = jnp.einsum('bhid,bhjd,bhijd->bhij', qc, kc, decay)
        o = jnp.einsum('bhcd,bhde->bhce', qc * jnp.exp(cg), S) + jnp.einsum('bhij,bhje->bhie', a, vc)
        S = jnp.exp(gl[:, :, 0, :, None]) * S + jnp.einsum('bhcd,bhce->bhde', kc * jnp.exp(gl - cg), vc)
        return S, o

    xs = (to_chunks(q.astype(f32)), to_chunks(k.astype(f32)), to_chunks(v.astype(f32)), to_chunks(log_f.astype(f32)))
    S, o = lax.scan(step, s0, xs)
    return from_chunks(o), S


def gated_delta_scan(q, k, v, g, beta, s0):
    f32 = jnp.float32
    qc = to_chunks(q.astype(f32)) * GDN_DK ** -0.5
    kc = to_chunks(k.astype(f32))
    vc = to_chunks(v.astype(f32))
    dv = vc.shape[-1]
    gc = jnp.cumsum(to_chunks(g.astype(f32)[..., None])[..., 0], axis=-1)
    bc = to_chunks(beta.astype(f32)[..., None])
    idx = jnp.arange(CHUNK)
    incl = idx[:, None] >= idx[None, :]
    strict = idx[:, None] > idx[None, :]
    decay = jnp.exp(jnp.where(incl, gc[..., :, None] - gc[..., None, :], -jnp.inf))
    kb = kc * bc
    lmat = jnp.where(strict, jnp.einsum('nbhid,nbhjd->nbhij', kb, kc) * decay, 0.0)
    rhs = jnp.concatenate([vc * bc, kb * jnp.exp(gc)[..., None]], axis=-1)
    sol = lax.linalg.triangular_solve(jnp.eye(CHUNK, dtype=f32) + lmat, rhs,
                                      left_side=True, lower=True, unit_diagonal=True)
    u, w = sol[..., :dv], sol[..., dv:]
    aqk = jnp.einsum('nbhid,nbhjd->nbhij', qc, kc) * decay
    qd = qc * jnp.exp(gc)[..., None]
    kd = kc * jnp.exp(gc[..., -1:] - gc)[..., None]
    gl = jnp.exp(gc[..., -1])

    def step(S, inp):
        u_n, w_n, qd_n, kd_n, a_n, gl_n = inp
        v_new = u_n - jnp.einsum('bhcd,bhde->bhce', w_n, S)
        o = jnp.einsum('bhcd,bhde->bhce', qd_n, S) + jnp.einsum('bhij,bhje->bhie', a_n, v_new)
        S = gl_n[..., None, None] * S + jnp.einsum('bhcd,bhce->bhde', kd_n, v_new)
        return S, o

    S, o = lax.scan(step, s0, (u, w, qd, kd, aqk, gl))
    return from_chunks(o), S


def two_stream_scan(scan_fn, ctx_args, lat_args, s0, reverse):
    flip = (lambda t: jnp.flip(t, axis=1)) if reverse else (lambda t: t)
    o_ctx, s_ctx = scan_fn(*[flip(t) for t in ctx_args], s0)
    o_lat, _ = scan_fn(*[flip(t) for t in lat_args], s_ctx)
    return flip(o_ctx), flip(o_lat)


def gated_head_norm(o, gate, w):
    B, L, H, d = o.shape
    n = o * lax.rsqrt(jnp.mean(o * o, -1, keepdims=True) + RMS_EPS) * w.astype(jnp.float32)
    return (n.reshape(B, L, H * d) * jax.nn.silu(gate.astype(jnp.float32))).astype(gate.dtype)


def mla_attend(q_nope, q_rope, k_nope, k_rope, v):
    s = (jnp.einsum('bqhd,bkhd->bhqk', q_nope, k_nope)
         + jnp.einsum('bqhd,bkd->bhqk', q_rope, k_rope)).astype(jnp.float32) * MLA_SCALE
    p = jax.nn.softmax(s, axis=-1).astype(v.dtype)
    return jnp.einsum('bhqk,bkhd->bqhd', p, v)


def stream_proj(h, lp, lb):
    f32 = jnp.float32
    B, L, _ = h.shape
    z = h @ lp['w_in']
    (qa, kva, hq, hi, hf_f, hf_b, hgate, gqkv, ggate, ga, gb, gates) = jnp.split(z, in_offsets(), axis=-1)
    q = (rms_norm(qa, lp['q_a_norm']) @ lp['w_q_b']).reshape(B, L, MLA_HEADS, MLA_NOPE + MLA_ROPE)
    kv = (rms_norm(kva[..., :MLA_KV_LORA], lp['kv_a_norm']) @ lp['w_kv_b']).reshape(B, L, MLA_HEADS, MLA_NOPE + MLA_V)
    f_fwd = lb[0] + (1.0 - lb[0]) * jax.nn.sigmoid(hf_f.astype(f32))
    f_bwd = lb[1] + (1.0 - lb[1]) * jax.nn.sigmoid(hf_b.astype(f32))
    hgh = lambda t: t.reshape(B, L, HG_HEADS, -1)
    hq_feat = hgh(jax.nn.silu(hq))
    hv = hgh(hi)
    qkv = jax.nn.silu(centred_dwconv(gqkv, lp['gdn_conv']))
    gq, gk, gv = jnp.split(qkv, [GDN_KW, 2 * GDN_KW], axis=-1)
    gdh = lambda t: t.reshape(B, L, GDN_HEADS, -1)
    gq, gk, gv = l2norm(gdh(gq)), l2norm(gdh(gk)), gdh(gv)
    a = ga.astype(f32).reshape(B, L, 2, GDN_HEADS)
    decay = -jnp.exp(lp['gdn_a_log'].astype(f32)) * jax.nn.softplus(a + lp['gdn_dt_bias'].astype(f32))
    beta = jax.nn.sigmoid(gb.astype(f32)).reshape(B, L, 2, GDN_HEADS)
    return dict(
        q_nope=q[..., :MLA_NOPE], q_rope=q[..., MLA_NOPE:],
        k_nope=kv[..., :MLA_NOPE], v=kv[..., MLA_NOPE:], k_rope=kva[..., MLA_KV_LORA:],
        hg_fwd=(hq_feat, hgh(1.0 - f_fwd), hv, hgh(jnp.log(f_fwd))),
        hg_bwd=(hq_feat, hgh(1.0 - f_bwd), hv, hgh(jnp.log(f_bwd))),
        hg_gate=hgate,
        gdn_fwd=(gq, gk, gv, decay[:, :, 0], beta[:, :, 0]),
        gdn_bwd=(gq, gk, gv, decay[:, :, 1], beta[:, :, 1]),
        gdn_gate=ggate, gates=gates)


def merge_branches(gates, outs, w_branch, w_out):
    o = jnp.stack(outs, axis=2)
    proj = jnp.einsum('blnw,nwd->blnd', o, w_branch)
    y = jnp.einsum('blnd,blnd->bld', jax.nn.sigmoid(gates).reshape(proj.shape), proj)
    return y @ w_out


def mixer(h_lat, h_ctx, cos, sin, lp, lb, want_ctx):
    pl = stream_proj(h_lat, lp, lb)
    pc = stream_proj(h_ctx, lp, lb)
    B, L, _ = h_lat.shape
    q_rope = rope2d(pl['q_rope'], cos[:, None], sin[:, None])
    k_nope = jnp.concatenate([pc['k_nope'], pl['k_nope']], axis=1)
    k_rope = jnp.concatenate([pc['k_rope'], rope2d(pl['k_rope'], cos, sin)], axis=1)
    v = jnp.concatenate([pc['v'], pl['v']], axis=1)
    nb = L // Q_BLOCK
    blocks = lambda t: t.reshape((B, nb, Q_BLOCK) + t.shape[2:]).swapaxes(0, 1)
    o = lax.map(lambda qs: mla_attend(qs[0], qs[1], k_nope, k_rope, v), (blocks(pl['q_nope']), blocks(q_rope)))
    mla_lat = o.swapaxes(0, 1).reshape(B, L, BRANCH_W)
    s0h = jnp.zeros((B, HG_HEADS, HG_DK, HG_DV), jnp.float32)
    hc_f, hl_f = two_stream_scan(gla_scan, pc['hg_fwd'], pl['hg_fwd'], s0h, False)
    hc_b, hl_b = two_stream_scan(gla_scan, pc['hg_bwd'], pl['hg_bwd'], s0h, True)
    s0g = jnp.zeros((B, GDN_HEADS, GDN_DK, GDN_DV), jnp.float32)
    gc_f, gl_f = two_stream_scan(gated_delta_scan, pc['gdn_fwd'], pl['gdn_fwd'], s0g, False)
    gc_b, gl_b = two_stream_scan(gated_delta_scan, pc['gdn_bwd'], pl['gdn_bwd'], s0g, True)
    hg_lat = gated_head_norm(hl_f + hl_b, pl['hg_gate'], lp['hg_norm'])
    gdn_lat = gated_head_norm(gl_f + gl_b, pl['gdn_gate'], lp['gdn_norm'])
    y_lat = merge_branches(pl['gates'], (mla_lat, hg_lat, gdn_lat), lp['w_branch'], lp['w_out'])
    if not want_ctx:
        return y_lat, None
    mla_ctx = mla_attend(pc['q_nope'], pc['q_rope'], pc['k_nope'], pc['k_rope'], pc['v'])
    mla_ctx = mla_ctx.reshape(B, -1, BRANCH_W)
    hg_ctx = gated_head_norm(hc_f + hc_b, pc['hg_gate'], lp['hg_norm'])
    gdn_ctx = gated_head_norm(gc_f + gc_b, pc['gdn_gate'], lp['gdn_norm'])
    y_ctx = merge_branches(pc['gates'], (mla_ctx, hg_ctx, gdn_ctx), lp['w_branch'], lp['w_out'])
    return y_lat, y_ctx


def moe_ffn(h, lp):
    f32 = jnp.float32
    T, D = h.shape
    scores = jax.nn.sigmoid((h @ lp['w_router']).astype(f32))
    sel = scores + lp['router_bias'].astype(f32)
    grp_score = lax.top_k(sel.reshape(T, N_GROUPS, N_EXPERTS // N_GROUPS), 2)[0].sum(-1)
    _, gidx = lax.top_k(grp_score, TOPK_GROUPS)
    rows = jnp.arange(T)[:, None]
    gmask = jnp.zeros((T, N_GROUPS), bool).at[rows, gidx].set(True)
    sel = jnp.where(jnp.repeat(gmask, N_EXPERTS // N_GROUPS, axis=1), sel, -jnp.inf)
    _, eidx = lax.top_k(sel, TOP_K)
    w = jnp.take_along_axis(scores, eidx, axis=1)
    w = w / jnp.sum(w, -1, keepdims=True) * ROUTED_SCALE
    comb = jnp.zeros((T, N_EXPERTS), f32).at[rows, eidx].set(w).astype(h.dtype)

    def block(args):
        hb, cb = args
        gate, up = jnp.split(jnp.einsum('td,edf->etf', hb, lp['w_gu']), 2, axis=-1)
        act = jax.nn.silu(gate) * up * cb.T[:, :, None]
        return jnp.einsum('etf,efd->td', act, lp['w_down'])

    nb = T // MOE_BLOCK
    routed = lax.map(block, (h.reshape(nb, MOE_BLOCK, D), comb.reshape(nb, MOE_BLOCK, N_EXPERTS))).reshape(T, D)
    gs, us = jnp.split(h @ lp['w_sh_gu'], 2, axis=-1)
    return routed + (jax.nn.silu(gs) * us) @ lp['w_sh_down']


def setup_inputs(seed: int = 0) -> dict:
    key = jax.random.key(seed)
    ks = jax.random.split(key, 32)
    D = D_MODEL
    nrm = lambda k, shape, s: jax.random.normal(k, shape, jnp.float32) * s
    dt = jnp.exp(jax.random.uniform(ks[15], (DEPTH, 2, GDN_HEADS), jnp.float32,
                                    minval=math.log(1e-3), maxval=math.log(1e-1)))
    return {
        "x": nrm(ks[0], (BATCH, SEQ, D), 1.0),
        "c": nrm(ks[1], (BATCH, D), 1.0),
        "ctx": nrm(ks[2], (BATCH, CTX_LEN, D), 1.0),
        "c_ctx": nrm(ks[3], (D,), 1.0),
        "w_mod": nrm(ks[4], (DEPTH, D, 6 * D), 0.5 * D ** -0.5),
        "b_mod": nrm(ks[5], (DEPTH, 6 * D), 0.02),
        "w_in": nrm(ks[6], (DEPTH, D, IN_WIDTH), D ** -0.5),
        "q_a_norm": 1.0 + nrm(ks[7], (DEPTH, MLA_Q_LORA), 0.02),
        "w_q_b": nrm(ks[8], (DEPTH, MLA_Q_LORA, MLA_HEADS * (MLA_NOPE + MLA_ROPE)), MLA_Q_LORA ** -0.5),
        "kv_a_norm": 1.0 + nrm(ks[9], (DEPTH, MLA_KV_LORA), 0.02),
        "w_kv_b": nrm(ks[10], (DEPTH, MLA_KV_LORA, MLA_HEADS * (MLA_NOPE + MLA_V)), MLA_KV_LORA ** -0.5),
        "hg_lb_logits": nrm(ks[11], (DEPTH, 2, HG_KW), 0.5),
        "hg_norm": 1.0 + nrm(ks[12], (DEPTH, HG_DV), 0.02),
        "gdn_conv": nrm(ks[13], (DEPTH, CONV_K, GDN_CONV_W), CONV_K ** -0.5),
        "gdn_a_log": jnp.log(jax.random.uniform(ks[14], (DEPTH, 2, GDN_HEADS), jnp.float32, minval=1.0, maxval=16.0)),
        "gdn_dt_bias": dt + jnp.log(-jnp.expm1(-dt)),
        "gdn_norm": 1.0 + nrm(ks[16], (DEPTH, GDN_DV), 0.02),
        "w_branch": nrm(ks[17], (DEPTH, N_BRANCH, BRANCH_W, D), BRANCH_W ** -0.5 * DEEPNORM_BETA),
        "w_out": nrm(ks[18], (DEPTH, D, D), D ** -0.5 * DEEPNORM_BETA),
        "ln1_g": 1.0 + nrm(ks[19], (DEPTH, D), 0.02),
        "ln1_b": nrm(ks[20], (DEPTH, D), 0.02),
        "ln2_g": 1.0 + nrm(ks[21], (DEPTH, D), 0.02),
        "ln2_b": nrm(ks[22], (DEPTH, D), 0.02),
        "w_router": nrm(ks[23], (DEPTH, D, N_EXPERTS), D ** -0.5),
        "router_bias": nrm(ks[24], (DEPTH, N_EXPERTS), 0.01),
        "w_gu": nrm(ks[25], (DEPTH, N_EXPERTS, D, 2 * EXPERT_FF), D ** -0.5),
        "w_down": nrm(ks[26], (DEPTH, N_EXPERTS, EXPERT_FF, D), EXPERT_FF ** -0.5 * DEEPNORM_BETA),
        "w_sh_gu": nrm(ks[27], (DEPTH, D, 2 * SHARED_FF), D ** -0.5),
        "w_sh_down": nrm(ks[28], (DEPTH, SHARED_FF, D), SHARED_FF ** -0.5 * DEEPNORM_BETA),
    }


def reference(x, c, ctx, c_ctx, w_mod, b_mod, w_in, q_a_norm, w_q_b, kv_a_norm, w_kv_b, hg_lb_logits, hg_norm,
              gdn_conv, gdn_a_log, gdn_dt_bias, gdn_norm, w_branch, w_out, ln1_g, ln1_b, ln2_g, ln2_b,
              w_router, router_bias, w_gu, w_down, w_sh_gu, w_sh_down):
    B, L, D = x.shape
    cos, sin = axial_rope_tables(L)
    lb_soft = jax.nn.softmax(hg_lb_logits.astype(jnp.float32), axis=0)
    lower_bounds = jnp.cumsum(lb_soft, axis=0) - lb_soft[0]
    silu_c = jax.nn.silu(c)
    silu_cc = jax.nn.silu(c_ctx)
    xc = ctx
    for l in range(DEPTH):
        last = l == DEPTH - 1
        lp = dict(w_in=w_in[l], q_a_norm=q_a_norm[l], w_q_b=w_q_b[l], kv_a_norm=kv_a_norm[l], w_kv_b=w_kv_b[l],
                  hg_norm=hg_norm[l], gdn_conv=gdn_conv[l], gdn_a_log=gdn_a_log[l], gdn_dt_bias=gdn_dt_bias[l],
                  gdn_norm=gdn_norm[l], w_branch=w_branch[l], w_out=w_out[l], w_router=w_router[l],
                  router_bias=router_bias[l], w_gu=w_gu[l], w_down=w_down[l], w_sh_gu=w_sh_gu[l],
                  w_sh_down=w_sh_down[l])
        mod = jnp.split((silu_c @ w_mod[l] + b_mod[l])[:, None, :], 6, axis=-1)
        modc = jnp.split((silu_cc @ w_mod[l] + b_mod[l])[None, None, :], 6, axis=-1)
        y, yc = mixer(modulate(x, mod[0], mod[1]), modulate(xc, modc[0], modc[1]), cos, sin, lp,
                      lower_bounds[l], not last)
        x = layer_norm(DEEPNORM_ALPHA * x + mod[2] * y, ln1_g[l], ln1_b[l])
        h = modulate(x, mod[3], mod[4])
        if last:
            f = moe_ffn(h.reshape(-1, D), lp).reshape(h.shape)
        else:
            xc = layer_norm(DEEPNORM_ALPHA * xc + modc[2] * yc, ln1_g[l], ln1_b[l])
            hc = modulate(xc, modc[3], modc[4])
            ff = moe_ffn(jnp.concatenate([h.reshape(-1, D), hc.reshape(-1, D)], axis=0), lp)
            f = ff[:B * L].reshape(h.shape)
            xc = layer_norm(DEEPNORM_ALPHA * xc + modc[5] * ff[B * L:].reshape(hc.shape), ln2_g[l], ln2_b[l])
        x = layer_norm(DEEPNORM_ALPHA * x + mod[5] * f, ln2_g[l], ln2_b[l])
    return x
```

```python
import functools
import math

import numpy as np
import jax
import jax.numpy as jnp
from jax import lax
from jax.experimental import pallas as pl
from jax.experimental.pallas import tpu as pltpu

F32 = jnp.float32
BF16 = jnp.bfloat16
HIGHEST = lax.Precision.HIGHEST

D_MODEL = 1024
DEPTH = 4
GRID_W = 64
MLA_HEADS = 8
MLA_Q_LORA = 384
MLA_KV_LORA = 256
MLA_NOPE = 64
MLA_ROPE = 32
MLA_V = 64
MLA_SCALE = (MLA_NOPE + MLA_ROPE) ** -0.5
ROPE_BASE = 10000.0
ROPE_FREQS = MLA_ROPE // 4
REC_HEADS = 4
REC_D = 128
REC_W = REC_HEADS * REC_D
CONV_K = 5
N_BRANCH = 3
BRANCH_W = 512
N_EXPERTS = 64
TOP_K = 8
N_GROUPS = 8
TOPK_GROUPS = 4
EXPERT_FF = 256
SHARED_FF = 256
ROUTED_SCALE = 2.5
DEEPNORM_ALPHA = (2 * DEPTH) ** 0.25
LN_EPS = 1e-6
RMS_EPS = 1e-6

TOK_TILE = 256
CHUNK = 64
MOD_ROWS = 16
VMEM_LIMIT = 56 * 1024 * 1024

ZM_GATES = 0
ZM_HG = 3 * D_MODEL
ZM_GGATE = ZM_HG + 5 * REC_W
ZM_WIDTH = ZM_GGATE + REC_W
ZA_WIDTH = MLA_Q_LORA + MLA_KV_LORA + 256
ZG_CONV = 3 * REC_W
ZG_WIDTH = ZG_CONV + 256


def _cp(sem, vmem=VMEM_LIMIT):
    return pltpu.CompilerParams(dimension_semantics=sem, vmem_limit_bytes=vmem)


def _dot(a, b, precision=None):
    return jnp.dot(a, b, preferred_element_type=F32, precision=precision)


def _dot_nt(a, b, precision=None):
    return lax.dot_general(a, b, (((1,), (1,)), ((), ())), preferred_element_type=F32, precision=precision)


def _sigmoid(x):
    return 1.0 / (1.0 + jnp.exp(-x))


def _silu(x):
    return x * _sigmoid(x)


def _mod_kernel(c_ref, w_ref, b_ref, o_ref):
    s = _silu(c_ref[...])
    o_ref[0] = _dot(s, w_ref[0], HIGHEST) + b_ref[0]


def _mod_all(c_all, w_mod, b_mod):
    tn = 1024
    n = w_mod.shape[-1]
    return pl.pallas_call(
        _mod_kernel,
        grid=(DEPTH, n // tn),
        in_specs=[
            pl.BlockSpec((MOD_ROWS, D_MODEL), lambda l, j: (0, 0)),
            pl.BlockSpec((1, D_MODEL, tn), lambda l, j: (l, 0, j)),
            pl.BlockSpec((1, 1, tn), lambda l, j: (l, 0, j)),
        ],
        out_specs=pl.BlockSpec((1, MOD_ROWS, tn), lambda l, j: (l, 0, j)),
        out_shape=jax.ShapeDtypeStruct((DEPTH, MOD_ROWS, n), F32),
        compiler_params=_cp(("arbitrary", "arbitrary")),
        name="mod_all",
    )(c_all, w_mod, b_mod.reshape(DEPTH, 1, n))


def _mod_spec(batch, k):
    return pl.BlockSpec((1, 1, D_MODEL), lambda b, t: (jnp.where(t == 0, batch, b), 0, k))


def _modulate_kernel(x_ref, sh_ref, sc_ref, o_ref):
    o_ref[0] = (x_ref[0] * (1.0 + sc_ref[0]) + sh_ref[0]).astype(o_ref.dtype)


def _modulate(x_all, mod_l):
    b, lt, d = x_all.shape
    return pl.pallas_call(
        _modulate_kernel,
        grid=(b, lt // TOK_TILE),
        in_specs=[
            pl.BlockSpec((1, TOK_TILE, d), lambda i, t: (i, t, 0)),
            _mod_spec(b, 0),
            _mod_spec(b, 1),
        ],
        out_specs=pl.BlockSpec((1, TOK_TILE, d), lambda i, t: (i, t, 0)),
        out_shape=jax.ShapeDtypeStruct((b, lt, d), BF16),
        compiler_params=_cp(("arbitrary", "arbitrary")),
        name="modulate",
    )(x_all, mod_l, mod_l)


def _proj_kernel(h_ref, w_ref, o_ref, *, rows):
    def body(r, carry):
        sl = pl.ds(pl.multiple_of(r * rows, rows), rows)
        o_ref[0, sl, :] = _dot(h_ref[0, sl, :], w_ref[...])
        return carry

    lax.fori_loop(0, h_ref.shape[1] // rows, body, 0)


def _proj(h, w, tn):
    b, lt, d = h.shape
    n = w.shape[1]
    return pl.pallas_call(
        functools.partial(_proj_kernel, rows=TOK_TILE),
        grid=(b, n // tn),
        in_specs=[
            pl.BlockSpec((1, lt, d), lambda i, j: (i, 0, 0)),
            pl.BlockSpec((d, tn), lambda i, j: (0, j)),
        ],
        out_specs=pl.BlockSpec((1, lt, tn), lambda i, j: (i, 0, j)),
        out_shape=jax.ShapeDtypeStruct((b, lt, n), F32),
        compiler_params=_cp(("arbitrary", "arbitrary")),
        name="proj",
    )(h, w)


def _rms(x, g):
    return x * lax.rsqrt(jnp.mean(x * x, axis=-1, keepdims=True) + RMS_EPS) * g


def _mla_prep_kernel(z_ref, qg_ref, kg_ref, wq1_ref, wq2_ref, wk_ref, wv_ref, ca_ref, sb_ref,
                     q_ref, k_ref, v_ref):
    z = z_ref[0]
    ca = ca_ref[...]
    sb = sb_ref[...]
    qn = _rms(z[:, :MLA_Q_LORA], qg_ref[...]).astype(BF16)
    qa = _dot(qn, wq1_ref[...])
    qb = _dot(qn, wq2_ref[...])
    kvn = _rms(z[:, MLA_Q_LORA:MLA_Q_LORA + MLA_KV_LORA], kg_ref[...]).astype(BF16)
    kn = _dot(kvn, wk_ref[...])
    v_ref[0] = _dot(kvn, wv_ref[...]).astype(v_ref.dtype)
    o = MLA_Q_LORA + MLA_KV_LORA
    kr = z[:, o:o + 128] * ca + z[:, o + 128:o + 256] * sb
    for h in range(MLA_HEADS):
        sl = slice(h * 128, (h + 1) * 128)
        q_ref[0, :, sl] = ((qa[:, sl] * ca + qb[:, sl] * sb) * MLA_SCALE).astype(q_ref.dtype)
        k_ref[0, :, sl] = (kn[:, sl] + kr).astype(k_ref.dtype)


def _mla_prep(z_mla, qg, kg, wq1, wq2, wk, wv, ca, sb):
    b, lt, _ = z_mla.shape
    full = lambda a: pl.BlockSpec(a.shape, lambda i, t: (0,) * a.ndim)
    tok = lambda w: pl.BlockSpec((1, TOK_TILE, w), lambda i, t: (i, t, 0))
    return pl.pallas_call(
        _mla_prep_kernel,
        grid=(b, lt // TOK_TILE),
        in_specs=[tok(ZA_WIDTH), full(qg), full(kg), full(wq1), full(wq2), full(wk), full(wv),
                  pl.BlockSpec((TOK_TILE, 128), lambda i, t: (t, 0)),
                  pl.BlockSpec((TOK_TILE, 128), lambda i, t: (t, 0))],
        out_specs=[tok(MLA_HEADS * 128), tok(MLA_HEADS * 128), tok(MLA_HEADS * MLA_V)],
        out_shape=[jax.ShapeDtypeStruct((b, lt, MLA_HEADS * 128), BF16),
                   jax.ShapeDtypeStruct((b, lt, MLA_HEADS * 128), BF16),
                   jax.ShapeDtypeStruct((b, lt, MLA_HEADS * MLA_V), BF16)],
        compiler_params=_cp(("arbitrary", "arbitrary")),
        name="mla_prep",
    )(z_mla, qg, kg, wq1, wq2, wk, wv, ca, sb)


def _attn_kernel(q_ref, k_ref, v_ref, o_ref, *, n_ctx):
    def attend(nk):
        outs = []
        for h in range(2):
            q = q_ref[0, :, h * 128:(h + 1) * 128]
            k = k_ref[0, :nk, h * 128:(h + 1) * 128]
            s = _dot_nt(q, k)
            p = jnp.exp(s - jnp.max(s, axis=1, keepdims=True))
            l = jnp.sum(p, axis=1, keepdims=True)
            o = _dot(p.astype(BF16), v_ref[0, :nk, h * MLA_V:(h + 1) * MLA_V])
            outs.append(o / l)
        o_ref[0] = jnp.concatenate(outs, axis=1).astype(o_ref.dtype)

    @pl.when(pl.program_id(2) == 0)
    def _():
        attend(n_ctx)

    @pl.when(pl.program_id(2) > 0)
    def _():
        attend(k_ref.shape[1])


def _attention(q, k, v, n_ctx):
    b, lt, _ = q.shape
    return pl.pallas_call(
        functools.partial(_attn_kernel, n_ctx=n_ctx),
        grid=(b, MLA_HEADS // 2, lt // TOK_TILE),
        in_specs=[
            pl.BlockSpec((1, TOK_TILE, 256), lambda i, h, t: (i, t, h)),
            pl.BlockSpec((1, lt, 256), lambda i, h, t: (i, 0, h)),
            pl.BlockSpec((1, lt, 2 * MLA_V), lambda i, h, t: (i, 0, h)),
        ],
        out_specs=pl.BlockSpec((1, TOK_TILE, 2 * MLA_V), lambda i, h, t: (i, t, h)),
        out_shape=jax.ShapeDtypeStruct((b, lt, MLA_HEADS * MLA_V), BF16),
        compiler_params=_cp(("arbitrary", "arbitrary", "arbitrary")),
        name="mla_attention",
    )(q, k, v)


def _rev_tile(s, n):
    return jnp.where(s == 0, 0, n - s)


def _time_index(shape, dim, reverse):
    i = lax.broadcasted_iota(jnp.int32, shape, dim)
    return (shape[dim] - 1 - i) if reverse else i


def _blk(i, size):
    return lax.shift_right_logical(i, jnp.int32(int(math.log2(size))))


def _gla_chunk(q, k, v, g, st_ref, reverse):
    c_len = q.shape[0]
    p = _time_index((c_len, REC_D), 0, reverse)
    ri = lax.broadcasted_iota(jnp.int32, (c_len, c_len), 0)
    ci = lax.broadcasted_iota(jnp.int32, (c_len, c_len), 1)
    c = g
    t = g
    a = jnp.zeros((c_len, c_len), F32)
    hb = 1
    while hb < c_len:
        odd = (_blk(p, hb) & 1) == 1
        qt = jnp.where(odd, q * jnp.exp(c), 0.0).astype(BF16)
        kt = jnp.where(odd, 0.0, k * jnp.exp(t - c)).astype(BF16)
        lv = _dot_nt(qt, kt)
        a = a + jnp.where(_blk(ri, 2 * hb) == _blk(ci, 2 * hb), lv, 0.0)
        t_lo = pltpu.roll(t, hb, 0)
        t_hi = pltpu.roll(t, c_len - hb, 0)
        prev, nxt = (t_hi, t_lo) if reverse else (t_lo, t_hi)
        c = c + jnp.where(odd, prev, 0.0)
        t = t + jnp.where(odd, prev, nxt)
        hb *= 2
    st = st_ref[...]
    qk = jnp.sum(q * k, axis=1, keepdims=True)
    o = (_dot_nt((q * jnp.exp(c)).astype(BF16), st.astype(BF16))
         + _dot(a.astype(BF16), v.astype(BF16)) + qk * v)
    kd = (k * jnp.exp(t - c)).astype(BF16)
    st_ref[...] = st * jnp.exp(t[0:1, :]) + _dot(v.T.astype(BF16), kd)
    return o


def _hgrn2_kernel(qf_ref, vf_ref, ff_ref, qb_ref, vb_ref, fb_ref, lb_ref, of_ref, ob_ref, st_ref):
    @pl.when(pl.program_id(1) == 0)
    def _():
        st_ref[...] = jnp.zeros(st_ref.shape, F32)

    n_chunks = TOK_TILE // CHUNK

    def body(ci, carry):
        for d, (q_ref, v_ref, f_ref, o_ref) in enumerate(((qf_ref, vf_ref, ff_ref, of_ref),
                                                           (qb_ref, vb_ref, fb_ref, ob_ref))):
            cc = ci if d == 0 else n_chunks - 1 - ci
            rows = pl.ds(pl.multiple_of(cc * CHUNK, CHUNK), CHUNK)
            for h in range(REC_HEADS):
                cols = slice(h * REC_D, (h + 1) * REC_D)
                lb = lb_ref[d, :, cols]
                f = lb + (1.0 - lb) * _sigmoid(f_ref[0, rows, cols])
                o_ref[0, rows, cols] = _gla_chunk(_silu(q_ref[0, rows, cols]), 1.0 - f, v_ref[0, rows, cols],
                                                  jnp.log(f), st_ref.at[d, h], reverse=(d == 1))
        return carry

    lax.fori_loop(0, n_chunks, body, 0)


def _hgrn2(z_main, lb):
    b, lt, _ = z_main.shape
    n = lt // TOK_TILE
    base = ZM_HG // REC_W
    fwd = lambda k: pl.BlockSpec((1, TOK_TILE, REC_W), lambda i, s: (i, s, base + k))
    bwd = lambda k: pl.BlockSpec((1, TOK_TILE, REC_W), lambda i, s: (i, _rev_tile(s, n), base + k))
    return pl.pallas_call(
        _hgrn2_kernel,
        grid=(b, n),
        in_specs=[fwd(0), fwd(1), fwd(2), bwd(0), bwd(1), bwd(3),
                  pl.BlockSpec((2, 1, REC_W), lambda i, s: (0, 0, 0))],
        out_specs=[pl.BlockSpec((1, TOK_TILE, REC_W), lambda i, s: (i, s, 0)),
                   pl.BlockSpec((1, TOK_TILE, REC_W), lambda i, s: (i, _rev_tile(s, n), 0))],
        out_shape=[jax.ShapeDtypeStruct((b, lt, REC_W), F32)] * 2,
        scratch_shapes=[pltpu.VMEM((2, REC_HEADS, REC_D, REC_D), F32)],
        compiler_params=_cp(("arbitrary", "arbitrary")),
        name="hgrn2_scan",
    )(z_main, z_main, z_main, z_main, z_main, z_main, lb)


def _gdn_prep_kernel(x_ref, w_ref, o_ref, *, n_ctx):
    x = x_ref[0]
    lt = x.shape[0]
    t = lax.broadcasted_iota(jnp.int32, x.shape, 0)
    lo = jnp.where(t < n_ctx, 0, n_ctx)
    hi = jnp.where(t < n_ctx, n_ctx, lt)
    acc = x * w_ref[CONV_K // 2:CONV_K // 2 + 1, :]
    for kk in range(CONV_K):
        off = kk - CONV_K // 2
        if off == 0:
            continue
        xs = pltpu.roll(x, (-off) % lt, 0)
        ok = jnp.logical_and(t + off >= lo, t + off < hi)
        acc = acc + jnp.where(ok, xs, 0.0) * w_ref[kk:kk + 1, :]
    y = _silu(acc)
    inv = lax.rsqrt(jnp.sum(y * y, axis=1, keepdims=True) + 1e-6)
    is_qk = pl.program_id(1) < 2 * REC_HEADS
    o_ref[0] = y * jnp.where(is_qk, inv, 1.0)


def _gdn_prep(z_gdn, conv_w, n_ctx):
    b, lt, _ = z_gdn.shape
    return pl.pallas_call(
        functools.partial(_gdn_prep_kernel, n_ctx=n_ctx),
        grid=(b, ZG_CONV // REC_D),
        in_specs=[pl.BlockSpec((1, lt, REC_D), lambda i, j: (i, 0, j)),
                  pl.BlockSpec((CONV_K, REC_D), lambda i, j: (0, j))],
        out_specs=pl.BlockSpec((1, lt, REC_D), lambda i, j: (i, 0, j)),
        out_shape=jax.ShapeDtypeStruct((b, lt, ZG_CONV), F32),
        compiler_params=_cp(("arbitrary", "arbitrary")),
        name="gdn_prep",
    )(z_gdn, conv_w)


def _gdn_chunk(q, k, v, gcol, grow, glast, beta, s_ref, reverse):
    c_len = q.shape[0]
    ri = _time_index((c_len, c_len), 0, reverse)
    ci = _time_index((c_len, c_len), 1, reverse)
    dec = jnp.where(ri >= ci, jnp.exp(jnp.minimum(gcol - grow, 0.0)), 0.0)
    kb = k * beta
    lm = jnp.where(ri > ci, _dot_nt(kb, k, HIGHEST) * dec, 0.0)
    eye = jnp.where(ri == ci, 1.0, 0.0)
    tinv = eye - jnp.where(_blk(ri, 2) == _blk(ci, 2), lm, 0.0)
    hb = 2
    while hb < c_len:
        off = jnp.where(_blk(ri, 2 * hb) == _blk(ci, 2 * hb), jnp.where(_blk(ri, hb) == _blk(ci, hb), 0.0, lm), 0.0)
        tinv = tinv - _dot(_dot(tinv, off, HIGHEST), tinv, HIGHEST)
        hb *= 2
    eg = jnp.exp(gcol)
    u = _dot(tinv, v * beta, HIGHEST)
    w = _dot(tinv, kb * eg, HIGHEST)
    qs = q * (REC_D ** -0.5)
    aqk = _dot_nt(qs.astype(BF16), k.astype(BF16)) * dec
    s = s_ref[...]
    sb = s.astype(BF16)
    v_new = u - _dot(w.astype(BF16), sb)
    o = _dot((qs * eg).astype(BF16), sb) + _dot(aqk.astype(BF16), v_new.astype(BF16))
    kd = k * jnp.exp(glast - gcol)
    s_ref[...] = s * jnp.exp(glast) + _dot(kd.T.astype(BF16), v_new.astype(BF16))
    return o


def _gdn_kernel(qf_ref, kf_ref, vf_ref, gf_ref, qb_ref, kb_ref, vb_ref, gb_ref, par_ref, of_ref, ob_ref, s_ref):
    @pl.when(pl.program_id(1) == 0)
    def _():
        s_ref[...] = jnp.zeros(s_ref.shape, F32)

    n_chunks = TOK_TILE // CHUNK
    neg_a = -jnp.exp(par_ref[0:1, :])
    dt_bias = par_ref[1:2, :]
    r2 = lax.broadcasted_iota(jnp.int32, (CHUNK, CHUNK), 0)
    c2 = lax.broadcasted_iota(jnp.int32, (CHUNK, CHUNK), 1)

    def body(ci, carry):
        for d, (q_ref, k_ref, v_ref, g_ref, o_ref) in enumerate(((qf_ref, kf_ref, vf_ref, gf_ref, of_ref),
                                                                  (qb_ref, kb_ref, vb_ref, gb_ref, ob_ref))):
            cc = ci if d == 0 else n_chunks - 1 - ci
            rows = pl.ds(pl.multiple_of(cc * CHUNK, CHUNK), CHUNK)
            ab = g_ref[0, rows, :]
            xa = ab + dt_bias
            softplus = jnp.maximum(xa, 0.0) + jnp.log(1.0 + jnp.exp(-jnp.abs(xa)))
            tri = jnp.where((r2 <= c2) if d == 1 else (r2 >= c2), 1.0, 0.0)
            gc = _dot(tri, neg_a * softplus, HIGHEST)
            gct = gc.T
            beta = _sigmoid(ab)
            last = 0 if d == 1 else CHUNK - 1
            for h in range(REC_HEADS):
                cols = slice(h * REC_D, (h + 1) * REC_D)
                ln = d * REC_HEADS + h
                o_ref[0, rows, cols] = _gdn_chunk(
                    q_ref[0, rows, cols], k_ref[0, rows, cols], v_ref[0, rows, cols],
                    gc[:, ln:ln + 1], gct[ln:ln + 1, :], gc[last:last + 1, ln:ln + 1],
                    beta[:, 2 * REC_HEADS + ln:2 * REC_HEADS + ln + 1], s_ref.at[d, h], reverse=(d == 1))
        return carry

    lax.fori_loop(0, n_chunks, body, 0)


def _gdn(qkv, z_gdn, par):
    b, lt, _ = qkv.shape
    n = lt // TOK_TILE
    fwd = lambda k: pl.BlockSpec((1, TOK_TILE, REC_W), lambda i, s: (i, s, k))
    bwd = lambda k: pl.BlockSpec((1, TOK_TILE, REC_W), lambda i, s: (i, _rev_tile(s, n), k))
    gcol = ZG_CONV // 128
    return pl.pallas_call(
        _gdn_kernel,
        grid=(b, n),
        in_specs=[fwd(0), fwd(1), fwd(2), pl.BlockSpec((1, TOK_TILE, 128), lambda i, s: (i, s, gcol)),
                  bwd(0), bwd(1), bwd(2), pl.BlockSpec((1, TOK_TILE, 128), lambda i, s: (i, _rev_tile(s, n), gcol)),
                  pl.BlockSpec((8, 128), lambda i, s: (0, 0))],
        out_specs=[pl.BlockSpec((1, TOK_TILE, REC_W), lambda i, s: (i, s, 0)),
                   pl.BlockSpec((1, TOK_TILE, REC_W), lambda i, s: (i, _rev_tile(s, n), 0))],
        out_shape=[jax.ShapeDtypeStruct((b, lt, REC_W), F32)] * 2,
        scratch_shapes=[pltpu.VMEM((2, REC_HEADS, REC_D, REC_D), F32)],
        compiler_params=_cp(("arbitrary", "arbitrary")),
        name="gdn_scan",
    )(qkv, qkv, qkv, z_gdn, qkv, qkv, qkv, z_gdn, par)


def _layer_norm(x, g, b):
    mu = jnp.mean(x, axis=-1, keepdims=True)
    xc = x - mu
    var = jnp.mean(xc * xc, axis=-1, keepdims=True)
    return xc * lax.rsqrt(var + LN_EPS) * g + b


def _head_norm_gate(o, gate, w):
    outs = []
    for h in range(REC_HEADS):
        cols = slice(h * REC_D, (h + 1) * REC_D)
        oh = o[:, cols]
        n = oh * lax.rsqrt(jnp.mean(oh * oh, axis=-1, keepdims=True) + RMS_EPS) * w
        outs.append(n * _silu(gate[:, cols]))
    return jnp.concatenate(outs, axis=1)


def _merge_kernel(mla_ref, hf_ref, hb_ref, gf_ref, gb_ref, gates_ref, hgate_ref, ggate_ref, x_ref,
                  m2_ref, m3_ref, m4_ref, hw_ref, gw_ref, wb_ref, wo_ref, lg_ref, lbias_ref, wr_ref,
                  x1_ref, h2_ref, lt_ref):
    hg = _head_norm_gate(hf_ref[0] + hb_ref[0], hgate_ref[0], hw_ref[...]).astype(BF16)
    gd = _head_norm_gate(gf_ref[0] + gb_ref[0], ggate_ref[0], gw_ref[...]).astype(BF16)
    y = jnp.zeros((TOK_TILE, D_MODEL), F32)
    for n, o in enumerate((mla_ref[0], hg, gd)):
        y = y + _sigmoid(gates_ref[0, :, n * D_MODEL:(n + 1) * D_MODEL]) * _dot(o, wb_ref[n])
    y = _dot(y.astype(BF16), wo_ref[...])
    x1 = _layer_norm(DEEPNORM_ALPHA * x_ref[0] + m2_ref[0] * y, lg_ref[...], lbias_ref[...])
    x1_ref[0] = x1
    h2 = x1 * (1.0 + m4_ref[0]) + m3_ref[0]
    h2_ref[0] = h2.astype(h2_ref.dtype)
    lt_ref[...] = _dot_nt(wr_ref[...], h2, HIGHEST)


def _merge(mla_o, hg_f, hg_b, gd_f, gd_b, z_main, x_all, mod_l, hg_w, gdn_w, wb, wo, ln_g, ln_b, wr_t):
    b, lt, d = x_all.shape
    n = lt // TOK_TILE
    tok = lambda w, k=0: pl.BlockSpec((1, TOK_TILE, w), lambda i, t: (i, t, k))
    full = lambda a: pl.BlockSpec(a.shape, lambda i, t: (0,) * a.ndim)
    return pl.pallas_call(
        _merge_kernel,
        grid=(b, n),
        in_specs=[tok(BRANCH_W), tok(REC_W), tok(REC_W), tok(REC_W), tok(REC_W),
                  tok(N_BRANCH * D_MODEL, 0), tok(REC_W, ZM_HG // REC_W + 4), tok(REC_W, ZM_GGATE // REC_W),
                  tok(d), _mod_spec(b, 2), _mod_spec(b, 3), _mod_spec(b, 4),
                  full(hg_w), full(gdn_w), full(wb), full(wo), full(ln_g), full(ln_b), full(wr_t)],
        out_specs=[tok(d), tok(d), pl.BlockSpec((N_EXPERTS, TOK_TILE), lambda i, t: (0, i * n + t))],
        out_shape=[jax.ShapeDtypeStruct((b, lt, d), F32), jax.ShapeDtypeStruct((b, lt, d), BF16),
                   jax.ShapeDtypeStruct((N_EXPERTS, b * lt), F32)],
        compiler_params=_cp(("arbitrary", "arbitrary")),
        name="merge",
    )(mla_o, hg_f, hg_b, gd_f, gd_b, z_main, z_main, z_main, x_all, mod_l, mod_l, mod_l,
      hg_w, gdn_w, wb, wo, ln_g, ln_b, wr_t)


def _first_max(x, idx, axes):
    m = x
    for ax in axes:
        m = jnp.max(m, axis=ax, keepdims=True)
    first = jnp.where(x == m, idx, jnp.int32(2 ** 30))
    for ax in axes:
        first = jnp.min(first, axis=ax, keepdims=True)
    return m, first


def _route_kernel(lt_ref, bias_ref, comb_ref):
    n_tok = lt_ref.shape[1]
    per = N_EXPERTS // N_GROUPS
    scores = _sigmoid(lt_ref[...]).reshape(N_GROUPS, per, n_tok)
    sel = scores + bias_ref[...]
    ig = lax.broadcasted_iota(jnp.int32, sel.shape, 0)
    ij = lax.broadcasted_iota(jnp.int32, sel.shape, 1)
    top1, a1 = _first_max(sel, ij, (1,))
    top2 = jnp.max(jnp.where(ij == a1, -jnp.inf, sel), axis=1, keepdims=True)
    grp = top1 + top2
    igg = lax.broadcasted_iota(jnp.int32, grp.shape, 0)
    gsel = jnp.zeros(grp.shape, F32)
    for _ in range(TOPK_GROUPS):
        _, a = _first_max(grp, igg, (0,))
        hit = igg == a
        gsel = jnp.where(hit, 1.0, gsel)
        grp = jnp.where(hit, -jnp.inf, grp)
    cur = jnp.where(gsel > 0.5, sel, -jnp.inf)
    ie = ig * per + ij
    esel = jnp.zeros(cur.shape, F32)
    for _ in range(TOP_K):
        _, a = _first_max(cur, ie, (1, 0))
        hit = ie == a
        esel = jnp.where(hit, 1.0, esel)
        cur = jnp.where(hit, -jnp.inf, cur)
    w = scores * esel
    tot = jnp.sum(jnp.sum(w, axis=1, keepdims=True), axis=0, keepdims=True)
    w = w / tot * ROUTED_SCALE
    comb_ref[...] = w.reshape(N_EXPERTS, n_tok).T


def _route(logits_t, bias):
    n_tok = logits_t.shape[1]
    tt = 512
    return pl.pallas_call(
        _route_kernel,
        grid=(n_tok // tt,),
        in_specs=[pl.BlockSpec((N_EXPERTS, tt), lambda i: (0, i)),
                  pl.BlockSpec((N_GROUPS, N_EXPERTS // N_GROUPS, 1), lambda i: (0, 0, 0))],
        out_specs=pl.BlockSpec((tt, N_EXPERTS), lambda i: (i, 0)),
        out_shape=jax.ShapeDtypeStruct((n_tok, N_EXPERTS), F32),
        compiler_params=_cp(("arbitrary",)),
        name="route",
    )(logits_t, bias)


def _moe_kernel(h_ref, comb_ref, wgu_ref, wd_ref, wsgu_ref, wsd_ref, o_ref):
    e = pl.program_id(1)
    h = h_ref[...]

    @pl.when(e == 0)
    def _():
        gu = _dot(h, wsgu_ref[...])
        act = _silu(gu[:, :SHARED_FF]) * gu[:, SHARED_FF:]
        o_ref[...] = _dot(act.astype(BF16), wsd_ref[...])

    gu = _dot(h, wgu_ref[0])
    pick = jnp.where(lax.broadcasted_iota(jnp.int32, (N_EXPERTS, EXPERT_FF), 0) == e, 1.0, 0.0)
    cw = _dot(comb_ref[...], pick, HIGHEST)
    act = _silu(gu[:, :EXPERT_FF]) * gu[:, EXPERT_FF:] * cw
    o_ref[...] += _dot(act.astype(BF16), wd_ref[0])


def _moe(h2, comb, wgu, wd, wsgu, wsd, tm):
    n_tok, d = h2.shape
    return pl.pallas_call(
        _moe_kernel,
        grid=(n_tok // tm, N_EXPERTS),
        in_specs=[pl.BlockSpec((tm, d), lambda i, e: (i, 0)),
                  pl.BlockSpec((tm, N_EXPERTS), lambda i, e: (i, 0)),
                  pl.BlockSpec((1, d, 2 * EXPERT_FF), lambda i, e: (e, 0, 0)),
                  pl.BlockSpec((1, EXPERT_FF, d), lambda i, e: (e, 0, 0)),
                  pl.BlockSpec(wsgu.shape, lambda i, e: (0, 0)),
                  pl.BlockSpec(wsd.shape, lambda i, e: (0, 0))],
        out_specs=pl.BlockSpec((tm, d), lambda i, e: (i, 0)),
        out_shape=jax.ShapeDtypeStruct((n_tok, d), F32),
        compiler_params=_cp(("arbitrary", "arbitrary")),
        name="moe",
    )(h2, comb, wgu, wd, wsgu, wsd)


def _ffn_norm_kernel(x_ref, f_ref, m5_ref, g_ref, b_ref, o_ref):
    o_ref[0] = _layer_norm(DEEPNORM_ALPHA * x_ref[0] + m5_ref[0] * f_ref[0], g_ref[...], b_ref[...])


def _ffn_norm(x1, f, mod_l, ln_g, ln_b):
    b, lt, d = x1.shape
    tok = pl.BlockSpec((1, TOK_TILE, d), lambda i, t: (i, t, 0))
    full = lambda a: pl.BlockSpec(a.shape, lambda i, t: (0,) * a.ndim)
    return pl.pallas_call(
        _ffn_norm_kernel,
        grid=(b, lt // TOK_TILE),
        in_specs=[tok, tok, _mod_spec(b, 5), full(ln_g), full(ln_b)],
        out_specs=tok,
        out_shape=jax.ShapeDtypeStruct((b, lt, d), F32),
        compiler_params=_cp(("arbitrary", "arbitrary")),
        name="ffn_norm",
    )(x1, f, mod_l, ln_g, ln_b)


def _rope_tables(n_ctx, n_lat):
    rows = n_lat // GRID_W
    row = jnp.broadcast_to(jnp.arange(rows, dtype=F32)[:, None], (rows, GRID_W)).reshape(-1)
    col = jnp.broadcast_to(jnp.arange(GRID_W, dtype=F32)[None, :], (rows, GRID_W)).reshape(-1)
    inv = ROPE_BASE ** (-jnp.arange(ROPE_FREQS, dtype=F32) / ROPE_FREQS)
    ang = jnp.stack([row[:, None] * inv, col[:, None] * inv], axis=1)
    cos = jnp.cos(ang)[:, :, None, :]
    sin = jnp.sin(ang)[:, :, None, :]
    cos32 = jnp.broadcast_to(cos, (n_lat, 2, 2, ROPE_FREQS)).reshape(n_lat, MLA_ROPE)
    sin32 = jnp.concatenate([-sin, sin], axis=2).reshape(n_lat, MLA_ROPE)
    cos32 = jnp.concatenate([jnp.ones((n_ctx, MLA_ROPE), F32), cos32], axis=0)
    sin32 = jnp.concatenate([jnp.zeros((n_ctx, MLA_ROPE), F32), sin32], axis=0)
    lt = n_ctx + n_lat
    ca = jnp.concatenate([jnp.ones((lt, MLA_NOPE), F32), cos32, jnp.zeros((lt, 32), F32)], axis=1)
    sb = jnp.concatenate([jnp.zeros((lt, MLA_NOPE), F32), sin32, jnp.zeros((lt, 32), F32)], axis=1)
    return ca, sb


def _rope_partner():
    idx = np.arange(MLA_ROPE).reshape(2, 2, ROPE_FREQS)
    return idx[:, ::-1, :].reshape(-1)


def _pack_layer(w_in, w_q_b, w_kv_b):
    sizes = (MLA_Q_LORA, MLA_KV_LORA + MLA_ROPE, REC_W, REC_W, REC_W, REC_W, REC_W,
             3 * REC_W, REC_W, 2 * REC_HEADS, 2 * REC_HEADS, N_BRANCH * D_MODEL)
    offs = np.cumsum((0,) + sizes)
    seg = lambda i: w_in[:, offs[i]:offs[i + 1]]
    d = w_in.shape[0]
    zeros = lambda n: jnp.zeros((d, n), w_in.dtype)
    partner = _rope_partner()
    kva = seg(1)
    k_rope = kva[:, MLA_KV_LORA:]
    w_main = jnp.concatenate([seg(11), seg(2), seg(3), seg(4), seg(5), seg(6), seg(8)], axis=1)
    w_mla = jnp.concatenate([seg(0), kva[:, :MLA_KV_LORA],
                             zeros(MLA_NOPE), k_rope, zeros(32),
                             zeros(MLA_NOPE), k_rope[:, partner], zeros(32)], axis=1)
    w_gdn = jnp.concatenate([seg(7), seg(9), seg(10), zeros(256 - 4 * REC_HEADS)], axis=1)
    r = w_q_b.shape[0]
    qb = w_q_b.reshape(r, MLA_HEADS, MLA_NOPE + MLA_ROPE)
    zq = lambda n: jnp.zeros((r, MLA_HEADS, n), w_q_b.dtype)
    wq1 = jnp.concatenate([qb, zq(32)], axis=2).reshape(r, MLA_HEADS * 128)
    wq2 = jnp.concatenate([zq(MLA_NOPE), qb[:, :, MLA_NOPE:][:, :, partner], zq(32)], axis=2).reshape(r, MLA_HEADS * 128)
    rk = w_kv_b.shape[0]
    kvb = w_kv_b.reshape(rk, MLA_HEADS, MLA_NOPE + MLA_V)
    wk = jnp.concatenate([kvb[:, :, :MLA_NOPE], jnp.zeros((rk, MLA_HEADS, 64), w_kv_b.dtype)], axis=2)
    wk = wk.reshape(rk, MLA_HEADS * 128)
    wv = kvb[:, :, MLA_NOPE:].reshape(rk, MLA_HEADS * MLA_V)
    bf = lambda a: a.astype(BF16)
    return bf(w_main), bf(w_mla), bf(w_gdn), bf(wq1), bf(wq2), bf(wk), bf(wv)


def kernel(x, c, ctx, c_ctx, w_mod, b_mod, w_in, q_a_norm, w_q_b, kv_a_norm, w_kv_b, hg_lb_logits, hg_norm,
           gdn_conv, gdn_a_log, gdn_dt_bias, gdn_norm, w_branch, w_out, ln1_g, ln1_b, ln2_g, ln2_b,
           w_router, router_bias, w_gu, w_down, w_sh_gu, w_sh_down):
    batch, n_lat, d = x.shape
    n_ctx = ctx.shape[1]
    assert n_ctx == TOK_TILE and n_lat % TOK_TILE == 0 and batch < MOD_ROWS and d == D_MODEL
    lt = n_ctx + n_lat

    c_all = jnp.zeros((MOD_ROWS, d), F32).at[:batch].set(c).at[batch].set(c_ctx)
    mod = _mod_all(c_all, w_mod, b_mod).reshape(DEPTH, MOD_ROWS, 1, 6 * d)
    ca, sb = _rope_tables(n_ctx, n_lat)
    lb_soft = jax.nn.softmax(hg_lb_logits.astype(F32), axis=0)
    lower = (jnp.cumsum(lb_soft, axis=0) - lb_soft[0]).reshape(DEPTH, 2, 1, REC_W)
    row = lambda a: a.reshape(1, -1)

    x_all = jnp.concatenate([ctx, x], axis=1)
    for l in range(DEPTH):
        w_main, w_mla, w_gdn, wq1, wq2, wk, wv = _pack_layer(w_in[l], w_q_b[l], w_kv_b[l])
        mod_l = mod[l]
        h = _modulate(x_all, mod_l)
        z_main = _proj(h, w_main, 512)
        z_mla = _proj(h, w_mla, ZA_WIDTH)
        z_gdn = _proj(h, w_gdn, ZG_WIDTH // 2)

        q, k, v = _mla_prep(z_mla, row(q_a_norm[l]), row(kv_a_norm[l]), wq1, wq2, wk, wv, ca, sb)
        mla_o = _attention(q, k, v, n_ctx)

        hg_f, hg_b = _hgrn2(z_main, lower[l])

        qkv = _gdn_prep(z_gdn, gdn_conv[l], n_ctx)
        par = jnp.zeros((8, 128), F32)
        par = par.at[0, :2 * REC_HEADS].set(gdn_a_log[l].reshape(-1))
        par = par.at[1, :2 * REC_HEADS].set(gdn_dt_bias[l].reshape(-1))
        gd_f, gd_b = _gdn(qkv, z_gdn, par)

        x1, h2, logits_t = _merge(mla_o, hg_f, hg_b, gd_f, gd_b, z_main, x_all, mod_l,
                                  row(hg_norm[l]), row(gdn_norm[l]), w_branch[l].astype(BF16),
                                  w_out[l].astype(BF16), row(ln1_g[l]), row(ln1_b[l]), w_router[l].T)
        comb = _route(logits_t, router_bias[l].reshape(N_GROUPS, N_EXPERTS // N_GROUPS, 1))
        f = _moe(h2.reshape(batch * lt, d), comb, w_gu[l].astype(BF16), w_down[l].astype(BF16),
                 w_sh_gu[l].astype(BF16), w_sh_down[l].astype(BF16), 1024 if (batch * lt) % 1024 == 0 else 512)
        x_all = _ffn_norm(x1, f.reshape(batch, lt, d), mod_l, row(ln2_g[l]), row(ln2_b[l]))
    return x_all[:, n_ctx:, :]
```

```python
import functools
import math

import numpy as np
import jax
import jax.numpy as jnp
from jax import lax
from jax.experimental import pallas as pl
from jax.experimental.pallas import tpu as pltpu

F32 = jnp.float32
BF16 = jnp.bfloat16
HIGHEST = lax.Precision.HIGHEST

D_MODEL = 1024
DEPTH = 4
GRID_W = 64
MLA_HEADS = 8
MLA_Q_LORA = 384
MLA_KV_LORA = 256
MLA_NOPE = 64
MLA_ROPE = 32
MLA_V = 64
MLA_SCALE = (MLA_NOPE + MLA_ROPE) ** -0.5
ROPE_BASE = 10000.0
ROPE_FREQS = MLA_ROPE // 4
REC_HEADS = 4
REC_D = 128
REC_W = REC_HEADS * REC_D
CONV_K = 5
N_BRANCH = 3
BRANCH_W = 512
N_EXPERTS = 64
TOP_K = 8
N_GROUPS = 8
TOPK_GROUPS = 4
EXPERT_FF = 256
SHARED_FF = 256
ROUTED_SCALE = 2.5
DEEPNORM_ALPHA = (2 * DEPTH) ** 0.25
LN_EPS = 1e-6
RMS_EPS = 1e-6

TOK_TILE = 256
CHUNK = 64
MOD_ROWS = 16
VMEM_LIMIT = 56 * 1024 * 1024

ZM_GATES = 0
ZM_HG = 3 * D_MODEL
ZM_GGATE = ZM_HG + 5 * REC_W
ZM_WIDTH = ZM_GGATE + REC_W
ZA_WIDTH = MLA_Q_LORA + MLA_KV_LORA + 256
ZG_CONV = 3 * REC_W
ZG_WIDTH = ZG_CONV + 256


def _cp(sem, vmem=VMEM_LIMIT):
    return pltpu.CompilerParams(dimension_semantics=sem, vmem_limit_bytes=vmem)


def _dot(a, b, precision=None):
    return jnp.dot(a, b, preferred_element_type=F32, precision=precision)


def _dot_nt(a, b, precision=None):
    return lax.dot_general(a, b, (((1,), (1,)), ((), ())), preferred_element_type=F32, precision=precision)


def _sigmoid(x):
    return 1.0 / (1.0 + jnp.exp(-x))


def _silu(x):
    return x * _sigmoid(x)


def _mod_kernel(c_ref, w_ref, b_ref, o_ref):
    s = _silu(c_ref[...])
    o_ref[0] = _dot(s, w_ref[0], HIGHEST) + b_ref[0]


def _mod_all(c_all, w_mod, b_mod):
    tn = 1024
    n = w_mod.shape[-1]
    return pl.pallas_call(
        _mod_kernel,
        grid=(DEPTH, n // tn),
        in_specs=[
            pl.BlockSpec((MOD_ROWS, D_MODEL), lambda l, j: (0, 0)),
            pl.BlockSpec((1, D_MODEL, tn), lambda l, j: (l, 0, j)),
            pl.BlockSpec((1, 1, tn), lambda l, j: (l, 0, j)),
        ],
        out_specs=pl.BlockSpec((1, MOD_ROWS, tn), lambda l, j: (l, 0, j)),
        out_shape=jax.ShapeDtypeStruct((DEPTH, MOD_ROWS, n), F32),
        compiler_params=_cp(("arbitrary", "arbitrary")),
        name="mod_all",
    )(c_all, w_mod, b_mod.reshape(DEPTH, 1, n))


def _mod_spec(batch, k):
    return pl.BlockSpec((1, 1, D_MODEL), lambda b, t: (jnp.where(t == 0, batch, b), 0, k))


def _modulate_kernel(x_ref, sh_ref, sc_ref, o_ref):
    o_ref[0] = (x_ref[0] * (1.0 + sc_ref[0]) + sh_ref[0]).astype(o_ref.dtype)


def _modulate(x_all, mod_l):
    b, lt, d = x_all.shape
    return pl.pallas_call(
        _modulate_kernel,
        grid=(b, lt // TOK_TILE),
        in_specs=[
            pl.BlockSpec((1, TOK_TILE, d), lambda i, t: (i, t, 0)),
            _mod_spec(b, 0),
            _mod_spec(b, 1),
        ],
        out_specs=pl.BlockSpec((1, TOK_TILE, d), lambda i, t: (i, t, 0)),
        out_shape=jax.ShapeDtypeStruct((b, lt, d), BF16),
        compiler_params=_cp(("arbitrary", "arbitrary")),
        name="modulate",
    )(x_all, mod_l, mod_l)


def _proj_kernel(h_ref, w_ref, o_ref, *, rows):
    def body(r, carry):
        sl = pl.ds(pl.multiple_of(r * rows, rows), rows)
        o_ref[0, sl, :] = _dot(h_ref[0, sl, :], w_ref[...])
        return carry

    lax.fori_loop(0, h_ref.shape[1] // rows, body, 0)


def _proj(h, w, tn):
    b, lt, d = h.shape
    n = w.shape[1]
    return pl.pallas_call(
        functools.partial(_proj_kernel, rows=TOK_TILE),
        grid=(b, n // tn),
        in_specs=[
            pl.BlockSpec((1, lt, d), lambda i, j: (i, 0, 0)),
            pl.BlockSpec((d, tn), lambda i, j: (0, j)),
        ],
        out_specs=pl.BlockSpec((1, lt, tn), lambda i, j: (i, 0, j)),
        out_shape=jax.ShapeDtypeStruct((b, lt, n), F32),
        compiler_params=_cp(("arbitrary", "arbitrary")),
        name="proj",
    )(h, w)


def _rms(x, g):
    return x * lax.rsqrt(jnp.mean(x * x, axis=-1, keepdims=True) + RMS_EPS) * g


def _mla_prep_kernel(z_ref, qg_ref, kg_ref, wq1_ref, wq2_ref, wk_ref, wv_ref, ca_ref, sb_ref,
                     q_ref, k_ref, v_ref):
    z = z_ref[0]
    ca = ca_ref[...]
    sb = sb_ref[...]
    qn = _rms(z[:, :MLA_Q_LORA], qg_ref[...]).astype(BF16)
    qa = _dot(qn, wq1_ref[...])
    qb = _dot(qn, wq2_ref[...])
    kvn = _rms(z[:, MLA_Q_LORA:MLA_Q_LORA + MLA_KV_LORA], kg_ref[...]).astype(BF16)
    kn = _dot(kvn, wk_ref[...])
    v_ref[0] = _dot(kvn, wv_ref[...]).astype(v_ref.dtype)
    o = MLA_Q_LORA + MLA_KV_LORA
    kr = z[:, o:o + 128] * ca + z[:, o + 128:o + 256] * sb
    for h in range(MLA_HEADS):
        sl = slice(h * 128, (h + 1) * 128)
        q_ref[0, :, sl] = ((qa[:, sl] * ca + qb[:, sl] * sb) * MLA_SCALE).astype(q_ref.dtype)
        k_ref[0, :, sl] = (kn[:, sl] + kr).astype(k_ref.dtype)


def _mla_prep(z_mla, qg, kg, wq1, wq2, wk, wv, ca, sb):
    b, lt, _ = z_mla.shape
    full = lambda a: pl.BlockSpec(a.shape, lambda i, t: (0,) * a.ndim)
    tok = lambda w: pl.BlockSpec((1, TOK_TILE, w), lambda i, t: (i, t, 0))
    return pl.pallas_call(
        _mla_prep_kernel,
        grid=(b, lt // TOK_TILE),
        in_specs=[tok(ZA_WIDTH), full(qg), full(kg), full(wq1), full(wq2), full(wk), full(wv),
                  pl.BlockSpec((TOK_TILE, 128), lambda i, t: (t, 0)),
                  pl.BlockSpec((TOK_TILE, 128), lambda i, t: (t, 0))],
        out_specs=[tok(MLA_HEADS * 128), tok(MLA_HEADS * 128), tok(MLA_HEADS * MLA_V)],
        out_shape=[jax.ShapeDtypeStruct((b, lt, MLA_HEADS * 128), BF16),
                   jax.ShapeDtypeStruct((b, lt, MLA_HEADS * 128), BF16),
                   jax.ShapeDtypeStruct((b, lt, MLA_HEADS * MLA_V), BF16)],
        compiler_params=_cp(("arbitrary", "arbitrary")),
        name="mla_prep",
    )(z_mla, qg, kg, wq1, wq2, wk, wv, ca, sb)


def _attn_kernel(q_ref, k_ref, v_ref, o_ref, *, n_ctx):
    def attend(nk):
        outs = []
        for h in range(2):
            q = q_ref[0, :, h * 128:(h + 1) * 128]
            k = k_ref[0, :nk, h * 128:(h + 1) * 128]
            s = _dot_nt(q, k)
            p = jnp.exp(s - jnp.max(s, axis=1, keepdims=True))
            l = jnp.sum(p, axis=1, keepdims=True)
            o = _dot(p.astype(BF16), v_ref[0, :nk, h * MLA_V:(h + 1) * MLA_V])
            outs.append(o / l)
        o_ref[0] = jnp.concatenate(outs, axis=1).astype(o_ref.dtype)

    @pl.when(pl.program_id(2) == 0)
    def _():
        attend(n_ctx)

    @pl.when(pl.program_id(2) > 0)
    def _():
        attend(k_ref.shape[1])


def _attention(q, k, v, n_ctx):
    b, lt, _ = q.shape
    return pl.pallas_call(
        functools.partial(_attn_kernel, n_ctx=n_ctx),
        grid=(b, MLA_HEADS // 2, lt // TOK_TILE),
        in_specs=[
            pl.BlockSpec((1, TOK_TILE, 256), lambda i, h, t: (i, t, h)),
            pl.BlockSpec((1, lt, 256), lambda i, h, t: (i, 0, h)),
            pl.BlockSpec((1, lt, 2 * MLA_V), lambda i, h, t: (i, 0, h)),
        ],
        out_specs=pl.BlockSpec((1, TOK_TILE, 2 * MLA_V), lambda i, h, t: (i, t, h)),
        out_shape=jax.ShapeDtypeStruct((b, lt, MLA_HEADS * MLA_V), BF16),
        compiler_params=_cp(("arbitrary", "arbitrary", "arbitrary")),
        name="mla_attention",
    )(q, k, v)


def _rev_tile(s, n):
    return jnp.where(s == 0, 0, n - s)


def _time_index(shape, dim, reverse):
    i = lax.broadcasted_iota(jnp.int32, shape, dim)
    return (shape[dim] - 1 - i) if reverse else i


def _blk(i, size):
    return lax.shift_right_logical(i, jnp.int32(int(math.log2(size))))


def _gla_chunks(items):
    c_len = items[0]["q"].shape[0]
    for it in items:
        it["c"] = it["g"]
        it["t"] = it["g"]
        it["a"] = jnp.zeros((c_len, c_len), F32)
    hb = 1
    while hb < c_len:
        for it in items:
            rev = it["reverse"]
            ri = _time_index((c_len, c_len), 0, rev)
            ci = _time_index((c_len, c_len), 1, rev)
            c, t = it["c"], it["t"]
            qt = (it["q"] * jnp.exp(c)).astype(BF16)
            kt = (it["k"] * jnp.exp(t - c)).astype(BF16)
            pair = jnp.logical_and(_blk(ri, hb) == _blk(ci, hb) + 1, (_blk(ri, hb) & 1) == 1)
            it["a"] = it["a"] + jnp.where(pair, _dot_nt(qt, kt), 0.0)
            odd = (_blk(_time_index((c_len, REC_D), 0, rev), hb) & 1) == 1
            t_lo = pltpu.roll(t, hb, 0)
            t_hi = pltpu.roll(t, c_len - hb, 0)
            prev, nxt = (t_hi, t_lo) if rev else (t_lo, t_hi)
            it["c"] = c + jnp.where(odd, prev, 0.0)
            it["t"] = t + jnp.where(odd, prev, nxt)
        hb *= 2
    outs = []
    for it in items:
        q, k, v, c, t = it["q"], it["k"], it["v"], it["c"], it["t"]
        st = it["st_ref"][...]
        qk = jnp.sum(q * k, axis=1, keepdims=True)
        outs.append(_dot_nt((q * jnp.exp(c)).astype(BF16), st.astype(BF16))
                    + _dot(it["a"].astype(BF16), v.astype(BF16)) + qk * v)
        kd = (k * jnp.exp(t - c)).astype(BF16)
        it["st_ref"][...] = st * jnp.exp(t[0:1, :]) + _dot(v.T.astype(BF16), kd)
    return outs


def _hgrn2_kernel(qf_ref, vf_ref, ff_ref, qb_ref, vb_ref, fb_ref, lb_ref, of_ref, ob_ref, st_ref):
    @pl.when(pl.program_id(1) == 0)
    def _():
        st_ref[...] = jnp.zeros(st_ref.shape, F32)

    n_chunks = TOK_TILE // CHUNK

    def body(ci, carry):
        items, dests = [], []
        for d, (q_ref, v_ref, f_ref, o_ref) in enumerate(((qf_ref, vf_ref, ff_ref, of_ref),
                                                           (qb_ref, vb_ref, fb_ref, ob_ref))):
            cc = ci if d == 0 else n_chunks - 1 - ci
            rows = pl.ds(pl.multiple_of(cc * CHUNK, CHUNK), CHUNK)
            for h in range(REC_HEADS):
                cols = slice(h * REC_D, (h + 1) * REC_D)
                lb = lb_ref[d, :, cols]
                f = lb + (1.0 - lb) * _sigmoid(f_ref[0, rows, cols])
                items.append(dict(q=_silu(q_ref[0, rows, cols]), k=1.0 - f, v=v_ref[0, rows, cols],
                                  g=jnp.log(f), st_ref=st_ref.at[d, h], reverse=(d == 1)))
                dests.append((o_ref, rows, cols))
        for (o_ref, rows, cols), o in zip(dests, _gla_chunks(items)):
            o_ref[0, rows, cols] = o
        return carry

    lax.fori_loop(0, n_chunks, body, 0)


def _hgrn2(z_main, lb):
    b, lt, _ = z_main.shape
    n = lt // TOK_TILE
    base = ZM_HG // REC_W
    fwd = lambda k: pl.BlockSpec((1, TOK_TILE, REC_W), lambda i, s: (i, s, base + k))
    bwd = lambda k: pl.BlockSpec((1, TOK_TILE, REC_W), lambda i, s: (i, _rev_tile(s, n), base + k))
    return pl.pallas_call(
        _hgrn2_kernel,
        grid=(b, n),
        in_specs=[fwd(0), fwd(1), fwd(2), bwd(0), bwd(1), bwd(3),
                  pl.BlockSpec((2, 1, REC_W), lambda i, s: (0, 0, 0))],
        out_specs=[pl.BlockSpec((1, TOK_TILE, REC_W), lambda i, s: (i, s, 0)),
                   pl.BlockSpec((1, TOK_TILE, REC_W), lambda i, s: (i, _rev_tile(s, n), 0))],
        out_shape=[jax.ShapeDtypeStruct((b, lt, REC_W), F32)] * 2,
        scratch_shapes=[pltpu.VMEM((2, REC_HEADS, REC_D, REC_D), F32)],
        compiler_params=_cp(("arbitrary", "arbitrary")),
        name="hgrn2_scan",
    )(z_main, z_main, z_main, z_main, z_main, z_main, lb)


def _gdn_prep_kernel(x_ref, w_ref, o_ref, *, n_ctx):
    x = x_ref[0]
    lt = x.shape[0]
    t = lax.broadcasted_iota(jnp.int32, x.shape, 0)
    lo = jnp.where(t < n_ctx, 0, n_ctx)
    hi = jnp.where(t < n_ctx, n_ctx, lt)
    acc = x * w_ref[CONV_K // 2:CONV_K // 2 + 1, :]
    for kk in range(CONV_K):
        off = kk - CONV_K // 2
        if off == 0:
            continue
        xs = pltpu.roll(x, (-off) % lt, 0)
        ok = jnp.logical_and(t + off >= lo, t + off < hi)
        acc = acc + jnp.where(ok, xs, 0.0) * w_ref[kk:kk + 1, :]
    y = _silu(acc)
    inv = lax.rsqrt(jnp.sum(y * y, axis=1, keepdims=True) + 1e-6)
    is_qk = pl.program_id(1) < 2 * REC_HEADS
    o_ref[0] = y * jnp.where(is_qk, inv, 1.0)


def _gdn_prep(z_gdn, conv_w, n_ctx):
    b, lt, _ = z_gdn.shape
    return pl.pallas_call(
        functools.partial(_gdn_prep_kernel, n_ctx=n_ctx),
        grid=(b, ZG_CONV // REC_D),
        in_specs=[pl.BlockSpec((1, lt, REC_D), lambda i, j: (i, 0, j)),
                  pl.BlockSpec((CONV_K, REC_D), lambda i, j: (0, j))],
        out_specs=pl.BlockSpec((1, lt, REC_D), lambda i, j: (i, 0, j)),
        out_shape=jax.ShapeDtypeStruct((b, lt, ZG_CONV), F32),
        compiler_params=_cp(("arbitrary", "arbitrary")),
        name="gdn_prep",
    )(z_gdn, conv_w)


def _gdn_chunks(items):
    c_len = items[0]["q"].shape[0]
    for it in items:
        ri = _time_index((c_len, c_len), 0, it["reverse"])
        ci = _time_index((c_len, c_len), 1, it["reverse"])
        it["ri"], it["ci"] = ri, ci
        it["dec"] = jnp.where(ri >= ci, jnp.exp(jnp.minimum(it["gcol"] - it["grow"], 0.0)), 0.0)
        it["kb"] = it["k"] * it["beta"]
        it["kbf"] = it["k"].astype(BF16)
    for it in items:
        ri, ci = it["ri"], it["ci"]
        lm = jnp.where(ri > ci, _dot_nt(it["kb"].astype(BF16), it["kbf"]) * it["dec"], 0.0)
        it["lm"] = lm
        it["tinv"] = jnp.where(ri == ci, 1.0, 0.0) - jnp.where(_blk(ri, 2) == _blk(ci, 2), lm, 0.0)
    hb = 2
    while hb < c_len:
        for it in items:
            ri, ci = it["ri"], it["ci"]
            off = jnp.where(_blk(ri, 2 * hb) == _blk(ci, 2 * hb),
                            jnp.where(_blk(ri, hb) == _blk(ci, hb), 0.0, it["lm"]), 0.0)
            it["tb"] = it["tinv"].astype(BF16)
            it["to"] = _dot(it["tb"], off.astype(BF16)).astype(BF16)
        for it in items:
            it["tinv"] = it["tinv"] - _dot(it["to"], it["tb"])
        hb *= 2
    for it in items:
        eg = jnp.exp(it["gcol"])
        tb = it["tinv"].astype(BF16)
        it["u"] = _dot(tb, (it["v"] * it["beta"]).astype(BF16))
        it["w"] = _dot(tb, (it["kb"] * eg).astype(BF16))
        qs = it["q"] * (REC_D ** -0.5)
        it["aqk"] = (_dot_nt(qs.astype(BF16), it["kbf"]) * it["dec"]).astype(BF16)
        it["qd"] = (qs * eg).astype(BF16)
        it["kdt"] = (it["k"] * jnp.exp(it["glast"] - it["gcol"])).T.astype(BF16)
    for it in items:
        it["s"] = it["s_ref"][...]
        it["sb"] = it["s"].astype(BF16)
        it["v_new"] = (it["u"] - _dot(it["w"].astype(BF16), it["sb"])).astype(BF16)
    outs = []
    for it in items:
        outs.append(_dot(it["qd"], it["sb"]) + _dot(it["aqk"], it["v_new"]))
        it["s_ref"][...] = it["s"] * jnp.exp(it["glast"]) + _dot(it["kdt"], it["v_new"])
    return outs


def _gdn_kernel(qf_ref, kf_ref, vf_ref, gf_ref, qb_ref, kb_ref, vb_ref, gb_ref, par_ref, of_ref, ob_ref, s_ref):
    @pl.when(pl.program_id(1) == 0)
    def _():
        s_ref[...] = jnp.zeros(s_ref.shape, F32)

    n_chunks = TOK_TILE // CHUNK
    neg_a = -jnp.exp(par_ref[0:1, :])
    dt_bias = par_ref[1:2, :]
    r2 = lax.broadcasted_iota(jnp.int32, (CHUNK, CHUNK), 0)
    c2 = lax.broadcasted_iota(jnp.int32, (CHUNK, CHUNK), 1)

    def body(ci, carry):
        items, dests = [], []
        for d, (q_ref, k_ref, v_ref, g_ref, o_ref) in enumerate(((qf_ref, kf_ref, vf_ref, gf_ref, of_ref),
                                                                  (qb_ref, kb_ref, vb_ref, gb_ref, ob_ref))):
            cc = ci if d == 0 else n_chunks - 1 - ci
            rows = pl.ds(pl.multiple_of(cc * CHUNK, CHUNK), CHUNK)
            ab = g_ref[0, rows, :]
            xa = ab + dt_bias
            softplus = jnp.maximum(xa, 0.0) + jnp.log(1.0 + jnp.exp(-jnp.abs(xa)))
            tri = jnp.where((r2 <= c2) if d == 1 else (r2 >= c2), 1.0, 0.0)
            gc = _dot(tri, neg_a * softplus, HIGHEST)
            gct = gc.T
            beta = _sigmoid(ab)
            last = 0 if d == 1 else CHUNK - 1
            for h in range(REC_HEADS):
                cols = slice(h * REC_D, (h + 1) * REC_D)
                ln = d * REC_HEADS + h
                items.append(dict(
                    q=q_ref[0, rows, cols], k=k_ref[0, rows, cols], v=v_ref[0, rows, cols],
                    gcol=gc[:, ln:ln + 1], grow=gct[ln:ln + 1, :], glast=gc[last:last + 1, ln:ln + 1],
                    beta=beta[:, 2 * REC_HEADS + ln:2 * REC_HEADS + ln + 1], s_ref=s_ref.at[d, h],
                    reverse=(d == 1)))
                dests.append((o_ref, rows, cols))
        for (o_ref, rows, cols), o in zip(dests, _gdn_chunks(items)):
            o_ref[0, rows, cols] = o
        return carry

    lax.fori_loop(0, n_chunks, body, 0)


def _gdn(qkv, z_gdn, par):
    b, lt, _ = qkv.shape
    n = lt // TOK_TILE
    fwd = lambda k: pl.BlockSpec((1, TOK_TILE, REC_W), lambda i, s: (i, s, k))
    bwd = lambda k: pl.BlockSpec((1, TOK_TILE, REC_W), lambda i, s: (i, _rev_tile(s, n), k))
    gcol = ZG_CONV // 128
    return pl.pallas_call(
        _gdn_kernel,
        grid=(b, n),
        in_specs=[fwd(0), fwd(1), fwd(2), pl.BlockSpec((1, TOK_TILE, 128), lambda i, s: (i, s, gcol)),
                  bwd(0), bwd(1), bwd(2), pl.BlockSpec((1, TOK_TILE, 128), lambda i, s: (i, _rev_tile(s, n), gcol)),
                  pl.BlockSpec((8, 128), lambda i, s: (0, 0))],
        out_specs=[pl.BlockSpec((1, TOK_TILE, REC_W), lambda i, s: (i, s, 0)),
                   pl.BlockSpec((1, TOK_TILE, REC_W), lambda i, s: (i, _rev_tile(s, n), 0))],
        out_shape=[jax.ShapeDtypeStruct((b, lt, REC_W), F32)] * 2,
        scratch_shapes=[pltpu.VMEM((2, REC_HEADS, REC_D, REC_D), F32)],
        compiler_params=_cp(("arbitrary", "arbitrary")),
        name="gdn_scan",
    )(qkv, qkv, qkv, z_gdn, qkv, qkv, qkv, z_gdn, par)


def _layer_norm(x, g, b):
    mu = jnp.mean(x, axis=-1, keepdims=True)
    xc = x - mu
    var = jnp.mean(xc * xc, axis=-1, keepdims=True)
    return xc * lax.rsqrt(var + LN_EPS) * g + b


def _head_norm_gate(o, gate, w):
    outs = []
    for h in range(REC_HEADS):
        cols = slice(h * REC_D, (h + 1) * REC_D)
        oh = o[:, cols]
        n = oh * lax.rsqrt(jnp.mean(oh * oh, axis=-1, keepdims=True) + RMS_EPS) * w
        outs.append(n * _silu(gate[:, cols]))
    return jnp.concatenate(outs, axis=1)


def _merge_kernel(mla_ref, hf_ref, hb_ref, gf_ref, gb_ref, gates_ref, hgate_ref, ggate_ref, x_ref,
                  m2_ref, m3_ref, m4_ref, hw_ref, gw_ref, wb_ref, wo_ref, lg_ref, lbias_ref, wr_ref,
                  x1_ref, h2_ref, lt_ref):
    hg = _head_norm_gate(hf_ref[0] + hb_ref[0], hgate_ref[0], hw_ref[...]).astype(BF16)
    gd = _head_norm_gate(gf_ref[0] + gb_ref[0], ggate_ref[0], gw_ref[...]).astype(BF16)
    y = jnp.zeros((TOK_TILE, D_MODEL), F32)
    for n, o in enumerate((mla_ref[0], hg, gd)):
        y = y + _sigmoid(gates_ref[0, :, n * D_MODEL:(n + 1) * D_MODEL]) * _dot(o, wb_ref[n])
    y = _dot(y.astype(BF16), wo_ref[...])
    x1 = _layer_norm(DEEPNORM_ALPHA * x_ref[0] + m2_ref[0] * y, lg_ref[...], lbias_ref[...])
    x1_ref[0] = x1
    h2 = x1 * (1.0 + m4_ref[0]) + m3_ref[0]
    h2_ref[0] = h2.astype(h2_ref.dtype)
    lt_ref[...] = _dot_nt(wr_ref[...], h2, HIGHEST)


def _merge(mla_o, hg_f, hg_b, gd_f, gd_b, z_main, x_all, mod_l, hg_w, gdn_w, wb, wo, ln_g, ln_b, wr_t):
    b, lt, d = x_all.shape
    n = lt // TOK_TILE
    tok = lambda w, k=0: pl.BlockSpec((1, TOK_TILE, w), lambda i, t: (i, t, k))
    full = lambda a: pl.BlockSpec(a.shape, lambda i, t: (0,) * a.ndim)
    return pl.pallas_call(
        _merge_kernel,
        grid=(b, n),
        in_specs=[tok(BRANCH_W), tok(REC_W), tok(REC_W), tok(REC_W), tok(REC_W),
                  tok(N_BRANCH * D_MODEL, 0), tok(REC_W, ZM_HG // REC_W + 4), tok(REC_W, ZM_GGATE // REC_W),
                  tok(d), _mod_spec(b, 2), _mod_spec(b, 3), _mod_spec(b, 4),
                  full(hg_w), full(gdn_w), full(wb), full(wo), full(ln_g), full(ln_b), full(wr_t)],
        out_specs=[tok(d), tok(d), pl.BlockSpec((N_EXPERTS, TOK_TILE), lambda i, t: (0, i * n + t))],
        out_shape=[jax.ShapeDtypeStruct((b, lt, d), F32), jax.ShapeDtypeStruct((b, lt, d), BF16),
                   jax.ShapeDtypeStruct((N_EXPERTS, b * lt), F32)],
        compiler_params=_cp(("arbitrary", "arbitrary")),
        name="merge",
    )(mla_o, hg_f, hg_b, gd_f, gd_b, z_main, z_main, z_main, x_all, mod_l, mod_l, mod_l,
      hg_w, gdn_w, wb, wo, ln_g, ln_b, wr_t)


def _first_max(x, idx, axes):
    m = x
    for ax in axes:
        m = jnp.max(m, axis=ax, keepdims=True)
    first = jnp.where(x == m, idx, jnp.int32(2 ** 30))
    for ax in axes:
        first = jnp.min(first, axis=ax, keepdims=True)
    return m, first


def _route_kernel(lt_ref, bias_ref, comb_ref):
    n_tok = lt_ref.shape[1]
    per = N_EXPERTS // N_GROUPS
    scores = _sigmoid(lt_ref[...]).reshape(N_GROUPS, per, n_tok)
    sel = scores + bias_ref[...]
    ig = lax.broadcasted_iota(jnp.int32, sel.shape, 0)
    ij = lax.broadcasted_iota(jnp.int32, sel.shape, 1)
    top1, a1 = _first_max(sel, ij, (1,))
    top2 = jnp.max(jnp.where(ij == a1, -jnp.inf, sel), axis=1, keepdims=True)
    grp = top1 + top2
    igg = lax.broadcasted_iota(jnp.int32, grp.shape, 0)
    gsel = jnp.zeros(grp.shape, F32)
    for _ in range(TOPK_GROUPS):
        _, a = _first_max(grp, igg, (0,))
        hit = igg == a
        gsel = jnp.where(hit, 1.0, gsel)
        grp = jnp.where(hit, -jnp.inf, grp)
    cur = jnp.where(gsel > 0.5, sel, -jnp.inf)
    ie = ig * per + ij
    esel = jnp.zeros(cur.shape, F32)
    for _ in range(TOP_K):
        _, a = _first_max(cur, ie, (1, 0))
        hit = ie == a
        esel = jnp.where(hit, 1.0, esel)
        cur = jnp.where(hit, -jnp.inf, cur)
    w = scores * esel
    tot = jnp.sum(jnp.sum(w, axis=1, keepdims=True), axis=0, keepdims=True)
    w = w / tot * ROUTED_SCALE
    comb_ref[...] = w.reshape(N_EXPERTS, n_tok).T


def _route(logits_t, bias):
    n_tok = logits_t.shape[1]
    tt = 512
    return pl.pallas_call(
        _route_kernel,
        grid=(n_tok // tt,),
        in_specs=[pl.BlockSpec((N_EXPERTS, tt), lambda i: (0, i)),
                  pl.BlockSpec((N_GROUPS, N_EXPERTS // N_GROUPS, 1), lambda i: (0, 0, 0))],
        out_specs=pl.BlockSpec((tt, N_EXPERTS), lambda i: (i, 0)),
        out_shape=jax.ShapeDtypeStruct((n_tok, N_EXPERTS), F32),
        compiler_params=_cp(("arbitrary",)),
        name="route",
    )(logits_t, bias)


MOE_EB = 4


def _moe_kernel(h_ref, comb_ref, wgu_ref, wd_ref, wsgu_ref, wsd_ref, o_ref):
    e = pl.program_id(1)
    h = h_ref[...]

    @pl.when(e == 0)
    def _():
        gu = _dot(h, wsgu_ref[...])
        act = _silu(gu[:, :SHARED_FF]) * gu[:, SHARED_FF:]
        o_ref[...] = _dot(act.astype(BF16), wsd_ref[...])

    wide = MOE_EB * EXPERT_FF
    owner = e * MOE_EB + _blk(lax.broadcasted_iota(jnp.int32, (N_EXPERTS, wide), 1), EXPERT_FF)
    pick = jnp.where(lax.broadcasted_iota(jnp.int32, (N_EXPERTS, wide), 0) == owner, 1.0, 0.0).astype(BF16)
    comb = comb_ref[...]
    comb_hi = comb.astype(BF16)
    comb_lo = (comb - comb_hi.astype(F32)).astype(BF16)
    cw = _dot(comb_hi, pick) + _dot(comb_lo, pick)
    acts = []
    for j in range(MOE_EB):
        gu = _dot(h, wgu_ref[j])
        act = _silu(gu[:, :EXPERT_FF]) * gu[:, EXPERT_FF:] * cw[:, j * EXPERT_FF:(j + 1) * EXPERT_FF]
        acts.append(act.astype(BF16))
    o_ref[...] += _dot(jnp.concatenate(acts, axis=1), wd_ref[...].reshape(wide, D_MODEL))


def _moe(h2, comb, wgu, wd, wsgu, wsd, tm):
    n_tok, d = h2.shape
    return pl.pallas_call(
        _moe_kernel,
        grid=(n_tok // tm, N_EXPERTS // MOE_EB),
        in_specs=[pl.BlockSpec((tm, d), lambda i, e: (i, 0)),
                  pl.BlockSpec((tm, N_EXPERTS), lambda i, e: (i, 0)),
                  pl.BlockSpec((MOE_EB, d, 2 * EXPERT_FF), lambda i, e: (e, 0, 0)),
                  pl.BlockSpec((MOE_EB, EXPERT_FF, d), lambda i, e: (e, 0, 0)),
                  pl.BlockSpec(wsgu.shape, lambda i, e: (0, 0)),
                  pl.BlockSpec(wsd.shape, lambda i, e: (0, 0))],
        out_specs=pl.BlockSpec((tm, d), lambda i, e: (i, 0)),
        out_shape=jax.ShapeDtypeStruct((n_tok, d), F32),
        compiler_params=_cp(("arbitrary", "arbitrary")),
        name="moe",
    )(h2, comb, wgu, wd, wsgu, wsd)


def _ffn_norm_kernel(x_ref, f_ref, m5_ref, g_ref, b_ref, o_ref):
    o_ref[0] = _layer_norm(DEEPNORM_ALPHA * x_ref[0] + m5_ref[0] * f_ref[0], g_ref[...], b_ref[...])


def _ffn_norm(x1, f, mod_l, ln_g, ln_b):
    b, lt, d = x1.shape
    tok = pl.BlockSpec((1, TOK_TILE, d), lambda i, t: (i, t, 0))
    full = lambda a: pl.BlockSpec(a.shape, lambda i, t: (0,) * a.ndim)
    return pl.pallas_call(
        _ffn_norm_kernel,
        grid=(b, lt // TOK_TILE),
        in_specs=[tok, tok, _mod_spec(b, 5), full(ln_g), full(ln_b)],
        out_specs=tok,
        out_shape=jax.ShapeDtypeStruct((b, lt, d), F32),
        compiler_params=_cp(("arbitrary", "arbitrary")),
        name="ffn_norm",
    )(x1, f, mod_l, ln_g, ln_b)


def _rope_tables(n_ctx, n_lat):
    rows = n_lat // GRID_W
    row = jnp.broadcast_to(jnp.arange(rows, dtype=F32)[:, None], (rows, GRID_W)).reshape(-1)
    col = jnp.broadcast_to(jnp.arange(GRID_W, dtype=F32)[None, :], (rows, GRID_W)).reshape(-1)
    inv = ROPE_BASE ** (-jnp.arange(ROPE_FREQS, dtype=F32) / ROPE_FREQS)
    ang = jnp.stack([row[:, None] * inv, col[:, None] * inv], axis=1)
    cos = jnp.cos(ang)[:, :, None, :]
    sin = jnp.sin(ang)[:, :, None, :]
    cos32 = jnp.broadcast_to(cos, (n_lat, 2, 2, ROPE_FREQS)).reshape(n_lat, MLA_ROPE)
    sin32 = jnp.concatenate([-sin, sin], axis=2).reshape(n_lat, MLA_ROPE)
    cos32 = jnp.concatenate([jnp.ones((n_ctx, MLA_ROPE), F32), cos32], axis=0)
    sin32 = jnp.concatenate([jnp.zeros((n_ctx, MLA_ROPE), F32), sin32], axis=0)
    lt = n_ctx + n_lat
    ca = jnp.concatenate([jnp.ones((lt, MLA_NOPE), F32), cos32, jnp.zeros((lt, 32), F32)], axis=1)
    sb = jnp.concatenate([jnp.zeros((lt, MLA_NOPE), F32), sin32, jnp.zeros((lt, 32), F32)], axis=1)
    return ca, sb


def _rope_partner():
    idx = np.arange(MLA_ROPE).reshape(2, 2, ROPE_FREQS)
    return idx[:, ::-1, :].reshape(-1)


def _pack_layer(w_in, w_q_b, w_kv_b):
    sizes = (MLA_Q_LORA, MLA_KV_LORA + MLA_ROPE, REC_W, REC_W, REC_W, REC_W, REC_W,
             3 * REC_W, REC_W, 2 * REC_HEADS, 2 * REC_HEADS, N_BRANCH * D_MODEL)
    offs = np.cumsum((0,) + sizes)
    seg = lambda i: w_in[:, offs[i]:offs[i + 1]]
    d = w_in.shape[0]
    zeros = lambda n: jnp.zeros((d, n), w_in.dtype)
    partner = _rope_partner()
    kva = seg(1)
    k_rope = kva[:, MLA_KV_LORA:]
    w_main = jnp.concatenate([seg(11), seg(2), seg(3), seg(4), seg(5), seg(6), seg(8)], axis=1)
    w_mla = jnp.concatenate([seg(0), kva[:, :MLA_KV_LORA],
                             zeros(MLA_NOPE), k_rope, zeros(32),
                             zeros(MLA_NOPE), k_rope[:, partner], zeros(32)], axis=1)
    w_gdn = jnp.concatenate([seg(7), seg(9), seg(10), zeros(256 - 4 * REC_HEADS)], axis=1)
    r = w_q_b.shape[0]
    qb = w_q_b.reshape(r, MLA_HEADS, MLA_NOPE + MLA_ROPE)
    zq = lambda n: jnp.zeros((r, MLA_HEADS, n), w_q_b.dtype)
    wq1 = jnp.concatenate([qb, zq(32)], axis=2).reshape(r, MLA_HEADS * 128)
    wq2 = jnp.concatenate([zq(MLA_NOPE), qb[:, :, MLA_NOPE:][:, :, partner], zq(32)], axis=2).reshape(r, MLA_HEADS * 128)
    rk = w_kv_b.shape[0]
    kvb = w_kv_b.reshape(rk, MLA_HEADS, MLA_NOPE + MLA_V)
    wk = jnp.concatenate([kvb[:, :, :MLA_NOPE], jnp.zeros((rk, MLA_HEADS, 64), w_kv_b.dtype)], axis=2)
    wk = wk.reshape(rk, MLA_HEADS * 128)
    wv = kvb[:, :, MLA_NOPE:].reshape(rk, MLA_HEADS * MLA_V)
    bf = lambda a: a.astype(BF16)
    return bf(w_main), bf(w_mla), bf(w_gdn), bf(wq1), bf(wq2), bf(wk), bf(wv)


def kernel(x, c, ctx, c_ctx, w_mod, b_mod, w_in, q_a_norm, w_q_b, kv_a_norm, w_kv_b, hg_lb_logits, hg_norm,
           gdn_conv, gdn_a_log, gdn_dt_bias, gdn_norm, w_branch, w_out, ln1_g, ln1_b, ln2_g, ln2_b,
           w_router, router_bias, w_gu, w_down, w_sh_gu, w_sh_down):
    batch, n_lat, d = x.shape
    n_ctx = ctx.shape[1]
    assert n_ctx == TOK_TILE and n_lat % TOK_TILE == 0 and batch < MOD_ROWS and d == D_MODEL
    lt = n_ctx + n_lat

    c_all = jnp.zeros((MOD_ROWS, d), F32).at[:batch].set(c).at[batch].set(c_ctx)
    mod = _mod_all(c_all, w_mod, b_mod).reshape(DEPTH, MOD_ROWS, 1, 6 * d)
    ca, sb = _rope_tables(n_ctx, n_lat)
    lb_soft = jax.nn.softmax(hg_lb_logits.astype(F32), axis=0)
    lower = (jnp.cumsum(lb_soft, axis=0) - lb_soft[0]).reshape(DEPTH, 2, 1, REC_W)
    row = lambda a: a.reshape(1, -1)

    x_all = jnp.concatenate([ctx, x], axis=1)
    for l in range(DEPTH):
        w_main, w_mla, w_gdn, wq1, wq2, wk, wv = _pack_layer(w_in[l], w_q_b[l], w_kv_b[l])
        mod_l = mod[l]
        h = _modulate(x_all, mod_l)
        z_main = _proj(h, w_main, 512)
        z_mla = _proj(h, w_mla, ZA_WIDTH)
        z_gdn = _proj(h, w_gdn, ZG_WIDTH // 2)

        q, k, v = _mla_prep(z_mla, row(q_a_norm[l]), row(kv_a_norm[l]), wq1, wq2, wk, wv, ca, sb)
        mla_o = _attention(q, k, v, n_ctx)

        hg_f, hg_b = _hgrn2(z_main, lower[l])

        qkv = _gdn_prep(z_gdn, gdn_conv[l], n_ctx)
        par = jnp.zeros((8, 128), F32)
        par = par.at[0, :2 * REC_HEADS].set(gdn_a_log[l].reshape(-1))
        par = par.at[1, :2 * REC_HEADS].set(gdn_dt_bias[l].reshape(-1))
        gd_f, gd_b = _gdn(qkv, z_gdn, par)

        x1, h2, logits_t = _merge(mla_o, hg_f, hg_b, gd_f, gd_b, z_main, x_all, mod_l,
                                  row(hg_norm[l]), row(gdn_norm[l]), w_branch[l].astype(BF16),
                                  w_out[l].astype(BF16), row(ln1_g[l]), row(ln1_b[l]), w_router[l].T)
        comb = _route(logits_t, router_bias[l].reshape(N_GROUPS, N_EXPERTS // N_GROUPS, 1))
        f = _moe(h2.reshape(batch * lt, d), comb, w_gu[l].astype(BF16), w_down[l].astype(BF16),
                 w_sh_gu[l].astype(BF16), w_sh_down[l].astype(BF16), 1024 if (batch * lt) % 1024 == 0 else 512)
        x_all = _ffn_norm(x1, f.reshape(batch, lt, d), mod_l, row(ln2_g[l]), row(ln2_b[l]))
    return x_all[:, n_ctx:, :]
```

```python
import functools
import math

import numpy as np
import jax
import jax.numpy as jnp
from jax import lax
from jax.experimental import pallas as pl
from jax.experimental.pallas import tpu as pltpu
from jax.experimental.pallas import tpu_sc as plsc

F32 = jnp.float32
BF16 = jnp.bfloat16
HIGHEST = lax.Precision.HIGHEST

D_MODEL = 1024
DEPTH = 4
GRID_W = 64
MLA_HEADS = 8
MLA_Q_LORA = 384
MLA_KV_LORA = 256
MLA_NOPE = 64
MLA_ROPE = 32
MLA_V = 64
MLA_SCALE = (MLA_NOPE + MLA_ROPE) ** -0.5
ROPE_BASE = 10000.0
ROPE_FREQS = MLA_ROPE // 4
REC_HEADS = 4
REC_D = 128
REC_W = REC_HEADS * REC_D
CONV_K = 5
N_BRANCH = 3
BRANCH_W = 512
N_EXPERTS = 64
TOP_K = 8
N_GROUPS = 8
TOPK_GROUPS = 4
EXPERT_FF = 256
SHARED_FF = 256
ROUTED_SCALE = 2.5
DEEPNORM_ALPHA = (2 * DEPTH) ** 0.25
LN_EPS = 1e-6
RMS_EPS = 1e-6

TOK_TILE = 256
CHUNK = 64
MOD_ROWS = 16
VMEM_LIMIT = 56 * 1024 * 1024

ZM_GATES = 0
ZM_HG = 3 * D_MODEL
ZM_GGATE = ZM_HG + 5 * REC_W
ZM_WIDTH = ZM_GGATE + REC_W
ZA_WIDTH = MLA_Q_LORA + MLA_KV_LORA + 256
ZG_CONV = 3 * REC_W
ZG_WIDTH = ZG_CONV + 256


def _cp(sem, vmem=VMEM_LIMIT):
    return pltpu.CompilerParams(dimension_semantics=sem, vmem_limit_bytes=vmem)


def _dot(a, b, precision=None):
    return jnp.dot(a, b, preferred_element_type=F32, precision=precision)


def _dot_nt(a, b, precision=None):
    return lax.dot_general(a, b, (((1,), (1,)), ((), ())), preferred_element_type=F32, precision=precision)


def _sigmoid(x):
    return 1.0 / (1.0 + jnp.exp(-x))


def _silu(x):
    return x * _sigmoid(x)


def _pack_rows(x):
    w = x.shape[1] // 2
    hi = lax.bitcast_convert_type(x[:, :w].astype(BF16).astype(F32), jnp.uint32)
    lo = lax.bitcast_convert_type(x[:, w:].astype(BF16).astype(F32), jnp.uint32)
    return lax.bitcast_convert_type(hi | lax.shift_right_logical(lo, jnp.uint32(16)), jnp.int32)


def _unpack_rows(p):
    u = lax.bitcast_convert_type(p, jnp.uint32)
    hi = lax.bitcast_convert_type(u & jnp.uint32(0xFFFF0000), F32).astype(BF16)
    lo = lax.bitcast_convert_type(lax.shift_left(u, jnp.uint32(16)), F32).astype(BF16)
    return jnp.concatenate([hi, lo], axis=1)


def _mod_kernel(c_ref, w_ref, b_ref, o_ref):
    s = _silu(c_ref[...])
    o_ref[0] = _dot(s, w_ref[0], HIGHEST) + b_ref[0]


def _mod_all(c_all, w_mod, b_mod):
    tn = 1024
    n = w_mod.shape[-1]
    return pl.pallas_call(
        _mod_kernel,
        grid=(DEPTH, n // tn),
        in_specs=[
            pl.BlockSpec((MOD_ROWS, D_MODEL), lambda l, j: (0, 0)),
            pl.BlockSpec((1, D_MODEL, tn), lambda l, j: (l, 0, j)),
            pl.BlockSpec((1, 1, tn), lambda l, j: (l, 0, j)),
        ],
        out_specs=pl.BlockSpec((1, MOD_ROWS, tn), lambda l, j: (l, 0, j)),
        out_shape=jax.ShapeDtypeStruct((DEPTH, MOD_ROWS, n), F32),
        compiler_params=_cp(("arbitrary", "arbitrary")),
        name="mod_all",
    )(c_all, w_mod, b_mod.reshape(DEPTH, 1, n))


def _mod_spec(batch, k):
    return pl.BlockSpec((1, 1, D_MODEL), lambda b, t: (jnp.where(t == 0, batch, b), 0, k))


def _modulate_kernel(x_ref, sh_ref, sc_ref, o_ref):
    o_ref[0] = (x_ref[0] * (1.0 + sc_ref[0]) + sh_ref[0]).astype(o_ref.dtype)


def _modulate(x_all, mod_l):
    b, lt, d = x_all.shape
    return pl.pallas_call(
        _modulate_kernel,
        grid=(b, lt // TOK_TILE),
        in_specs=[
            pl.BlockSpec((1, TOK_TILE, d), lambda i, t: (i, t, 0)),
            _mod_spec(b, 0),
            _mod_spec(b, 1),
        ],
        out_specs=pl.BlockSpec((1, TOK_TILE, d), lambda i, t: (i, t, 0)),
        out_shape=jax.ShapeDtypeStruct((b, lt, d), BF16),
        compiler_params=_cp(("arbitrary", "arbitrary")),
        name="modulate",
    )(x_all, mod_l, mod_l)


def _proj_kernel(h_ref, w_ref, o_ref, *, rows):
    def body(r, carry):
        sl = pl.ds(pl.multiple_of(r * rows, rows), rows)
        o_ref[0, sl, :] = _dot(h_ref[0, sl, :], w_ref[...])
        return carry

    lax.fori_loop(0, h_ref.shape[1] // rows, body, 0)


def _proj(h, w, tn):
    b, lt, d = h.shape
    n = w.shape[1]
    return pl.pallas_call(
        functools.partial(_proj_kernel, rows=TOK_TILE),
        grid=(b, n // tn),
        in_specs=[
            pl.BlockSpec((1, lt, d), lambda i, j: (i, 0, 0)),
            pl.BlockSpec((d, tn), lambda i, j: (0, j)),
        ],
        out_specs=pl.BlockSpec((1, lt, tn), lambda i, j: (i, 0, j)),
        out_shape=jax.ShapeDtypeStruct((b, lt, n), F32),
        compiler_params=_cp(("arbitrary", "arbitrary")),
        name="proj",
    )(h, w)


def _rms(x, g):
    return x * lax.rsqrt(jnp.mean(x * x, axis=-1, keepdims=True) + RMS_EPS) * g


def _mla_prep_kernel(z_ref, qg_ref, kg_ref, wq1_ref, wq2_ref, wk_ref, wv_ref, ca_ref, sb_ref,
                     q_ref, k_ref, v_ref):
    z = z_ref[0]
    ca = ca_ref[...]
    sb = sb_ref[...]
    qn = _rms(z[:, :MLA_Q_LORA], qg_ref[...]).astype(BF16)
    qa = _dot(qn, wq1_ref[...])
    qb = _dot(qn, wq2_ref[...])
    kvn = _rms(z[:, MLA_Q_LORA:MLA_Q_LORA + MLA_KV_LORA], kg_ref[...]).astype(BF16)
    kn = _dot(kvn, wk_ref[...])
    v_ref[0] = _dot(kvn, wv_ref[...]).astype(v_ref.dtype)
    o = MLA_Q_LORA + MLA_KV_LORA
    kr = z[:, o:o + 128] * ca + z[:, o + 128:o + 256] * sb
    for h in range(MLA_HEADS):
        sl = slice(h * 128, (h + 1) * 128)
        q_ref[0, :, sl] = ((qa[:, sl] * ca + qb[:, sl] * sb) * MLA_SCALE).astype(q_ref.dtype)
        k_ref[0, :, sl] = (kn[:, sl] + kr).astype(k_ref.dtype)


def _mla_prep(z_mla, qg, kg, wq1, wq2, wk, wv, ca, sb):
    b, lt, _ = z_mla.shape
    full = lambda a: pl.BlockSpec(a.shape, lambda i, t: (0,) * a.ndim)
    tok = lambda w: pl.BlockSpec((1, TOK_TILE, w), lambda i, t: (i, t, 0))
    return pl.pallas_call(
        _mla_prep_kernel,
        grid=(b, lt // TOK_TILE),
        in_specs=[tok(ZA_WIDTH), full(qg), full(kg), full(wq1), full(wq2), full(wk), full(wv),
                  pl.BlockSpec((TOK_TILE, 128), lambda i, t: (t, 0)),
                  pl.BlockSpec((TOK_TILE, 128), lambda i, t: (t, 0))],
        out_specs=[tok(MLA_HEADS * 128), tok(MLA_HEADS * 128), tok(MLA_HEADS * MLA_V)],
        out_shape=[jax.ShapeDtypeStruct((b, lt, MLA_HEADS * 128), BF16),
                   jax.ShapeDtypeStruct((b, lt, MLA_HEADS * 128), BF16),
                   jax.ShapeDtypeStruct((b, lt, MLA_HEADS * MLA_V), BF16)],
        compiler_params=_cp(("arbitrary", "arbitrary")),
        name="mla_prep",
    )(z_mla, qg, kg, wq1, wq2, wk, wv, ca, sb)


def _attn_kernel(q_ref, k_ref, v_ref, o_ref, *, n_ctx):
    def attend(nk):
        outs = []
        for h in range(2):
            q = q_ref[0, :, h * 128:(h + 1) * 128]
            k = k_ref[0, :nk, h * 128:(h + 1) * 128]
            s = _dot_nt(q, k)
            p = jnp.exp(s - jnp.max(s, axis=1, keepdims=True))
            l = jnp.sum(p, axis=1, keepdims=True)
            o = _dot(p.astype(BF16), v_ref[0, :nk, h * MLA_V:(h + 1) * MLA_V])
            outs.append(o / l)
        o_ref[0] = jnp.concatenate(outs, axis=1).astype(o_ref.dtype)

    @pl.when(pl.program_id(2) == 0)
    def _():
        attend(n_ctx)

    @pl.when(pl.program_id(2) > 0)
    def _():
        attend(k_ref.shape[1])


def _attention(q, k, v, n_ctx):
    b, lt, _ = q.shape
    return pl.pallas_call(
        functools.partial(_attn_kernel, n_ctx=n_ctx),
        grid=(b, MLA_HEADS // 2, lt // TOK_TILE),
        in_specs=[
            pl.BlockSpec((1, TOK_TILE, 256), lambda i, h, t: (i, t, h)),
            pl.BlockSpec((1, lt, 256), lambda i, h, t: (i, 0, h)),
            pl.BlockSpec((1, lt, 2 * MLA_V), lambda i, h, t: (i, 0, h)),
        ],
        out_specs=pl.BlockSpec((1, TOK_TILE, 2 * MLA_V), lambda i, h, t: (i, t, h)),
        out_shape=jax.ShapeDtypeStruct((b, lt, MLA_HEADS * MLA_V), BF16),
        compiler_params=_cp(("arbitrary", "arbitrary", "arbitrary")),
        name="mla_attention",
    )(q, k, v)


def _rev_tile(s, n):
    return jnp.where(s == 0, 0, n - s)


def _time_index(shape, dim, reverse):
    i = lax.broadcasted_iota(jnp.int32, shape, dim)
    return (shape[dim] - 1 - i) if reverse else i


def _blk(i, size):
    return lax.shift_right_logical(i, jnp.int32(int(math.log2(size))))


def _gla_chunks(items):
    c_len = items[0]["q"].shape[0]
    for it in items:
        it["c"] = it["g"]
        it["t"] = it["g"]
        it["a"] = jnp.zeros((c_len, c_len), F32)
    hb = 1
    while hb < c_len:
        for it in items:
            rev = it["reverse"]
            ri = _time_index((c_len, c_len), 0, rev)
            ci = _time_index((c_len, c_len), 1, rev)
            c, t = it["c"], it["t"]
            qt = (it["q"] * jnp.exp(c)).astype(BF16)
            kt = (it["k"] * jnp.exp(t - c)).astype(BF16)
            pair = jnp.logical_and(_blk(ri, hb) == _blk(ci, hb) + 1, (_blk(ri, hb) & 1) == 1)
            it["a"] = it["a"] + jnp.where(pair, _dot_nt(qt, kt), 0.0)
            odd = (_blk(_time_index((c_len, REC_D), 0, rev), hb) & 1) == 1
            t_lo = pltpu.roll(t, hb, 0)
            t_hi = pltpu.roll(t, c_len - hb, 0)
            prev, nxt = (t_hi, t_lo) if rev else (t_lo, t_hi)
            it["c"] = c + jnp.where(odd, prev, 0.0)
            it["t"] = t + jnp.where(odd, prev, nxt)
        hb *= 2
    outs = []
    for it in items:
        q, k, v, c, t = it["q"], it["k"], it["v"], it["c"], it["t"]
        st = it["st_ref"][...]
        qk = jnp.sum(q * k, axis=1, keepdims=True)
        outs.append(_dot_nt((q * jnp.exp(c)).astype(BF16), st.astype(BF16))
                    + _dot(it["a"].astype(BF16), v.astype(BF16)) + qk * v)
        kd = (k * jnp.exp(t - c)).astype(BF16)
        it["st_ref"][...] = st * jnp.exp(t[0:1, :]) + _dot(v.T.astype(BF16), kd)
    return outs


def _hgrn2_kernel(qf_ref, vf_ref, ff_ref, qb_ref, vb_ref, fb_ref, lb_ref, of_ref, ob_ref, st_ref):
    @pl.when(pl.program_id(1) == 0)
    def _():
        st_ref[...] = jnp.zeros(st_ref.shape, F32)

    n_chunks = TOK_TILE // CHUNK

    def body(ci, carry):
        items, dests = [], []
        for d, (q_ref, v_ref, f_ref, o_ref) in enumerate(((qf_ref, vf_ref, ff_ref, of_ref),
                                                           (qb_ref, vb_ref, fb_ref, ob_ref))):
            cc = ci if d == 0 else n_chunks - 1 - ci
            rows = pl.ds(pl.multiple_of(cc * CHUNK, CHUNK), CHUNK)
            for h in range(REC_HEADS):
                cols = slice(h * REC_D, (h + 1) * REC_D)
                lb = lb_ref[d, :, cols]
                f = lb + (1.0 - lb) * _sigmoid(f_ref[0, rows, cols])
                items.append(dict(q=_silu(q_ref[0, rows, cols]), k=1.0 - f, v=v_ref[0, rows, cols],
                                  g=jnp.log(f), st_ref=st_ref.at[d, h], reverse=(d == 1)))
                dests.append((o_ref, rows, cols))
        for (o_ref, rows, cols), o in zip(dests, _gla_chunks(items)):
            o_ref[0, rows, cols] = o
        return carry

    lax.fori_loop(0, n_chunks, body, 0)


def _hgrn2(z_main, lb):
    b, lt, _ = z_main.shape
    n = lt // TOK_TILE
    base = ZM_HG // REC_W
    fwd = lambda k: pl.BlockSpec((1, TOK_TILE, REC_W), lambda i, s: (i, s, base + k))
    bwd = lambda k: pl.BlockSpec((1, TOK_TILE, REC_W), lambda i, s: (i, _rev_tile(s, n), base + k))
    return pl.pallas_call(
        _hgrn2_kernel,
        grid=(b, n),
        in_specs=[fwd(0), fwd(1), fwd(2), bwd(0), bwd(1), bwd(3),
                  pl.BlockSpec((2, 1, REC_W), lambda i, s: (0, 0, 0))],
        out_specs=[pl.BlockSpec((1, TOK_TILE, REC_W), lambda i, s: (i, s, 0)),
                   pl.BlockSpec((1, TOK_TILE, REC_W), lambda i, s: (i, _rev_tile(s, n), 0))],
        out_shape=[jax.ShapeDtypeStruct((b, lt, REC_W), F32)] * 2,
        scratch_shapes=[pltpu.VMEM((2, REC_HEADS, REC_D, REC_D), F32)],
        compiler_params=_cp(("arbitrary", "arbitrary")),
        name="hgrn2_scan",
    )(z_main, z_main, z_main, z_main, z_main, z_main, lb)


def _gdn_prep_kernel(x_ref, w_ref, o_ref, *, n_ctx):
    x = x_ref[0]
    lt = x.shape[0]
    t = lax.broadcasted_iota(jnp.int32, x.shape, 0)
    lo = jnp.where(t < n_ctx, 0, n_ctx)
    hi = jnp.where(t < n_ctx, n_ctx, lt)
    acc = x * w_ref[CONV_K // 2:CONV_K // 2 + 1, :]
    for kk in range(CONV_K):
        off = kk - CONV_K // 2
        if off == 0:
            continue
        xs = pltpu.roll(x, (-off) % lt, 0)
        ok = jnp.logical_and(t + off >= lo, t + off < hi)
        acc = acc + jnp.where(ok, xs, 0.0) * w_ref[kk:kk + 1, :]
    y = _silu(acc)
    inv = lax.rsqrt(jnp.sum(y * y, axis=1, keepdims=True) + 1e-6)
    is_qk = pl.program_id(1) < 2 * REC_HEADS
    o_ref[0] = y * jnp.where(is_qk, inv, 1.0)


def _gdn_prep(z_gdn, conv_w, n_ctx):
    b, lt, _ = z_gdn.shape
    return pl.pallas_call(
        functools.partial(_gdn_prep_kernel, n_ctx=n_ctx),
        grid=(b, ZG_CONV // REC_D),
        in_specs=[pl.BlockSpec((1, lt, REC_D), lambda i, j: (i, 0, j)),
                  pl.BlockSpec((CONV_K, REC_D), lambda i, j: (0, j))],
        out_specs=pl.BlockSpec((1, lt, REC_D), lambda i, j: (i, 0, j)),
        out_shape=jax.ShapeDtypeStruct((b, lt, ZG_CONV), F32),
        compiler_params=_cp(("arbitrary", "arbitrary")),
        name="gdn_prep",
    )(z_gdn, conv_w)


def _gdn_chunks(items):
    c_len = items[0]["q"].shape[0]
    for it in items:
        ri = _time_index((c_len, c_len), 0, it["reverse"])
        ci = _time_index((c_len, c_len), 1, it["reverse"])
        it["ri"], it["ci"] = ri, ci
        it["dec"] = jnp.where(ri >= ci, jnp.exp(jnp.minimum(it["gcol"] - it["grow"], 0.0)), 0.0)
        it["kb"] = it["k"] * it["beta"]
        it["kbf"] = it["k"].astype(BF16)
    for it in items:
        ri, ci = it["ri"], it["ci"]
        lm = jnp.where(ri > ci, _dot_nt(it["kb"].astype(BF16), it["kbf"]) * it["dec"], 0.0)
        it["lm"] = lm
        it["tinv"] = jnp.where(ri == ci, 1.0, 0.0) - jnp.where(_blk(ri, 2) == _blk(ci, 2), lm, 0.0)
    hb = 2
    while hb < c_len:
        for it in items:
            ri, ci = it["ri"], it["ci"]
            off = jnp.where(_blk(ri, 2 * hb) == _blk(ci, 2 * hb),
                            jnp.where(_blk(ri, hb) == _blk(ci, hb), 0.0, it["lm"]), 0.0)
            it["tb"] = it["tinv"].astype(BF16)
            it["to"] = _dot(it["tb"], off.astype(BF16)).astype(BF16)
        for it in items:
            it["tinv"] = it["tinv"] - _dot(it["to"], it["tb"])
        hb *= 2
    for it in items:
        eg = jnp.exp(it["gcol"])
        tb = it["tinv"].astype(BF16)
        it["u"] = _dot(tb, (it["v"] * it["beta"]).astype(BF16))
        it["w"] = _dot(tb, (it["kb"] * eg).astype(BF16))
        qs = it["q"] * (REC_D ** -0.5)
        it["aqk"] = (_dot_nt(qs.astype(BF16), it["kbf"]) * it["dec"]).astype(BF16)
        it["qd"] = (qs * eg).astype(BF16)
        it["kdt"] = (it["k"] * jnp.exp(it["glast"] - it["gcol"])).T.astype(BF16)
    for it in items:
        it["s"] = it["s_ref"][...]
        it["sb"] = it["s"].astype(BF16)
        it["v_new"] = (it["u"] - _dot(it["w"].astype(BF16), it["sb"])).astype(BF16)
    outs = []
    for it in items:
        outs.append(_dot(it["qd"], it["sb"]) + _dot(it["aqk"], it["v_new"]))
        it["s_ref"][...] = it["s"] * jnp.exp(it["glast"]) + _dot(it["kdt"], it["v_new"])
    return outs


def _gdn_kernel(qf_ref, kf_ref, vf_ref, gf_ref, qb_ref, kb_ref, vb_ref, gb_ref, par_ref, of_ref, ob_ref, s_ref):
    @pl.when(pl.program_id(1) == 0)
    def _():
        s_ref[...] = jnp.zeros(s_ref.shape, F32)

    n_chunks = TOK_TILE // CHUNK
    neg_a = -jnp.exp(par_ref[0:1, :])
    dt_bias = par_ref[1:2, :]
    r2 = lax.broadcasted_iota(jnp.int32, (CHUNK, CHUNK), 0)
    c2 = lax.broadcasted_iota(jnp.int32, (CHUNK, CHUNK), 1)

    def body(ci, carry):
        items, dests = [], []
        for d, (q_ref, k_ref, v_ref, g_ref, o_ref) in enumerate(((qf_ref, kf_ref, vf_ref, gf_ref, of_ref),
                                                                  (qb_ref, kb_ref, vb_ref, gb_ref, ob_ref))):
            cc = ci if d == 0 else n_chunks - 1 - ci
            rows = pl.ds(pl.multiple_of(cc * CHUNK, CHUNK), CHUNK)
            ab = g_ref[0, rows, :]
            xa = ab + dt_bias
            softplus = jnp.maximum(xa, 0.0) + jnp.log(1.0 + jnp.exp(-jnp.abs(xa)))
            tri = jnp.where((r2 <= c2) if d == 1 else (r2 >= c2), 1.0, 0.0)
            gc = _dot(tri, neg_a * softplus, HIGHEST)
            gct = gc.T
            beta = _sigmoid(ab)
            last = 0 if d == 1 else CHUNK - 1
            for h in range(REC_HEADS):
                cols = slice(h * REC_D, (h + 1) * REC_D)
                ln = d * REC_HEADS + h
                items.append(dict(
                    q=q_ref[0, rows, cols], k=k_ref[0, rows, cols], v=v_ref[0, rows, cols],
                    gcol=gc[:, ln:ln + 1], grow=gct[ln:ln + 1, :], glast=gc[last:last + 1, ln:ln + 1],
                    beta=beta[:, 2 * REC_HEADS + ln:2 * REC_HEADS + ln + 1], s_ref=s_ref.at[d, h],
                    reverse=(d == 1)))
                dests.append((o_ref, rows, cols))
        for (o_ref, rows, cols), o in zip(dests, _gdn_chunks(items)):
            o_ref[0, rows, cols] = o
        return carry

    lax.fori_loop(0, n_chunks, body, 0)


def _gdn(qkv, z_gdn, par):
    b, lt, _ = qkv.shape
    n = lt // TOK_TILE
    fwd = lambda k: pl.BlockSpec((1, TOK_TILE, REC_W), lambda i, s: (i, s, k))
    bwd = lambda k: pl.BlockSpec((1, TOK_TILE, REC_W), lambda i, s: (i, _rev_tile(s, n), k))
    gcol = ZG_CONV // 128
    return pl.pallas_call(
        _gdn_kernel,
        grid=(b, n),
        in_specs=[fwd(0), fwd(1), fwd(2), pl.BlockSpec((1, TOK_TILE, 128), lambda i, s: (i, s, gcol)),
                  bwd(0), bwd(1), bwd(2), pl.BlockSpec((1, TOK_TILE, 128), lambda i, s: (i, _rev_tile(s, n), gcol)),
                  pl.BlockSpec((8, 128), lambda i, s: (0, 0))],
        out_specs=[pl.BlockSpec((1, TOK_TILE, REC_W), lambda i, s: (i, s, 0)),
                   pl.BlockSpec((1, TOK_TILE, REC_W), lambda i, s: (i, _rev_tile(s, n), 0))],
        out_shape=[jax.ShapeDtypeStruct((b, lt, REC_W), F32)] * 2,
        scratch_shapes=[pltpu.VMEM((2, REC_HEADS, REC_D, REC_D), F32)],
        compiler_params=_cp(("arbitrary", "arbitrary")),
        name="gdn_scan",
    )(qkv, qkv, qkv, z_gdn, qkv, qkv, qkv, z_gdn, par)


def _layer_norm(x, g, b):
    mu = jnp.mean(x, axis=-1, keepdims=True)
    xc = x - mu
    var = jnp.mean(xc * xc, axis=-1, keepdims=True)
    return xc * lax.rsqrt(var + LN_EPS) * g + b


def _head_norm_gate(o, gate, w):
    outs = []
    for h in range(REC_HEADS):
        cols = slice(h * REC_D, (h + 1) * REC_D)
        oh = o[:, cols]
        n = oh * lax.rsqrt(jnp.mean(oh * oh, axis=-1, keepdims=True) + RMS_EPS) * w
        outs.append(n * _silu(gate[:, cols]))
    return jnp.concatenate(outs, axis=1)


def _merge_kernel(mla_ref, hf_ref, hb_ref, gf_ref, gb_ref, gates_ref, hgate_ref, ggate_ref, x_ref,
                  m2_ref, m3_ref, m4_ref, hw_ref, gw_ref, wb_ref, wo_ref, lg_ref, lbias_ref, wr_ref,
                  x1_ref, h2_ref, lt_ref):
    hg = _head_norm_gate(hf_ref[0] + hb_ref[0], hgate_ref[0], hw_ref[...]).astype(BF16)
    gd = _head_norm_gate(gf_ref[0] + gb_ref[0], ggate_ref[0], gw_ref[...]).astype(BF16)
    y = jnp.zeros((TOK_TILE, D_MODEL), F32)
    for n, o in enumerate((mla_ref[0], hg, gd)):
        y = y + _sigmoid(gates_ref[0, :, n * D_MODEL:(n + 1) * D_MODEL]) * _dot(o, wb_ref[n])
    y = _dot(y.astype(BF16), wo_ref[...])
    x1 = _layer_norm(DEEPNORM_ALPHA * x_ref[0] + m2_ref[0] * y, lg_ref[...], lbias_ref[...])
    x1_ref[0] = x1
    h2 = x1 * (1.0 + m4_ref[0]) + m3_ref[0]
    h2_ref[0] = _pack_rows(h2)
    lt_ref[...] = _dot_nt(wr_ref[...], h2, HIGHEST)


def _merge(mla_o, hg_f, hg_b, gd_f, gd_b, z_main, x_all, mod_l, hg_w, gdn_w, wb, wo, ln_g, ln_b, wr_t):
    b, lt, d = x_all.shape
    n = lt // TOK_TILE
    tok = lambda w, k=0: pl.BlockSpec((1, TOK_TILE, w), lambda i, t: (i, t, k))
    full = lambda a: pl.BlockSpec(a.shape, lambda i, t: (0,) * a.ndim)
    return pl.pallas_call(
        _merge_kernel,
        grid=(b, n),
        in_specs=[tok(BRANCH_W), tok(REC_W), tok(REC_W), tok(REC_W), tok(REC_W),
                  tok(N_BRANCH * D_MODEL, 0), tok(REC_W, ZM_HG // REC_W + 4), tok(REC_W, ZM_GGATE // REC_W),
                  tok(d), _mod_spec(b, 2), _mod_spec(b, 3), _mod_spec(b, 4),
                  full(hg_w), full(gdn_w), full(wb), full(wo), full(ln_g), full(ln_b), full(wr_t)],
        out_specs=[tok(d), tok(d // 2), pl.BlockSpec((N_EXPERTS, TOK_TILE), lambda i, t: (0, i * n + t))],
        out_shape=[jax.ShapeDtypeStruct((b, lt, d), F32), jax.ShapeDtypeStruct((b, lt, d // 2), jnp.int32),
                   jax.ShapeDtypeStruct((N_EXPERTS, b * lt), F32)],
        compiler_params=_cp(("arbitrary", "arbitrary")),
        name="merge",
    )(mla_o, hg_f, hg_b, gd_f, gd_b, z_main, z_main, z_main, x_all, mod_l, mod_l, mod_l,
      hg_w, gdn_w, wb, wo, ln_g, ln_b, wr_t)


def _first_max(x, idx, axes):
    m = x
    for ax in axes:
        m = jnp.max(m, axis=ax, keepdims=True)
    first = jnp.where(x == m, idx, jnp.int32(2 ** 30))
    for ax in axes:
        first = jnp.min(first, axis=ax, keepdims=True)
    return m, first


def _route_kernel(lt_ref, bias_ref, e_ref, r_ref, w_ref, cnt_ref, run_ref):
    @pl.when(pl.program_id(0) == 0)
    def _():
        run_ref[...] = jnp.zeros(run_ref.shape, F32)

    n_tok = lt_ref.shape[1]
    per = N_EXPERTS // N_GROUPS
    scores = _sigmoid(lt_ref[...]).reshape(N_GROUPS, per, n_tok)
    sel = scores + bias_ref[...]
    ig = lax.broadcasted_iota(jnp.int32, sel.shape, 0)
    ij = lax.broadcasted_iota(jnp.int32, sel.shape, 1)
    top1, a1 = _first_max(sel, ij, (1,))
    top2 = jnp.max(jnp.where(ij == a1, -jnp.inf, sel), axis=1, keepdims=True)
    grp = top1 + top2
    igg = lax.broadcasted_iota(jnp.int32, grp.shape, 0)
    gsel = jnp.zeros(grp.shape, F32)
    for _ in range(TOPK_GROUPS):
        _, a = _first_max(grp, igg, (0,))
        hit = igg == a
        gsel = jnp.where(hit, 1.0, gsel)
        grp = jnp.where(hit, -jnp.inf, grp)
    cur = jnp.where(gsel > 0.5, sel, -jnp.inf)
    ie = ig * per + ij
    esel = jnp.zeros(cur.shape, F32)
    for _ in range(TOP_K):
        _, a = _first_max(cur, ie, (1, 0))
        hit = ie == a
        esel = jnp.where(hit, 1.0, esel)
        cur = jnp.where(hit, -jnp.inf, cur)
    w = scores * esel
    tot = jnp.sum(jnp.sum(w, axis=1, keepdims=True), axis=0, keepdims=True)
    w = (w / tot * ROUTED_SCALE).reshape(N_EXPERTS, n_tok)
    m = esel.reshape(N_EXPERTS, n_tok)
    mb = m.astype(BF16)
    ti = lax.broadcasted_iota(jnp.int32, (n_tok, n_tok), 0)
    tj = lax.broadcasted_iota(jnp.int32, (n_tok, n_tok), 1)
    upto = _dot(mb, jnp.where(ti <= tj, 1.0, 0.0).astype(BF16))
    ei = lax.broadcasted_iota(jnp.int32, (N_EXPERTS, N_EXPERTS), 0)
    ej = lax.broadcasted_iota(jnp.int32, (N_EXPERTS, N_EXPERTS), 1)
    lower = _dot(jnp.where(ej < ei, 1.0, 0.0).astype(BF16), mb)
    run = run_ref[:, 0:1]
    rank = run + upto - 1.0
    run_ref[...] = jnp.broadcast_to(run + upto[:, n_tok - 1:n_tok], run_ref.shape)
    cnt_ref[...] = run_ref[...]
    eid = lax.broadcasted_iota(jnp.int32, m.shape, 0).astype(F32)
    rows_e, rows_r, rows_w = [], [], []
    for k in range(TOP_K):
        pick = jnp.where(lower == float(k), m, 0.0)
        rows_e.append(jnp.sum(pick * eid, axis=0, keepdims=True))
        rows_r.append(jnp.sum(pick * rank, axis=0, keepdims=True))
        rows_w.append(jnp.sum(pick * w, axis=0, keepdims=True))
    e_ref[...] = jnp.concatenate(rows_e, axis=0).astype(jnp.int32)
    r_ref[...] = jnp.concatenate(rows_r, axis=0).astype(jnp.int32)
    w_ref[...] = jnp.concatenate(rows_w, axis=0).T


ROUTE_TILE = 512


def _route(logits_t, bias):
    n_tok = logits_t.shape[1]
    tt = ROUTE_TILE
    return pl.pallas_call(
        _route_kernel,
        grid=(n_tok // tt,),
        in_specs=[pl.BlockSpec((N_EXPERTS, tt), lambda i: (0, i)),
                  pl.BlockSpec((N_GROUPS, N_EXPERTS // N_GROUPS, 1), lambda i: (0, 0, 0))],
        out_specs=[pl.BlockSpec((TOP_K, tt), lambda i: (0, i)),
                   pl.BlockSpec((TOP_K, tt), lambda i: (0, i)),
                   pl.BlockSpec((tt, TOP_K), lambda i: (i, 0)),
                   pl.BlockSpec((N_EXPERTS, 128), lambda i: (0, 0))],
        out_shape=[jax.ShapeDtypeStruct((TOP_K, n_tok), jnp.int32),
                   jax.ShapeDtypeStruct((TOP_K, n_tok), jnp.int32),
                   jax.ShapeDtypeStruct((n_tok, TOP_K), F32),
                   jax.ShapeDtypeStruct((N_EXPERTS, 128), F32)],
        scratch_shapes=[pltpu.VMEM((N_EXPERTS, 128), F32)],
        compiler_params=_cp(("arbitrary",)),
        name="route",
    )(logits_t, bias)


def _slot_kernel(start_ref, e_ref, r_ref, p_ref):
    e = e_ref[...]
    pos = r_ref[...]
    for j in range(N_EXPERTS):
        pos = pos + jnp.where(e == j, start_ref[j], 0)
    p_ref[...] = pos


def _slots(starts, e_k, r_k):
    n_tok = e_k.shape[1]
    tt = ROUTE_TILE
    spec = pl.BlockSpec((TOP_K, tt), lambda i, s: (0, i))
    return pl.pallas_call(
        _slot_kernel,
        grid_spec=pltpu.PrefetchScalarGridSpec(num_scalar_prefetch=1, grid=(n_tok // tt,),
                                               in_specs=[spec, spec], out_specs=spec),
        out_shape=jax.ShapeDtypeStruct((TOP_K, n_tok), jnp.int32),
        compiler_params=_cp(("arbitrary",)),
        name="moe_slots",
    )(starts, e_k, r_k)


SC_WINDOW = 128
SC_ROW = 256


def _sc_mesh():
    return plsc.VectorSubcoreMesh(core_axis_name="c", subcore_axis_name="s")


def _sc_scatter(x, idx, n_out):
    rows = x.shape[0]
    n = idx.shape[0]
    nb = rows // SC_WINDOW

    @functools.partial(pl.kernel, out_type=jax.ShapeDtypeStruct((n_out, SC_ROW), x.dtype), mesh=_sc_mesh())
    def k(x_hbm, i_hbm, o_hbm):
        def body(x_vmem, i_vmem):
            pltpu.sync_copy(x_vmem, o_hbm.at[i_vmem.at[0]])

        pltpu.emit_pipeline(
            body,
            grid=(n // SC_WINDOW,),
            in_specs=[pl.BlockSpec((SC_WINDOW, SC_ROW), lambda i: (i % nb, 0)),
                      pl.BlockSpec((1, SC_WINDOW), lambda i: (0, i))],
            out_specs=[],
            core_axis_name=("c", "s"),
            dimension_semantics=(pltpu.PARALLEL,),
        )(x_hbm, i_hbm)

    return k(x, idx.reshape(1, n))


def _sc_gather(table, idx):
    n = idx.shape[0]

    @functools.partial(pl.kernel, out_type=jax.ShapeDtypeStruct((n, SC_ROW), table.dtype), mesh=_sc_mesh())
    def k(x_hbm, i_hbm, o_hbm):
        def body(i_vmem, o_vmem):
            pltpu.sync_copy(x_hbm.at[i_vmem.at[0]], o_vmem)

        pltpu.emit_pipeline(
            body,
            grid=(n // SC_WINDOW,),
            in_specs=[pl.BlockSpec((1, SC_WINDOW), lambda i: (0, i))],
            out_specs=[pl.BlockSpec((SC_WINDOW, SC_ROW), lambda i: (i, 0))],
            core_axis_name=("c", "s"),
            dimension_semantics=(pltpu.PARALLEL,),
        )(i_hbm, o_hbm)

    return k(table, idx.reshape(1, n))


MOE_TM = 256


def _expert_kernel(te_ref, tr_ref, x_ref, wgu_ref, wd_ref, y_ref):
    i = pl.program_id(0)
    valid = tr_ref[i]

    @pl.when(valid > 0)
    def _():
        live = lax.broadcasted_iota(jnp.int32, x_ref.shape, 0) < valid
        x = _unpack_rows(jnp.where(live, x_ref[...], 0))
        gu = _dot(x, wgu_ref[0])
        act = _silu(gu[:, :EXPERT_FF]) * gu[:, EXPERT_FF:]
        y_ref[...] = _pack_rows(_dot(act.astype(BF16), wd_ref[0]))

    @pl.when(valid == 0)
    def _():
        y_ref[...] = jnp.zeros(y_ref.shape, y_ref.dtype)


def _experts(tile_expert, tile_rows, xs, wgu, wd):
    n_rows, half = xs.shape
    d = 2 * half
    return pl.pallas_call(
        _expert_kernel,
        grid_spec=pltpu.PrefetchScalarGridSpec(
            num_scalar_prefetch=2,
            grid=(n_rows // MOE_TM,),
            in_specs=[pl.BlockSpec((MOE_TM, half), lambda i, te, tr: (i, 0)),
                      pl.BlockSpec((1, d, 2 * EXPERT_FF), lambda i, te, tr: (te[i], 0, 0)),
                      pl.BlockSpec((1, EXPERT_FF, d), lambda i, te, tr: (te[i], 0, 0))],
            out_specs=pl.BlockSpec((MOE_TM, half), lambda i, te, tr: (i, 0))),
        out_shape=jax.ShapeDtypeStruct((n_rows, half), jnp.int32),
        compiler_params=_cp(("arbitrary",)),
        name="moe_experts",
    )(tile_expert, tile_rows, xs, wgu, wd)


def _combine_kernel(g_ref, w_ref, h_ref, x_ref, m5_ref, wsgu_ref, wsd_ref, lg_ref, lb_ref, o_ref):
    h = _unpack_rows(h_ref[...])
    gu = _dot(h, wsgu_ref[...])
    act = _silu(gu[:, :SHARED_FF]) * gu[:, SHARED_FF:]
    f = _dot(act.astype(BF16), wsd_ref[...])
    w = w_ref[...]
    for k in range(TOP_K):
        f = f + w[:, k:k + 1] * _unpack_rows(g_ref[k]).astype(F32)
    o_ref[0] = _layer_norm(DEEPNORM_ALPHA * x_ref[0] + m5_ref[0] * f, lg_ref[...], lb_ref[...])


def _combine(g, w_k, h2p, x1, mod_l, wsgu, wsd, ln_g, ln_b):
    b, lt, d = x1.shape
    n = lt // TOK_TILE
    half = d // 2
    tok = pl.BlockSpec((1, TOK_TILE, d), lambda i, t: (i, t, 0))
    full = lambda a: pl.BlockSpec(a.shape, lambda i, t: (0,) * a.ndim)
    return pl.pallas_call(
        _combine_kernel,
        grid=(b, n),
        in_specs=[pl.BlockSpec((TOP_K, TOK_TILE, half), lambda i, t: (0, i * n + t, 0)),
                  pl.BlockSpec((TOK_TILE, TOP_K), lambda i, t: (i * n + t, 0)),
                  pl.BlockSpec((TOK_TILE, half), lambda i, t: (i * n + t, 0)),
                  tok, _mod_spec(b, 5), full(wsgu), full(wsd), full(ln_g), full(ln_b)],
        out_specs=tok,
        out_shape=jax.ShapeDtypeStruct((b, lt, d), F32),
        compiler_params=_cp(("arbitrary", "arbitrary")),
        name="moe_combine",
    )(g, w_k, h2p, x1, mod_l, wsgu, wsd, ln_g, ln_b)


def _moe_sparse(h2p, logits_t, bias, wgu, wd, wsgu, wsd, x1, mod_l, ln_g, ln_b):
    n_tok, half = h2p.shape
    e_k, r_k, w_k, counts = _route(logits_t, bias)
    counts = counts[:, 0].astype(jnp.int32)
    padded = (counts + MOE_TM - 1) // MOE_TM * MOE_TM
    ends = jnp.cumsum(padded)
    starts = ends - padded
    n_rows = (n_tok * TOP_K + N_EXPERTS * (MOE_TM - 1)) // MOE_TM * MOE_TM
    tile0 = jnp.arange(n_rows // MOE_TM, dtype=jnp.int32) * MOE_TM
    tile_expert = jnp.minimum(jnp.sum(tile0[:, None] >= ends[None, :], axis=1), N_EXPERTS - 1).astype(jnp.int32)
    tile_rows = jnp.clip(counts[tile_expert] - (tile0 - starts[tile_expert]), 0, MOE_TM).astype(jnp.int32)
    pos = _slots(starts.astype(jnp.int32), e_k, r_k)
    parts = half // SC_ROW
    idx = (pos[:, :, None] * parts + jnp.arange(parts, dtype=jnp.int32)).reshape(-1)
    xs = _sc_scatter(h2p.reshape(n_tok * parts, SC_ROW), idx, n_rows * parts)
    ys = _experts(tile_expert, tile_rows, xs.reshape(n_rows, half), wgu, wd)
    g = _sc_gather(ys.reshape(n_rows * parts, SC_ROW), idx).reshape(TOP_K, n_tok, half)
    return _combine(g, w_k, h2p, x1, mod_l, wsgu, wsd, ln_g, ln_b)


def _rope_tables(n_ctx, n_lat):
    rows = n_lat // GRID_W
    row = jnp.broadcast_to(jnp.arange(rows, dtype=F32)[:, None], (rows, GRID_W)).reshape(-1)
    col = jnp.broadcast_to(jnp.arange(GRID_W, dtype=F32)[None, :], (rows, GRID_W)).reshape(-1)
    inv = ROPE_BASE ** (-jnp.arange(ROPE_FREQS, dtype=F32) / ROPE_FREQS)
    ang = jnp.stack([row[:, None] * inv, col[:, None] * inv], axis=1)
    cos = jnp.cos(ang)[:, :, None, :]
    sin = jnp.sin(ang)[:, :, None, :]
    cos32 = jnp.broadcast_to(cos, (n_lat, 2, 2, ROPE_FREQS)).reshape(n_lat, MLA_ROPE)
    sin32 = jnp.concatenate([-sin, sin], axis=2).reshape(n_lat, MLA_ROPE)
    cos32 = jnp.concatenate([jnp.ones((n_ctx, MLA_ROPE), F32), cos32], axis=0)
    sin32 = jnp.concatenate([jnp.zeros((n_ctx, MLA_ROPE), F32), sin32], axis=0)
    lt = n_ctx + n_lat
    ca = jnp.concatenate([jnp.ones((lt, MLA_NOPE), F32), cos32, jnp.zeros((lt, 32), F32)], axis=1)
    sb = jnp.concatenate([jnp.zeros((lt, MLA_NOPE), F32), sin32, jnp.zeros((lt, 32), F32)], axis=1)
    return ca, sb


def _rope_partner():
    idx = np.arange(MLA_ROPE).reshape(2, 2, ROPE_FREQS)
    return idx[:, ::-1, :].reshape(-1)


def _pack_layer(w_in, w_q_b, w_kv_b):
    sizes = (MLA_Q_LORA, MLA_KV_LORA + MLA_ROPE, REC_W, REC_W, REC_W, REC_W, REC_W,
             3 * REC_W, REC_W, 2 * REC_HEADS, 2 * REC_HEADS, N_BRANCH * D_MODEL)
    offs = np.cumsum((0,) + sizes)
    seg = lambda i: w_in[:, offs[i]:offs[i + 1]]
    d = w_in.shape[0]
    zeros = lambda n: jnp.zeros((d, n), w_in.dtype)
    partner = _rope_partner()
    kva = seg(1)
    k_rope = kva[:, MLA_KV_LORA:]
    w_main = jnp.concatenate([seg(11), seg(2), seg(3), seg(4), seg(5), seg(6), seg(8)], axis=1)
    w_mla = jnp.concatenate([seg(0), kva[:, :MLA_KV_LORA],
                             zeros(MLA_NOPE), k_rope, zeros(32),
                             zeros(MLA_NOPE), k_rope[:, partner], zeros(32)], axis=1)
    w_gdn = jnp.concatenate([seg(7), seg(9), seg(10), zeros(256 - 4 * REC_HEADS)], axis=1)
    r = w_q_b.shape[0]
    qb = w_q_b.reshape(r, MLA_HEADS, MLA_NOPE + MLA_ROPE)
    zq = lambda n: jnp.zeros((r, MLA_HEADS, n), w_q_b.dtype)
    wq1 = jnp.concatenate([qb, zq(32)], axis=2).reshape(r, MLA_HEADS * 128)
    wq2 = jnp.concatenate([zq(MLA_NOPE), qb[:, :, MLA_NOPE:][:, :, partner], zq(32)], axis=2).reshape(r, MLA_HEADS * 128)
    rk = w_kv_b.shape[0]
    kvb = w_kv_b.reshape(rk, MLA_HEADS, MLA_NOPE + MLA_V)
    wk = jnp.concatenate([kvb[:, :, :MLA_NOPE], jnp.zeros((rk, MLA_HEADS, 64), w_kv_b.dtype)], axis=2)
    wk = wk.reshape(rk, MLA_HEADS * 128)
    wv = kvb[:, :, MLA_NOPE:].reshape(rk, MLA_HEADS * MLA_V)
    bf = lambda a: a.astype(BF16)
    return bf(w_main), bf(w_mla), bf(w_gdn), bf(wq1), bf(wq2), bf(wk), bf(wv)


def kernel(x, c, ctx, c_ctx, w_mod, b_mod, w_in, q_a_norm, w_q_b, kv_a_norm, w_kv_b, hg_lb_logits, hg_norm,
           gdn_conv, gdn_a_log, gdn_dt_bias, gdn_norm, w_branch, w_out, ln1_g, ln1_b, ln2_g, ln2_b,
           w_router, router_bias, w_gu, w_down, w_sh_gu, w_sh_down):
    batch, n_lat, d = x.shape
    n_ctx = ctx.shape[1]
    assert n_ctx == TOK_TILE and n_lat % TOK_TILE == 0 and batch < MOD_ROWS and d == D_MODEL
    lt = n_ctx + n_lat

    c_all = jnp.zeros((MOD_ROWS, d), F32).at[:batch].set(c).at[batch].set(c_ctx)
    mod = _mod_all(c_all, w_mod, b_mod).reshape(DEPTH, MOD_ROWS, 1, 6 * d)
    ca, sb = _rope_tables(n_ctx, n_lat)
    lb_soft = jax.nn.softmax(hg_lb_logits.astype(F32), axis=0)
    lower = (jnp.cumsum(lb_soft, axis=0) - lb_soft[0]).reshape(DEPTH, 2, 1, REC_W)
    row = lambda a: a.reshape(1, -1)

    x_all = jnp.concatenate([ctx, x], axis=1)
    for l in range(DEPTH):
        w_main, w_mla, w_gdn, wq1, wq2, wk, wv = _pack_layer(w_in[l], w_q_b[l], w_kv_b[l])
        mod_l = mod[l]
        h = _modulate(x_all, mod_l)
        z_main = _proj(h, w_main, 512)
        z_mla = _proj(h, w_mla, ZA_WIDTH)
        z_gdn = _proj(h, w_gdn, ZG_WIDTH // 2)

        q, k, v = _mla_prep(z_mla, row(q_a_norm[l]), row(kv_a_norm[l]), wq1, wq2, wk, wv, ca, sb)
        mla_o = _attention(q, k, v, n_ctx)

        hg_f, hg_b = _hgrn2(z_main, lower[l])

        qkv = _gdn_prep(z_gdn, gdn_conv[l], n_ctx)
        par = jnp.zeros((8, 128), F32)
        par = par.at[0, :2 * REC_HEADS].set(gdn_a_log[l].reshape(-1))
        par = par.at[1, :2 * REC_HEADS].set(gdn_dt_bias[l].reshape(-1))
        gd_f, gd_b = _gdn(qkv, z_gdn, par)

        x1, h2, logits_t = _merge(mla_o, hg_f, hg_b, gd_f, gd_b, z_main, x_all, mod_l,
                                  row(hg_norm[l]), row(gdn_norm[l]), w_branch[l].astype(BF16),
                                  w_out[l].astype(BF16), row(ln1_g[l]), row(ln1_b[l]), w_router[l].T)
        x_all = _moe_sparse(h2.reshape(batch * lt, d // 2), logits_t,
                            router_bias[l].reshape(N_GROUPS, N_EXPERTS // N_GROUPS, 1),
                            w_gu[l].astype(BF16), w_down[l].astype(BF16),
                            w_sh_gu[l].astype(BF16), w_sh_down[l].astype(BF16),
                            x1, mod_l, row(ln2_g[l]), row(ln2_b[l]))
    return x_all[:, n_ctx:, :]
```

```python
import functools
import math

import numpy as np
import jax
import jax.numpy as jnp
from jax import lax
from jax.experimental import pallas as pl
from jax.experimental.pallas import tpu as pltpu
from jax.experimental.pallas import tpu_sc as plsc

F32 = jnp.float32
BF16 = jnp.bfloat16
HIGHEST = lax.Precision.HIGHEST

D_MODEL = 1024
DEPTH = 4
GRID_W = 64
MLA_HEADS = 8
MLA_Q_LORA = 384
MLA_KV_LORA = 256
MLA_NOPE = 64
MLA_ROPE = 32
MLA_V = 64
MLA_SCALE = (MLA_NOPE + MLA_ROPE) ** -0.5
ROPE_BASE = 10000.0
ROPE_FREQS = MLA_ROPE // 4
REC_HEADS = 4
REC_D = 128
REC_W = REC_HEADS * REC_D
CONV_K = 5
N_BRANCH = 3
BRANCH_W = 512
N_EXPERTS = 64
TOP_K = 8
N_GROUPS = 8
TOPK_GROUPS = 4
EXPERT_FF = 256
SHARED_FF = 256
ROUTED_SCALE = 2.5
DEEPNORM_ALPHA = (2 * DEPTH) ** 0.25
LN_EPS = 1e-6
RMS_EPS = 1e-6

TOK_TILE = 256
CHUNK = 64
MOD_ROWS = 16
VMEM_LIMIT = 56 * 1024 * 1024

ZM_Q, ZM_I, ZM_HGATE, ZM_GGATE = (3 * D_MODEL // REC_W + k for k in range(4))
ZA_WIDTH = MLA_Q_LORA + MLA_KV_LORA + 256
ZG_CONV = 3 * REC_W
ZG_WIDTH = ZG_CONV + 256


def _cp(sem, vmem=VMEM_LIMIT):
    return pltpu.CompilerParams(dimension_semantics=sem, vmem_limit_bytes=vmem)


def _dot(a, b, precision=None):
    return jnp.dot(a, b, preferred_element_type=F32, precision=precision)


def _dot_nt(a, b, precision=None):
    return lax.dot_general(a, b, (((1,), (1,)), ((), ())), preferred_element_type=F32, precision=precision)


def _sigmoid(x):
    return 1.0 / (1.0 + jnp.exp(-x))


def _silu(x):
    return x * _sigmoid(x)


ROW_PARTS = 2
SC_ROW = D_MODEL // (2 * ROW_PARTS)
SC_WINDOW = 128


def _pack_parts(x):
    q = x.shape[1] // (2 * ROW_PARTS)
    bits = lambda a: lax.bitcast_convert_type(a.astype(BF16).astype(F32), jnp.uint32)
    parts = []
    for p in range(ROW_PARTS):
        hi = bits(x[:, p * q:(p + 1) * q])
        lo = bits(x[:, (ROW_PARTS + p) * q:(ROW_PARTS + p + 1) * q])
        parts.append(lax.bitcast_convert_type(hi | lax.shift_right_logical(lo, jnp.uint32(16)), jnp.int32))
    return parts


def _unpack_parts(parts):
    his, los = [], []
    for p in parts:
        u = lax.bitcast_convert_type(p, jnp.uint32)
        his.append(lax.bitcast_convert_type(u & jnp.uint32(0xFFFF0000), F32).astype(BF16))
        los.append(lax.bitcast_convert_type(lax.shift_left(u, jnp.uint32(16)), F32).astype(BF16))
    return jnp.concatenate(his + los, axis=1)


def _mod_kernel(c_ref, w_ref, b_ref, o_ref):
    s = _silu(c_ref[...])
    o_ref[0] = _dot(s, w_ref[0], HIGHEST) + b_ref[0]


def _mod_all(c_all, w_mod, b_mod):
    tn = 1024
    n = w_mod.shape[-1]
    return pl.pallas_call(
        _mod_kernel,
        grid=(DEPTH, n // tn),
        in_specs=[
            pl.BlockSpec((MOD_ROWS, D_MODEL), lambda l, j: (0, 0)),
            pl.BlockSpec((1, D_MODEL, tn), lambda l, j: (l, 0, j)),
            pl.BlockSpec((1, 1, tn), lambda l, j: (l, 0, j)),
        ],
        out_specs=pl.BlockSpec((1, MOD_ROWS, tn), lambda l, j: (l, 0, j)),
        out_shape=jax.ShapeDtypeStruct((DEPTH, MOD_ROWS, n), F32),
        compiler_params=_cp(("arbitrary", "arbitrary")),
        name="mod_all",
    )(c_all, w_mod, b_mod.reshape(DEPTH, 1, n))


def _mod_spec(batch, k):
    return pl.BlockSpec((1, 1, D_MODEL), lambda b, t: (jnp.where(t == 0, batch, b), 0, k))


def _modulate_kernel(x_ref, sh_ref, sc_ref, o_ref):
    o_ref[0] = (x_ref[0] * (1.0 + sc_ref[0]) + sh_ref[0]).astype(o_ref.dtype)


def _modulate(x_all, mod_l):
    b, lt, d = x_all.shape
    return pl.pallas_call(
        _modulate_kernel,
        grid=(b, lt // TOK_TILE),
        in_specs=[
            pl.BlockSpec((1, TOK_TILE, d), lambda i, t: (i, t, 0)),
            _mod_spec(b, 0),
            _mod_spec(b, 1),
        ],
        out_specs=pl.BlockSpec((1, TOK_TILE, d), lambda i, t: (i, t, 0)),
        out_shape=jax.ShapeDtypeStruct((b, lt, d), BF16),
        compiler_params=_cp(("arbitrary", "arbitrary")),
        name="modulate",
    )(x_all, mod_l, mod_l)


def _proj_kernel(h_ref, w_ref, o_ref, *, rows):
    def body(r, carry):
        sl = pl.ds(pl.multiple_of(r * rows, rows), rows)
        o_ref[0, sl, :] = _dot(h_ref[0, sl, :], w_ref[...]).astype(o_ref.dtype)
        return carry

    lax.fori_loop(0, h_ref.shape[1] // rows, body, 0)


def _proj(h, w, tn, dtype=F32):
    b, lt, d = h.shape
    n = w.shape[1]
    return pl.pallas_call(
        functools.partial(_proj_kernel, rows=TOK_TILE),
        grid=(b, n // tn),
        in_specs=[
            pl.BlockSpec((1, lt, d), lambda i, j: (i, 0, 0)),
            pl.BlockSpec((d, tn), lambda i, j: (0, j)),
        ],
        out_specs=pl.BlockSpec((1, lt, tn), lambda i, j: (i, 0, j)),
        out_shape=jax.ShapeDtypeStruct((b, lt, n), dtype),
        compiler_params=_cp(("arbitrary", "arbitrary")),
        name="proj",
    )(h, w)


def _rms(x, g):
    return x * lax.rsqrt(jnp.mean(x * x, axis=-1, keepdims=True) + RMS_EPS) * g


def _mla_prep_kernel(z_ref, qg_ref, kg_ref, wq1_ref, wq2_ref, wk_ref, wv_ref, ca_ref, sb_ref,
                     q_ref, k_ref, v_ref):
    z = z_ref[0]
    ca = ca_ref[...]
    sb = sb_ref[...]
    qn = _rms(z[:, :MLA_Q_LORA], qg_ref[...]).astype(BF16)
    qa = _dot(qn, wq1_ref[...])
    qb = _dot(qn, wq2_ref[...])
    kvn = _rms(z[:, MLA_Q_LORA:MLA_Q_LORA + MLA_KV_LORA], kg_ref[...]).astype(BF16)
    kn = _dot(kvn, wk_ref[...])
    v_ref[0] = _dot(kvn, wv_ref[...]).astype(v_ref.dtype)
    o = MLA_Q_LORA + MLA_KV_LORA
    kr = z[:, o:o + 128] * ca + z[:, o + 128:o + 256] * sb
    for h in range(MLA_HEADS):
        sl = slice(h * 128, (h + 1) * 128)
        q_ref[0, :, sl] = ((qa[:, sl] * ca + qb[:, sl] * sb) * MLA_SCALE).astype(q_ref.dtype)
        k_ref[0, :, sl] = (kn[:, sl] + kr).astype(k_ref.dtype)


def _mla_prep(z_mla, qg, kg, wq1, wq2, wk, wv, ca, sb):
    b, lt, _ = z_mla.shape
    full = lambda a: pl.BlockSpec(a.shape, lambda i, t: (0,) * a.ndim)
    tok = lambda w: pl.BlockSpec((1, TOK_TILE, w), lambda i, t: (i, t, 0))
    return pl.pallas_call(
        _mla_prep_kernel,
        grid=(b, lt // TOK_TILE),
        in_specs=[tok(ZA_WIDTH), full(qg), full(kg), full(wq1), full(wq2), full(wk), full(wv),
                  pl.BlockSpec((TOK_TILE, 128), lambda i, t: (t, 0)),
                  pl.BlockSpec((TOK_TILE, 128), lambda i, t: (t, 0))],
        out_specs=[tok(MLA_HEADS * 128), tok(MLA_HEADS * 128), tok(MLA_HEADS * MLA_V)],
        out_shape=[jax.ShapeDtypeStruct((b, lt, MLA_HEADS * 128), BF16),
                   jax.ShapeDtypeStruct((b, lt, MLA_HEADS * 128), BF16),
                   jax.ShapeDtypeStruct((b, lt, MLA_HEADS * MLA_V), BF16)],
        compiler_params=_cp(("arbitrary", "arbitrary")),
        name="mla_prep",
    )(z_mla, qg, kg, wq1, wq2, wk, wv, ca, sb)


def _attn_kernel(q_ref, k_ref, v_ref, o_ref, *, n_ctx):
    def attend(nk):
        outs = []
        for h in range(2):
            q = q_ref[0, :, h * 128:(h + 1) * 128]
            k = k_ref[0, :nk, h * 128:(h + 1) * 128]
            s = _dot_nt(q, k)
            p = jnp.exp(s - jnp.max(s, axis=1, keepdims=True))
            l = jnp.sum(p, axis=1, keepdims=True)
            o = _dot(p.astype(BF16), v_ref[0, :nk, h * MLA_V:(h + 1) * MLA_V])
            outs.append(o / l)
        o_ref[0] = jnp.concatenate(outs, axis=1).astype(o_ref.dtype)

    @pl.when(pl.program_id(2) == 0)
    def _():
        attend(n_ctx)

    @pl.when(pl.program_id(2) > 0)
    def _():
        attend(k_ref.shape[1])


def _attention(q, k, v, n_ctx):
    b, lt, _ = q.shape
    return pl.pallas_call(
        functools.partial(_attn_kernel, n_ctx=n_ctx),
        grid=(b, MLA_HEADS // 2, lt // TOK_TILE),
        in_specs=[
            pl.BlockSpec((1, TOK_TILE, 256), lambda i, h, t: (i, t, h)),
            pl.BlockSpec((1, lt, 256), lambda i, h, t: (i, 0, h)),
            pl.BlockSpec((1, lt, 2 * MLA_V), lambda i, h, t: (i, 0, h)),
        ],
        out_specs=pl.BlockSpec((1, TOK_TILE, 2 * MLA_V), lambda i, h, t: (i, t, h)),
        out_shape=jax.ShapeDtypeStruct((b, lt, MLA_HEADS * MLA_V), BF16),
        compiler_params=_cp(("arbitrary", "arbitrary", "arbitrary")),
        name="mla_attention",
    )(q, k, v)


def _rev_tile(s, n):
    return jnp.where(s == 0, 0, n - s)


def _time_index(shape, dim, reverse):
    i = lax.broadcasted_iota(jnp.int32, shape, dim)
    return (shape[dim] - 1 - i) if reverse else i


def _blk(i, size):
    return lax.shift_right_logical(i, jnp.int32(int(math.log2(size))))


def _gla_chunks(items):
    c_len = items[0]["q"].shape[0]
    for it in items:
        it["c"] = it["g"]
        it["t"] = it["g"]
        it["a"] = jnp.zeros((c_len, c_len), F32)
    hb = 1
    while hb < c_len:
        for it in items:
            rev = it["reverse"]
            ri = _time_index((c_len, c_len), 0, rev)
            ci = _time_index((c_len, c_len), 1, rev)
            c, t = it["c"], it["t"]
            qt = (it["q"] * jnp.exp(c)).astype(BF16)
            kt = (it["k"] * jnp.exp(t - c)).astype(BF16)
            pair = jnp.logical_and(_blk(ri, hb) == _blk(ci, hb) + 1, (_blk(ri, hb) & 1) == 1)
            it["a"] = it["a"] + jnp.where(pair, _dot_nt(qt, kt), 0.0)
            odd = (_blk(_time_index((c_len, REC_D), 0, rev), hb) & 1) == 1
            t_lo = pltpu.roll(t, hb, 0)
            t_hi = pltpu.roll(t, c_len - hb, 0)
            prev, nxt = (t_hi, t_lo) if rev else (t_lo, t_hi)
            it["c"] = c + jnp.where(odd, prev, 0.0)
            it["t"] = t + jnp.where(odd, prev, nxt)
        hb *= 2
    outs = []
    for it in items:
        q, k, v, c, t = it["q"], it["k"], it["v"], it["c"], it["t"]
        st = it["st_ref"][...]
        qk = jnp.sum(q * k, axis=1, keepdims=True)
        outs.append(_dot_nt((q * jnp.exp(c)).astype(BF16), st.astype(BF16))
                    + _dot(it["a"].astype(BF16), v.astype(BF16)) + qk * v)
        kd = (k * jnp.exp(t - c)).astype(BF16)
        it["st_ref"][...] = st * jnp.exp(t[0:1, :]) + _dot(v.T.astype(BF16), kd)
    return outs


def _hgrn2_kernel(qf_ref, vf_ref, ff_ref, qb_ref, vb_ref, fb_ref, lb_ref, of_ref, ob_ref, st_ref):
    @pl.when(pl.program_id(1) == 0)
    def _():
        st_ref[...] = jnp.zeros(st_ref.shape, F32)

    n_chunks = TOK_TILE // CHUNK

    def body(ci, carry):
        items, dests = [], []
        for d, (q_ref, v_ref, f_ref, o_ref) in enumerate(((qf_ref, vf_ref, ff_ref, of_ref),
                                                           (qb_ref, vb_ref, fb_ref, ob_ref))):
            cc = ci if d == 0 else n_chunks - 1 - ci
            rows = pl.ds(pl.multiple_of(cc * CHUNK, CHUNK), CHUNK)
            for h in range(REC_HEADS):
                cols = slice(h * REC_D, (h + 1) * REC_D)
                lb = lb_ref[d, :, cols]
                f = lb + (1.0 - lb) * _sigmoid(f_ref[0, rows, cols])
                items.append(dict(q=_silu(q_ref[0, rows, cols].astype(F32)), k=1.0 - f,
                                  v=v_ref[0, rows, cols].astype(F32),
                                  g=jnp.log(f), st_ref=st_ref.at[d, h], reverse=(d == 1)))
                dests.append((o_ref, rows, cols))
        for (o_ref, rows, cols), o in zip(dests, _gla_chunks(items)):
            o_ref[0, rows, cols] = o
        return carry

    lax.fori_loop(0, n_chunks, body, 0)


def _hgrn2(z_main, z_f, lb):
    b, lt, _ = z_main.shape
    n = lt // TOK_TILE
    fwd = lambda k: pl.BlockSpec((1, TOK_TILE, REC_W), lambda i, s: (i, s, k))
    bwd = lambda k: pl.BlockSpec((1, TOK_TILE, REC_W), lambda i, s: (i, _rev_tile(s, n), k))
    return pl.pallas_call(
        _hgrn2_kernel,
        grid=(b, n),
        in_specs=[fwd(ZM_Q), fwd(ZM_I), fwd(0), bwd(ZM_Q), bwd(ZM_I), bwd(1),
                  pl.BlockSpec((2, 1, REC_W), lambda i, s: (0, 0, 0))],
        out_specs=[pl.BlockSpec((1, TOK_TILE, REC_W), lambda i, s: (i, s, 0)),
                   pl.BlockSpec((1, TOK_TILE, REC_W), lambda i, s: (i, _rev_tile(s, n), 0))],
        out_shape=[jax.ShapeDtypeStruct((b, lt, REC_W), F32)] * 2,
        scratch_shapes=[pltpu.VMEM((2, REC_HEADS, REC_D, REC_D), F32)],
        compiler_params=_cp(("arbitrary", "arbitrary")),
        name="hgrn2_scan",
    )(z_main, z_main, z_f, z_main, z_main, z_f, lb)


def _gdn_prep_kernel(x_ref, w_ref, o_ref, *, n_ctx):
    x = x_ref[0]
    lt = x.shape[0]
    t = lax.broadcasted_iota(jnp.int32, x.shape, 0)
    lo = jnp.where(t < n_ctx, 0, n_ctx)
    hi = jnp.where(t < n_ctx, n_ctx, lt)
    acc = x * w_ref[CONV_K // 2:CONV_K // 2 + 1, :]
    for kk in range(CONV_K):
        off = kk - CONV_K // 2
        if off == 0:
            continue
        xs = pltpu.roll(x, (-off) % lt, 0)
        ok = jnp.logical_and(t + off >= lo, t + off < hi)
        acc = acc + jnp.where(ok, xs, 0.0) * w_ref[kk:kk + 1, :]
    y = _silu(acc)
    inv = lax.rsqrt(jnp.sum(y * y, axis=1, keepdims=True) + 1e-6)
    is_qk = pl.program_id(1) < 2 * REC_HEADS
    o_ref[0] = y * jnp.where(is_qk, inv, 1.0)


def _gdn_prep(z_gdn, conv_w, n_ctx):
    b, lt, _ = z_gdn.shape
    return pl.pallas_call(
        functools.partial(_gdn_prep_kernel, n_ctx=n_ctx),
        grid=(b, ZG_CONV // REC_D),
        in_specs=[pl.BlockSpec((1, lt, REC_D), lambda i, j: (i, 0, j)),
                  pl.BlockSpec((CONV_K, REC_D), lambda i, j: (0, j))],
        out_specs=pl.BlockSpec((1, lt, REC_D), lambda i, j: (i, 0, j)),
        out_shape=jax.ShapeDtypeStruct((b, lt, ZG_CONV), F32),
        compiler_params=_cp(("arbitrary", "arbitrary")),
        name="gdn_prep",
    )(z_gdn, conv_w)


def _gdn_tile(items):
    t_len = items[0]["q"].shape[0]
    n_chunks = t_len // CHUNK
    for it in items:
        ri = _time_index((t_len, t_len), 0, it["reverse"])
        ci = _time_index((t_len, t_len), 1, it["reverse"])
        it["ri"], it["ci"] = ri, ci
        same = _blk(ri, CHUNK) == _blk(ci, CHUNK)
        it["dec"] = jnp.where(jnp.logical_and(same, ri >= ci),
                              jnp.exp(jnp.minimum(it["gcol"] - it["grow"], 0.0)), 0.0)
        it["kb"] = it["k"] * it["beta"]
        it["kbf"] = it["k"].astype(BF16)
    for it in items:
        ri, ci = it["ri"], it["ci"]
        lm = jnp.where(ri > ci, _dot_nt(it["kb"].astype(BF16), it["kbf"]) * it["dec"], 0.0)
        it["lm"] = lm
        it["tinv"] = jnp.where(ri == ci, 1.0, 0.0) - jnp.where(_blk(ri, 2) == _blk(ci, 2), lm, 0.0)
    hb = 2
    while hb < CHUNK:
        for it in items:
            ri, ci = it["ri"], it["ci"]
            off = jnp.where(_blk(ri, 2 * hb) == _blk(ci, 2 * hb),
                            jnp.where(_blk(ri, hb) == _blk(ci, hb), 0.0, it["lm"]), 0.0)
            it["tb"] = it["tinv"].astype(BF16)
            it["to"] = _dot(it["tb"], off.astype(BF16)).astype(BF16)
        for it in items:
            it["tinv"] = it["tinv"] - _dot(it["to"], it["tb"])
        hb *= 2
    for it in items:
        eg = jnp.exp(it["gcol"])
        tb = it["tinv"].astype(BF16)
        it["u"] = _dot(tb, (it["v"] * it["beta"]).astype(BF16))
        it["w"] = _dot(tb, (it["kb"] * eg).astype(BF16)).astype(BF16)
        qs = it["q"] * (REC_D ** -0.5)
        it["aqk"] = (_dot_nt(qs.astype(BF16), it["kbf"]) * it["dec"]).astype(BF16)
        it["qd"] = (qs * eg).astype(BF16)
        kd = it["k"] * jnp.exp(it["gtot"] - it["gcol"])
        it["kdt"] = [kd[c * CHUNK:(c + 1) * CHUNK].T.astype(BF16) for c in range(n_chunks)]
        it["s"] = it["s_ref"][...]
        it["o"], it["vn"] = [None] * n_chunks, [None] * n_chunks
    for step in range(n_chunks):
        for it in items:
            c = n_chunks - 1 - step if it["reverse"] else step
            it["c"] = c
            rows = slice(c * CHUNK, (c + 1) * CHUNK)
            it["sb"] = it["s"].astype(BF16)
            it["vn"][c] = (it["u"][rows] - _dot(it["w"][rows], it["sb"])).astype(BF16)
        for it in items:
            c = it["c"]
            rows = slice(c * CHUNK, (c + 1) * CHUNK)
            it["o"][c] = _dot(it["qd"][rows], it["sb"])
            it["s"] = (it["s"] * jnp.exp(it["gtot"][c * CHUNK:c * CHUNK + 1])
                       + _dot(it["kdt"][c], it["vn"][c]))
    outs = []
    for it in items:
        it["s_ref"][...] = it["s"]
        outs.append(jnp.concatenate(it["o"], axis=0) + _dot(it["aqk"], jnp.concatenate(it["vn"], axis=0)))
    return outs


def _gdn_kernel(qf_ref, kf_ref, vf_ref, gf_ref, qb_ref, kb_ref, vb_ref, gb_ref, par_ref, of_ref, ob_ref, s_ref):
    @pl.when(pl.program_id(1) == 0)
    def _():
        s_ref[...] = jnp.zeros(s_ref.shape, F32)

    neg_a = -jnp.exp(par_ref[0:1, :])
    dt_bias = par_ref[1:2, :]
    r2 = lax.broadcasted_iota(jnp.int32, (TOK_TILE, TOK_TILE), 0)
    c2 = lax.broadcasted_iota(jnp.int32, (TOK_TILE, TOK_TILE), 1)
    same = _blk(r2, CHUNK) == _blk(c2, CHUNK)
    items, dests = [], []
    for d, (q_ref, k_ref, v_ref, g_ref, o_ref) in enumerate(((qf_ref, kf_ref, vf_ref, gf_ref, of_ref),
                                                              (qb_ref, kb_ref, vb_ref, gb_ref, ob_ref))):
        ab = g_ref[0]
        xa = ab + dt_bias
        g = neg_a * (jnp.maximum(xa, 0.0) + jnp.log(1.0 + jnp.exp(-jnp.abs(xa))))
        tri = jnp.where(jnp.logical_and(same, (r2 <= c2) if d == 1 else (r2 >= c2)), 1.0, 0.0)
        gc = _dot(tri, g, HIGHEST)
        gtot = _dot(jnp.where(same, 1.0, 0.0), g, HIGHEST)
        gct = gc.T
        beta = _sigmoid(ab)
        for h in range(REC_HEADS):
            cols = slice(h * REC_D, (h + 1) * REC_D)
            ln = d * REC_HEADS + h
            items.append(dict(
                q=q_ref[0, :, cols], k=k_ref[0, :, cols], v=v_ref[0, :, cols],
                gcol=gc[:, ln:ln + 1], grow=gct[ln:ln + 1, :], gtot=gtot[:, ln:ln + 1],
                beta=beta[:, 2 * REC_HEADS + ln:2 * REC_HEADS + ln + 1], s_ref=s_ref.at[d, h],
                reverse=(d == 1)))
            dests.append((o_ref, cols))
    for (o_ref, cols), o in zip(dests, _gdn_tile(items)):
        o_ref[0, :, cols] = o


def _gdn(qkv, z_gdn, par):
    b, lt, _ = qkv.shape
    n = lt // TOK_TILE
    fwd = lambda k: pl.BlockSpec((1, TOK_TILE, REC_W), lambda i, s: (i, s, k))
    bwd = lambda k: pl.BlockSpec((1, TOK_TILE, REC_W), lambda i, s: (i, _rev_tile(s, n), k))
    gcol = ZG_CONV // 128
    return pl.pallas_call(
        _gdn_kernel,
        grid=(b, n),
        in_specs=[fwd(0), fwd(1), fwd(2), pl.BlockSpec((1, TOK_TILE, 128), lambda i, s: (i, s, gcol)),
                  bwd(0), bwd(1), bwd(2), pl.BlockSpec((1, TOK_TILE, 128), lambda i, s: (i, _rev_tile(s, n), gcol)),
                  pl.BlockSpec((8, 128), lambda i, s: (0, 0))],
        out_specs=[pl.BlockSpec((1, TOK_TILE, REC_W), lambda i, s: (i, s, 0)),
                   pl.BlockSpec((1, TOK_TILE, REC_W), lambda i, s: (i, _rev_tile(s, n), 0))],
        out_shape=[jax.ShapeDtypeStruct((b, lt, REC_W), F32)] * 2,
        scratch_shapes=[pltpu.VMEM((2, REC_HEADS, REC_D, REC_D), F32)],
        compiler_params=_cp(("arbitrary", "arbitrary")),
        name="gdn_scan",
    )(qkv, qkv, qkv, z_gdn, qkv, qkv, qkv, z_gdn, par)


def _layer_norm(x, g, b):
    mu = jnp.mean(x, axis=-1, keepdims=True)
    xc = x - mu
    var = jnp.mean(xc * xc, axis=-1, keepdims=True)
    return xc * lax.rsqrt(var + LN_EPS) * g + b


def _head_norm_gate(o, gate, w):
    outs = []
    for h in range(REC_HEADS):
        cols = slice(h * REC_D, (h + 1) * REC_D)
        oh = o[:, cols]
        n = oh * lax.rsqrt(jnp.mean(oh * oh, axis=-1, keepdims=True) + RMS_EPS) * w
        outs.append(n * _silu(gate[:, cols].astype(F32)))
    return jnp.concatenate(outs, axis=1)


def _merge_kernel(mla_ref, hf_ref, hb_ref, gf_ref, gb_ref, gates_ref, hgate_ref, ggate_ref, x_ref,
                  m2_ref, m3_ref, m4_ref, hw_ref, gw_ref, wb_ref, wo_ref, lg_ref, lbias_ref, wr_ref,
                  x1_ref, h2_ref, lt_ref):
    hg = _head_norm_gate(hf_ref[0] + hb_ref[0], hgate_ref[0], hw_ref[...]).astype(BF16)
    gd = _head_norm_gate(gf_ref[0] + gb_ref[0], ggate_ref[0], gw_ref[...]).astype(BF16)
    y = jnp.zeros((TOK_TILE, D_MODEL), F32)
    for n, o in enumerate((mla_ref[0], hg, gd)):
        y = y + _sigmoid(gates_ref[0, :, n * D_MODEL:(n + 1) * D_MODEL].astype(F32)) * _dot(o, wb_ref[n])
    y = _dot(y.astype(BF16), wo_ref[...])
    x1 = _layer_norm(DEEPNORM_ALPHA * x_ref[0] + m2_ref[0] * y, lg_ref[...], lbias_ref[...])
    x1_ref[0] = x1
    h2 = x1 * (1.0 + m4_ref[0]) + m3_ref[0]
    for p, part in enumerate(_pack_parts(h2)):
        h2_ref[p, 0] = part
    lt_ref[...] = _dot_nt(wr_ref[...], h2, HIGHEST)


def _merge(mla_o, hg_f, hg_b, gd_f, gd_b, z_main, x_all, mod_l, hg_w, gdn_w, wb, wo, ln_g, ln_b, wr_t):
    b, lt, d = x_all.shape
    n = lt // TOK_TILE
    tok = lambda w, k=0: pl.BlockSpec((1, TOK_TILE, w), lambda i, t: (i, t, k))
    full = lambda a: pl.BlockSpec(a.shape, lambda i, t: (0,) * a.ndim)
    return pl.pallas_call(
        _merge_kernel,
        grid=(b, n),
        in_specs=[tok(BRANCH_W), tok(REC_W), tok(REC_W), tok(REC_W), tok(REC_W),
                  tok(N_BRANCH * D_MODEL, 0), tok(REC_W, ZM_HGATE), tok(REC_W, ZM_GGATE),
                  tok(d), _mod_spec(b, 2), _mod_spec(b, 3), _mod_spec(b, 4),
                  full(hg_w), full(gdn_w), full(wb), full(wo), full(ln_g), full(ln_b), full(wr_t)],
        out_specs=[tok(d), pl.BlockSpec((ROW_PARTS, 1, TOK_TILE, SC_ROW), lambda i, t: (0, i, t, 0)),
                   pl.BlockSpec((N_EXPERTS, TOK_TILE), lambda i, t: (0, i * n + t))],
        out_shape=[jax.ShapeDtypeStruct((b, lt, d), F32),
                   jax.ShapeDtypeStruct((ROW_PARTS, b, lt, SC_ROW), jnp.int32),
                   jax.ShapeDtypeStruct((N_EXPERTS, b * lt), F32)],
        compiler_params=_cp(("arbitrary", "arbitrary")),
        name="merge",
    )(mla_o, hg_f, hg_b, gd_f, gd_b, z_main, z_main, z_main, x_all, mod_l, mod_l, mod_l,
      hg_w, gdn_w, wb, wo, ln_g, ln_b, wr_t)


def _first_max(x, idx, axes):
    m = x
    for ax in axes:
        m = jnp.max(m, axis=ax, keepdims=True)
    first = jnp.where(x == m, idx, jnp.int32(2 ** 30))
    for ax in axes:
        first = jnp.min(first, axis=ax, keepdims=True)
    return m, first


def _route_kernel(lt_ref, bias_ref, e_ref, r_ref, w_ref, cnt_ref, run_ref):
    @pl.when(pl.program_id(0) == 0)
    def _():
        run_ref[...] = jnp.zeros(run_ref.shape, F32)

    n_tok = lt_ref.shape[1]
    per = N_EXPERTS // N_GROUPS
    scores = _sigmoid(lt_ref[...]).reshape(N_GROUPS, per, n_tok)
    sel = scores + bias_ref[...]
    ig = lax.broadcasted_iota(jnp.int32, sel.shape, 0)
    ij = lax.broadcasted_iota(jnp.int32, sel.shape, 1)
    top1, a1 = _first_max(sel, ij, (1,))
    top2 = jnp.max(jnp.where(ij == a1, -jnp.inf, sel), axis=1, keepdims=True)
    grp = top1 + top2
    igg = lax.broadcasted_iota(jnp.int32, grp.shape, 0)
    gsel = jnp.zeros(grp.shape, F32)
    for _ in range(TOPK_GROUPS):
        _, a = _first_max(grp, igg, (0,))
        hit = igg == a
        gsel = jnp.where(hit, 1.0, gsel)
        grp = jnp.where(hit, -jnp.inf, grp)
    cur = jnp.where(gsel > 0.5, sel, -jnp.inf)
    ie = ig * per + ij
    esel = jnp.zeros(cur.shape, F32)
    for _ in range(TOP_K):
        _, a = _first_max(cur, ie, (1, 0))
        hit = ie == a
        esel = jnp.where(hit, 1.0, esel)
        cur = jnp.where(hit, -jnp.inf, cur)
    w = scores * esel
    tot = jnp.sum(jnp.sum(w, axis=1, keepdims=True), axis=0, keepdims=True)
    w = (w / tot * ROUTED_SCALE).reshape(N_EXPERTS, n_tok)
    m = esel.reshape(N_EXPERTS, n_tok)
    mb = m.astype(BF16)
    ti = lax.broadcasted_iota(jnp.int32, (n_tok, n_tok), 0)
    tj = lax.broadcasted_iota(jnp.int32, (n_tok, n_tok), 1)
    upto = _dot(mb, jnp.where(ti <= tj, 1.0, 0.0).astype(BF16))
    ei = lax.broadcasted_iota(jnp.int32, (N_EXPERTS, N_EXPERTS), 0)
    ej = lax.broadcasted_iota(jnp.int32, (N_EXPERTS, N_EXPERTS), 1)
    lower = _dot(jnp.where(ej < ei, 1.0, 0.0).astype(BF16), mb)
    run = run_ref[:, 0:1]
    rank = run + upto - 1.0
    run_ref[...] = jnp.broadcast_to(run + upto[:, n_tok - 1:n_tok], run_ref.shape)
    cnt_ref[...] = run_ref[...]
    eid = lax.broadcasted_iota(jnp.int32, m.shape, 0).astype(F32)
    rows_e, rows_r, rows_w = [], [], []
    for k in range(TOP_K):
        pick = jnp.where(lower == float(k), m, 0.0)
        rows_e.append(jnp.sum(pick * eid, axis=0, keepdims=True))
        rows_r.append(jnp.sum(pick * rank, axis=0, keepdims=True))
        rows_w.append(jnp.sum(pick * w, axis=0, keepdims=True))
    e_ref[...] = jnp.concatenate(rows_e, axis=0).astype(jnp.int32)
    r_ref[...] = jnp.concatenate(rows_r, axis=0).astype(jnp.int32)
    w_ref[...] = jnp.concatenate(rows_w, axis=0).T


ROUTE_TILE = 512


def _route(logits_t, bias):
    n_tok = logits_t.shape[1]
    tt = ROUTE_TILE
    return pl.pallas_call(
        _route_kernel,
        grid=(n_tok // tt,),
        in_specs=[pl.BlockSpec((N_EXPERTS, tt), lambda i: (0, i)),
                  pl.BlockSpec((N_GROUPS, N_EXPERTS // N_GROUPS, 1), lambda i: (0, 0, 0))],
        out_specs=[pl.BlockSpec((TOP_K, tt), lambda i: (0, i)),
                   pl.BlockSpec((TOP_K, tt), lambda i: (0, i)),
                   pl.BlockSpec((tt, TOP_K), lambda i: (i, 0)),
                   pl.BlockSpec((N_EXPERTS, 128), lambda i: (0, 0))],
        out_shape=[jax.ShapeDtypeStruct((TOP_K, n_tok), jnp.int32),
                   jax.ShapeDtypeStruct((TOP_K, n_tok), jnp.int32),
                   jax.ShapeDtypeStruct((n_tok, TOP_K), F32),
                   jax.ShapeDtypeStruct((N_EXPERTS, 128), F32)],
        scratch_shapes=[pltpu.VMEM((N_EXPERTS, 128), F32)],
        compiler_params=_cp(("arbitrary",)),
        name="route",
    )(logits_t, bias)


def _slot_kernel(start_ref, e_ref, r_ref, p_ref, *, n_rows):
    e = e_ref[...]
    pos = r_ref[...]
    for j in range(N_EXPERTS):
        pos = pos + jnp.where(e == j, start_ref[j], 0)
    for p in range(ROW_PARTS):
        p_ref[p] = pos + p * n_rows


def _slots(starts, e_k, r_k, n_rows):
    n_tok = e_k.shape[1]
    tt = ROUTE_TILE
    spec = pl.BlockSpec((TOP_K, tt), lambda i, s: (0, i))
    return pl.pallas_call(
        functools.partial(_slot_kernel, n_rows=n_rows),
        grid_spec=pltpu.PrefetchScalarGridSpec(
            num_scalar_prefetch=1, grid=(n_tok // tt,), in_specs=[spec, spec],
            out_specs=pl.BlockSpec((ROW_PARTS, TOP_K, tt), lambda i, s: (0, 0, i))),
        out_shape=jax.ShapeDtypeStruct((ROW_PARTS, TOP_K, n_tok), jnp.int32),
        compiler_params=_cp(("arbitrary",)),
        name="moe_slots",
    )(starts, e_k, r_k)


def _sc_mesh():
    return plsc.VectorSubcoreMesh(core_axis_name="c", subcore_axis_name="s")


def _sc_scatter(x, idx, n_out):
    n = idx.shape[0]
    nbt = x.shape[0] // ROW_PARTS // SC_WINDOW
    reps = n // x.shape[0]

    @functools.partial(pl.kernel, out_type=jax.ShapeDtypeStruct((n_out, SC_ROW), x.dtype), mesh=_sc_mesh())
    def k(x_hbm, i_hbm, o_hbm):
        def body(x_vmem, i_vmem):
            pltpu.sync_copy(x_vmem, o_hbm.at[i_vmem.at[0]])

        pltpu.emit_pipeline(
            body,
            grid=(n // SC_WINDOW,),
            in_specs=[pl.BlockSpec((SC_WINDOW, SC_ROW), lambda i: (i // (reps * nbt) * nbt + i % nbt, 0)),
                      pl.BlockSpec((1, SC_WINDOW), lambda i: (0, i))],
            out_specs=[],
            core_axis_name=("c", "s"),
            dimension_semantics=(pltpu.PARALLEL,),
        )(x_hbm, i_hbm)

    return k(x, idx.reshape(1, n))


def _sc_gather(table, idx):
    n = idx.shape[0]

    @functools.partial(pl.kernel, out_type=jax.ShapeDtypeStruct((n, SC_ROW), table.dtype), mesh=_sc_mesh())
    def k(x_hbm, i_hbm, o_hbm):
        def body(i_vmem, o_vmem):
            pltpu.sync_copy(x_hbm.at[i_vmem.at[0]], o_vmem)

        pltpu.emit_pipeline(
            body,
            grid=(n // SC_WINDOW,),
            in_specs=[pl.BlockSpec((1, SC_WINDOW), lambda i: (0, i))],
            out_specs=[pl.BlockSpec((SC_WINDOW, SC_ROW), lambda i: (i, 0))],
            core_axis_name=("c", "s"),
            dimension_semantics=(pltpu.PARALLEL,),
        )(i_hbm, o_hbm)

    return k(table, idx.reshape(1, n))


MOE_TM = 256


def _expert_kernel(te_ref, tr_ref, x_ref, wgu_ref, wd_ref, y_ref, wgu_bf, wd_bf):
    i = pl.program_id(0)
    valid = tr_ref[i]

    @pl.when(jnp.logical_or(i == 0, te_ref[i] != te_ref[jnp.maximum(i - 1, 0)]))
    def _():
        wgu_bf[...] = wgu_ref[0].astype(BF16)
        wd_bf[...] = wd_ref[0].astype(BF16)

    @pl.when(valid > 0)
    def _():
        live = lax.broadcasted_iota(jnp.int32, x_ref.shape[1:], 0) < valid
        x = _unpack_parts([jnp.where(live, x_ref[p], 0) for p in range(ROW_PARTS)])
        gu = _dot(x, wgu_bf[...])
        act = _silu(gu[:, :EXPERT_FF]) * gu[:, EXPERT_FF:]
        for p, part in enumerate(_pack_parts(_dot(act.astype(BF16), wd_bf[...]))):
            y_ref[p] = part

    @pl.when(valid == 0)
    def _():
        y_ref[...] = jnp.zeros(y_ref.shape, y_ref.dtype)


def _experts(tile_expert, tile_rows, xs, wgu, wd):
    _, n_rows, _ = xs.shape
    d = wgu.shape[1]
    rows = pl.BlockSpec((ROW_PARTS, MOE_TM, SC_ROW), lambda i, te, tr: (0, i, 0))
    return pl.pallas_call(
        _expert_kernel,
        grid_spec=pltpu.PrefetchScalarGridSpec(
            num_scalar_prefetch=2,
            grid=(n_rows // MOE_TM,),
            in_specs=[rows,
                      pl.BlockSpec((1, d, 2 * EXPERT_FF), lambda i, te, tr: (te[i], 0, 0)),
                      pl.BlockSpec((1, EXPERT_FF, d), lambda i, te, tr: (te[i], 0, 0))],
            out_specs=rows,
            scratch_shapes=[pltpu.VMEM((d, 2 * EXPERT_FF), BF16), pltpu.VMEM((EXPERT_FF, d), BF16)]),
        out_shape=jax.ShapeDtypeStruct(xs.shape, jnp.int32),
        compiler_params=_cp(("arbitrary",)),
        name="moe_experts",
    )(tile_expert, tile_rows, xs, wgu, wd)


def _combine_kernel(g_ref, w_ref, h_ref, x_ref, m5_ref, wsgu_ref, wsd_ref, lg_ref, lb_ref, o_ref):
    h = _unpack_parts([h_ref[p, 0] for p in range(ROW_PARTS)])
    gu = _dot(h, wsgu_ref[...])
    act = _silu(gu[:, :SHARED_FF]) * gu[:, SHARED_FF:]
    f = _dot(act.astype(BF16), wsd_ref[...])
    w = w_ref[...]
    for k in range(TOP_K):
        f = f + w[:, k:k + 1] * _unpack_parts([g_ref[p, k] for p in range(ROW_PARTS)]).astype(F32)
    o_ref[0] = _layer_norm(DEEPNORM_ALPHA * x_ref[0] + m5_ref[0] * f, lg_ref[...], lb_ref[...])


def _combine(g, w_k, h2p, x1, mod_l, wsgu, wsd, ln_g, ln_b):
    b, lt, d = x1.shape
    n = lt // TOK_TILE
    tok = pl.BlockSpec((1, TOK_TILE, d), lambda i, t: (i, t, 0))
    full = lambda a: pl.BlockSpec(a.shape, lambda i, t: (0,) * a.ndim)
    return pl.pallas_call(
        _combine_kernel,
        grid=(b, n),
        in_specs=[pl.BlockSpec((ROW_PARTS, TOP_K, TOK_TILE, SC_ROW), lambda i, t: (0, 0, i * n + t, 0)),
                  pl.BlockSpec((TOK_TILE, TOP_K), lambda i, t: (i * n + t, 0)),
                  pl.BlockSpec((ROW_PARTS, 1, TOK_TILE, SC_ROW), lambda i, t: (0, i, t, 0)),
                  tok, _mod_spec(b, 5), full(wsgu), full(wsd), full(ln_g), full(ln_b)],
        out_specs=tok,
        out_shape=jax.ShapeDtypeStruct((b, lt, d), F32),
        compiler_params=_cp(("arbitrary", "arbitrary")),
        name="moe_combine",
    )(g, w_k, h2p, x1, mod_l, wsgu, wsd, ln_g, ln_b)


def _moe_sparse(h2p, logits_t, bias, wgu, wd, wsgu, wsd, x1, mod_l, ln_g, ln_b):
    n_tok = h2p.shape[1] * h2p.shape[2]
    e_k, r_k, w_k, counts = _route(logits_t, bias)
    counts = counts[:, 0].astype(jnp.int32)
    padded = (counts + MOE_TM - 1) // MOE_TM * MOE_TM
    ends = jnp.cumsum(padded)
    starts = ends - padded
    n_rows = (n_tok * TOP_K + N_EXPERTS * (MOE_TM - 1)) // MOE_TM * MOE_TM
    tile0 = jnp.arange(n_rows // MOE_TM, dtype=jnp.int32) * MOE_TM
    tile_expert = jnp.minimum(jnp.sum(tile0[:, None] >= ends[None, :], axis=1), N_EXPERTS - 1).astype(jnp.int32)
    tile_rows = jnp.clip(counts[tile_expert] - (tile0 - starts[tile_expert]), 0, MOE_TM).astype(jnp.int32)
    idx = _slots(starts.astype(jnp.int32), e_k, r_k, n_rows).reshape(-1)
    xs = _sc_scatter(h2p.reshape(ROW_PARTS * n_tok, SC_ROW), idx, ROW_PARTS * n_rows)
    ys = _experts(tile_expert, tile_rows, xs.reshape(ROW_PARTS, n_rows, SC_ROW), wgu, wd)
    g = _sc_gather(ys.reshape(ROW_PARTS * n_rows, SC_ROW), idx).reshape(ROW_PARTS, TOP_K, n_tok, SC_ROW)
    return _combine(g, w_k, h2p, x1, mod_l, wsgu, wsd, ln_g, ln_b)


def _rope_tables(n_ctx, n_lat):
    rows = n_lat // GRID_W
    row = jnp.broadcast_to(jnp.arange(rows, dtype=F32)[:, None], (rows, GRID_W)).reshape(-1)
    col = jnp.broadcast_to(jnp.arange(GRID_W, dtype=F32)[None, :], (rows, GRID_W)).reshape(-1)
    inv = ROPE_BASE ** (-jnp.arange(ROPE_FREQS, dtype=F32) / ROPE_FREQS)
    ang = jnp.stack([row[:, None] * inv, col[:, None] * inv], axis=1)
    cos = jnp.cos(ang)[:, :, None, :]
    sin = jnp.sin(ang)[:, :, None, :]
    cos32 = jnp.broadcast_to(cos, (n_lat, 2, 2, ROPE_FREQS)).reshape(n_lat, MLA_ROPE)
    sin32 = jnp.concatenate([-sin, sin], axis=2).reshape(n_lat, MLA_ROPE)
    cos32 = jnp.concatenate([jnp.ones((n_ctx, MLA_ROPE), F32), cos32], axis=0)
    sin32 = jnp.concatenate([jnp.zeros((n_ctx, MLA_ROPE), F32), sin32], axis=0)
    lt = n_ctx + n_lat
    ca = jnp.concatenate([jnp.ones((lt, MLA_NOPE), F32), cos32, jnp.zeros((lt, 32), F32)], axis=1)
    sb = jnp.concatenate([jnp.zeros((lt, MLA_NOPE), F32), sin32, jnp.zeros((lt, 32), F32)], axis=1)
    return ca, sb


def _rope_partner():
    idx = np.arange(MLA_ROPE).reshape(2, 2, ROPE_FREQS)
    return idx[:, ::-1, :].reshape(-1)


def _pack_layer(w_in, w_q_b, w_kv_b):
    sizes = (MLA_Q_LORA, MLA_KV_LORA + MLA_ROPE, REC_W, REC_W, REC_W, REC_W, REC_W,
             3 * REC_W, REC_W, 2 * REC_HEADS, 2 * REC_HEADS, N_BRANCH * D_MODEL)
    offs = np.cumsum((0,) + sizes)
    seg = lambda i: w_in[:, offs[i]:offs[i + 1]]
    d = w_in.shape[0]
    zeros = lambda n: jnp.zeros((d, n), w_in.dtype)
    partner = _rope_partner()
    kva = seg(1)
    k_rope = kva[:, MLA_KV_LORA:]
    w_main = jnp.concatenate([seg(11), seg(2), seg(3), seg(6), seg(8)], axis=1)
    w_f = jnp.concatenate([seg(4), seg(5)], axis=1)
    w_mla = jnp.concatenate([seg(0), kva[:, :MLA_KV_LORA],
                             zeros(MLA_NOPE), k_rope, zeros(32),
                             zeros(MLA_NOPE), k_rope[:, partner], zeros(32)], axis=1)
    w_gdn = jnp.concatenate([seg(7), seg(9), seg(10), zeros(256 - 4 * REC_HEADS)], axis=1)
    r = w_q_b.shape[0]
    qb = w_q_b.reshape(r, MLA_HEADS, MLA_NOPE + MLA_ROPE)
    zq = lambda n: jnp.zeros((r, MLA_HEADS, n), w_q_b.dtype)
    wq1 = jnp.concatenate([qb, zq(32)], axis=2).reshape(r, MLA_HEADS * 128)
    wq2 = jnp.concatenate([zq(MLA_NOPE), qb[:, :, MLA_NOPE:][:, :, partner], zq(32)], axis=2).reshape(r, MLA_HEADS * 128)
    rk = w_kv_b.shape[0]
    kvb = w_kv_b.reshape(rk, MLA_HEADS, MLA_NOPE + MLA_V)
    wk = jnp.concatenate([kvb[:, :, :MLA_NOPE], jnp.zeros((rk, MLA_HEADS, 64), w_kv_b.dtype)], axis=2)
    wk = wk.reshape(rk, MLA_HEADS * 128)
    wv = kvb[:, :, MLA_NOPE:].reshape(rk, MLA_HEADS * MLA_V)
    bf = lambda a: a.astype(BF16)
    return bf(w_main), bf(w_f), bf(w_mla), bf(w_gdn), bf(wq1), bf(wq2), bf(wk), bf(wv)


def kernel(x, c, ctx, c_ctx, w_mod, b_mod, w_in, q_a_norm, w_q_b, kv_a_norm, w_kv_b, hg_lb_logits, hg_norm,
           gdn_conv, gdn_a_log, gdn_dt_bias, gdn_norm, w_branch, w_out, ln1_g, ln1_b, ln2_g, ln2_b,
           w_router, router_bias, w_gu, w_down, w_sh_gu, w_sh_down):
    batch, n_lat, d = x.shape
    n_ctx = ctx.shape[1]
    assert n_ctx == TOK_TILE and n_lat % TOK_TILE == 0 and batch < MOD_ROWS and d == D_MODEL
    lt = n_ctx + n_lat

    c_all = jnp.zeros((MOD_ROWS, d), F32).at[:batch].set(c).at[batch].set(c_ctx)
    mod = _mod_all(c_all, w_mod, b_mod).reshape(DEPTH, MOD_ROWS, 1, 6 * d)
    ca, sb = _rope_tables(n_ctx, n_lat)
    lb_soft = jax.nn.softmax(hg_lb_logits.astype(F32), axis=0)
    lower = (jnp.cumsum(lb_soft, axis=0) - lb_soft[0]).reshape(DEPTH, 2, 1, REC_W)
    row = lambda a: a.reshape(1, -1)

    x_all = jnp.concatenate([ctx, x], axis=1)
    for l in range(DEPTH):
        w_main, w_f, w_mla, w_gdn, wq1, wq2, wk, wv = _pack_layer(w_in[l], w_q_b[l], w_kv_b[l])
        mod_l = mod[l]
        h = _modulate(x_all, mod_l)
        z_main = _proj(h, w_main, REC_W, BF16)
        z_f = _proj(h, w_f, REC_W)
        z_mla = _proj(h, w_mla, ZA_WIDTH)
        z_gdn = _proj(h, w_gdn, ZG_WIDTH // 2)

        q, k, v = _mla_prep(z_mla, row(q_a_norm[l]), row(kv_a_norm[l]), wq1, wq2, wk, wv, ca, sb)
        mla_o = _attention(q, k, v, n_ctx)

        hg_f, hg_b = _hgrn2(z_main, z_f, lower[l])

        qkv = _gdn_prep(z_gdn, gdn_conv[l], n_ctx)
        par = jnp.zeros((8, 128), F32)
        par = par.at[0, :2 * REC_HEADS].set(gdn_a_log[l].reshape(-1))
        par = par.at[1, :2 * REC_HEADS].set(gdn_dt_bias[l].reshape(-1))
        gd_f, gd_b = _gdn(qkv, z_gdn, par)

        x1, h2, logits_t = _merge(mla_o, hg_f, hg_b, gd_f, gd_b, z_main, x_all, mod_l,
                                  row(hg_norm[l]), row(gdn_norm[l]), w_branch[l].astype(BF16),
                                  w_out[l].astype(BF16), row(ln1_g[l]), row(ln1_b[l]), w_router[l].T)
        x_all = _moe_sparse(h2, logits_t,
                            router_bias[l].reshape(N_GROUPS, N_EXPERTS // N_GROUPS, 1),
                            w_gu[l], w_down[l],
                            w_sh_gu[l].astype(BF16), w_sh_down[l].astype(BF16),
                            x1, mod_l, row(ln2_g[l]), row(ln2_b[l]))
    return x_all[:, n_ctx:, :]
```

```python
import functools
import math

import numpy as np
import jax
import jax.numpy as jnp
from jax import lax
from jax.experimental import pallas as pl
from jax.experimental.pallas import tpu as pltpu
from jax.experimental.pallas import tpu_sc as plsc

F32 = jnp.float32
BF16 = jnp.bfloat16
HIGHEST = lax.Precision.HIGHEST

D_MODEL = 1024
DEPTH = 4
GRID_W = 64
MLA_HEADS = 8
MLA_Q_LORA = 384
MLA_KV_LORA = 256
MLA_NOPE = 64
MLA_ROPE = 32
MLA_V = 64
MLA_SCALE = (MLA_NOPE + MLA_ROPE) ** -0.5
ROPE_BASE = 10000.0
ROPE_FREQS = MLA_ROPE // 4
REC_HEADS = 4
REC_D = 128
REC_W = REC_HEADS * REC_D
CONV_K = 5
N_BRANCH = 3
BRANCH_W = 512
N_EXPERTS = 64
TOP_K = 8
N_GROUPS = 8
TOPK_GROUPS = 4
EXPERT_FF = 256
SHARED_FF = 256
ROUTED_SCALE = 2.5
DEEPNORM_ALPHA = (2 * DEPTH) ** 0.25
LN_EPS = 1e-6
RMS_EPS = 1e-6

TOK_TILE = 256
CHUNK = 64
MOD_ROWS = 16
VMEM_LIMIT = 56 * 1024 * 1024

ZM_Q, ZM_I, ZM_HGATE, ZM_GGATE = (3 * D_MODEL // REC_W + k for k in range(4))
ZA_WIDTH = MLA_Q_LORA + MLA_KV_LORA + 256
ZG_CONV = 3 * REC_W
ZG_WIDTH = ZG_CONV + 256


def _cp(sem, vmem=VMEM_LIMIT):
    return pltpu.CompilerParams(dimension_semantics=sem, vmem_limit_bytes=vmem)


def _dot(a, b, precision=None):
    return jnp.dot(a, b, preferred_element_type=F32, precision=precision)


def _dot_nt(a, b, precision=None):
    return lax.dot_general(a, b, (((1,), (1,)), ((), ())), preferred_element_type=F32, precision=precision)


def _sigmoid(x):
    return 1.0 / (1.0 + jnp.exp(-x))


def _silu(x):
    return x * _sigmoid(x)


ROW_PARTS = 2
SC_ROW = D_MODEL // (2 * ROW_PARTS)
SC_WINDOW = 128


def _pack_parts(x):
    q = x.shape[1] // (2 * ROW_PARTS)
    bits = lambda a: lax.bitcast_convert_type(a.astype(BF16).astype(F32), jnp.uint32)
    parts = []
    for p in range(ROW_PARTS):
        hi = bits(x[:, p * q:(p + 1) * q])
        lo = bits(x[:, (ROW_PARTS + p) * q:(ROW_PARTS + p + 1) * q])
        parts.append(lax.bitcast_convert_type(hi | lax.shift_right_logical(lo, jnp.uint32(16)), jnp.int32))
    return parts


def _unpack_parts(parts):
    his, los = [], []
    for p in parts:
        u = lax.bitcast_convert_type(p, jnp.uint32)
        his.append(lax.bitcast_convert_type(u & jnp.uint32(0xFFFF0000), F32).astype(BF16))
        los.append(lax.bitcast_convert_type(lax.shift_left(u, jnp.uint32(16)), F32).astype(BF16))
    return jnp.concatenate(his + los, axis=1)


def _mod_kernel(c_ref, w_ref, b_ref, o_ref):
    s = _silu(c_ref[...])
    o_ref[0] = _dot(s, w_ref[0], HIGHEST) + b_ref[0]


def _mod_all(c_all, w_mod, b_mod):
    tn = 1024
    n = w_mod.shape[-1]
    return pl.pallas_call(
        _mod_kernel,
        grid=(DEPTH, n // tn),
        in_specs=[
            pl.BlockSpec((MOD_ROWS, D_MODEL), lambda l, j: (0, 0)),
            pl.BlockSpec((1, D_MODEL, tn), lambda l, j: (l, 0, j)),
            pl.BlockSpec((1, 1, tn), lambda l, j: (l, 0, j)),
        ],
        out_specs=pl.BlockSpec((1, MOD_ROWS, tn), lambda l, j: (l, 0, j)),
        out_shape=jax.ShapeDtypeStruct((DEPTH, MOD_ROWS, n), F32),
        compiler_params=_cp(("arbitrary", "arbitrary")),
        name="mod_all",
    )(c_all, w_mod, b_mod.reshape(DEPTH, 1, n))


def _mod_spec(batch, k):
    return pl.BlockSpec((1, 1, D_MODEL), lambda b, t: (jnp.where(t == 0, batch, b), 0, k))


def _modulate_kernel(x_ref, sh_ref, sc_ref, o_ref):
    o_ref[0] = (x_ref[0] * (1.0 + sc_ref[0]) + sh_ref[0]).astype(o_ref.dtype)


def _modulate(x_all, mod_l):
    b, lt, d = x_all.shape
    return pl.pallas_call(
        _modulate_kernel,
        grid=(b, lt // TOK_TILE),
        in_specs=[
            pl.BlockSpec((1, TOK_TILE, d), lambda i, t: (i, t, 0)),
            _mod_spec(b, 0),
            _mod_spec(b, 1),
        ],
        out_specs=pl.BlockSpec((1, TOK_TILE, d), lambda i, t: (i, t, 0)),
        out_shape=jax.ShapeDtypeStruct((b, lt, d), BF16),
        compiler_params=_cp(("arbitrary", "arbitrary")),
        name="modulate",
    )(x_all, mod_l, mod_l)


PROJ_ROWS = 256


def _proj_kernel(h_ref, w_ref, o_ref, *, rows):
    def body(r, carry):
        sl = pl.ds(pl.multiple_of(r * rows, rows), rows)
        o_ref[0, sl, :] = _dot(h_ref[0, sl, :], w_ref[...]).astype(o_ref.dtype)
        return carry

    lax.fori_loop(0, h_ref.shape[1] // rows, body, 0)


def _proj(h, w, tn, dtype=F32):
    b, lt, d = h.shape
    n = w.shape[1]
    return pl.pallas_call(
        functools.partial(_proj_kernel, rows=PROJ_ROWS),
        grid=(b, n // tn),
        in_specs=[
            pl.BlockSpec((1, lt, d), lambda i, j: (i, 0, 0)),
            pl.BlockSpec((d, tn), lambda i, j: (0, j)),
        ],
        out_specs=pl.BlockSpec((1, lt, tn), lambda i, j: (i, 0, j)),
        out_shape=jax.ShapeDtypeStruct((b, lt, n), dtype),
        compiler_params=_cp(("arbitrary", "arbitrary")),
        name="proj",
    )(h, w)


def _rms(x, g):
    return x * lax.rsqrt(jnp.mean(x * x, axis=-1, keepdims=True) + RMS_EPS) * g


def _mla_prep_kernel(z_ref, qg_ref, kg_ref, wq1_ref, wq2_ref, wk_ref, wv_ref, ca_ref, sb_ref,
                     q_ref, k_ref, v_ref):
    z = z_ref[0]
    ca = ca_ref[...]
    sb = sb_ref[...]
    qn = _rms(z[:, :MLA_Q_LORA], qg_ref[...]).astype(BF16)
    qa = _dot(qn, wq1_ref[...])
    qb = _dot(qn, wq2_ref[...])
    kvn = _rms(z[:, MLA_Q_LORA:MLA_Q_LORA + MLA_KV_LORA], kg_ref[...]).astype(BF16)
    kn = _dot(kvn, wk_ref[...])
    v = _dot(kvn, wv_ref[...])
    lane = lax.broadcasted_iota(jnp.int32, v.shape, 1)
    v_ref[0] = jnp.where((lane & 127) == MLA_V, 1.0, v).astype(v_ref.dtype)
    o = MLA_Q_LORA + MLA_KV_LORA
    kr = z[:, o:o + 128] * ca + z[:, o + 128:o + 256] * sb
    for h in range(MLA_HEADS):
        sl = slice(h * 128, (h + 1) * 128)
        q_ref[0, :, sl] = ((qa[:, sl] * ca + qb[:, sl] * sb) * MLA_SCALE).astype(q_ref.dtype)
        k_ref[0, :, sl] = (kn[:, sl] + kr).astype(k_ref.dtype)


def _mla_prep(z_mla, qg, kg, wq1, wq2, wk, wv, ca, sb):
    b, lt, _ = z_mla.shape
    full = lambda a: pl.BlockSpec(a.shape, lambda i, t: (0,) * a.ndim)
    tok = lambda w: pl.BlockSpec((1, TOK_TILE, w), lambda i, t: (i, t, 0))
    return pl.pallas_call(
        _mla_prep_kernel,
        grid=(b, lt // TOK_TILE),
        in_specs=[tok(ZA_WIDTH), full(qg), full(kg), full(wq1), full(wq2), full(wk), full(wv),
                  pl.BlockSpec((TOK_TILE, 128), lambda i, t: (t, 0)),
                  pl.BlockSpec((TOK_TILE, 128), lambda i, t: (t, 0))],
        out_specs=[tok(MLA_HEADS * 128)] * 3,
        out_shape=[jax.ShapeDtypeStruct((b, lt, MLA_HEADS * 128), BF16)] * 3,
        compiler_params=_cp(("arbitrary", "arbitrary")),
        name="mla_prep",
    )(z_mla, qg, kg, wq1, wq2, wk, wv, ca, sb)


def _attn_kernel(q_ref, k_ref, v_ref, o_ref, *, n_ctx):
    def attend(nk):
        outs = []
        for h in range(2):
            q = q_ref[0, :, h * 128:(h + 1) * 128]
            k = k_ref[0, :nk, h * 128:(h + 1) * 128]
            s = _dot_nt(q, k)
            p = jnp.exp(s - jnp.max(s, axis=1, keepdims=True))
            o = _dot(p.astype(BF16), v_ref[0, :nk, h * 128:(h + 1) * 128])
            outs.append(o[:, :MLA_V] / o[:, MLA_V:MLA_V + 1])
        o_ref[0] = jnp.concatenate(outs, axis=1).astype(o_ref.dtype)

    @pl.when(pl.program_id(2) == 0)
    def _():
        attend(n_ctx)

    @pl.when(pl.program_id(2) > 0)
    def _():
        attend(k_ref.shape[1])


def _attention(q, k, v, n_ctx):
    b, lt, _ = q.shape
    return pl.pallas_call(
        functools.partial(_attn_kernel, n_ctx=n_ctx),
        grid=(b, MLA_HEADS // 2, lt // TOK_TILE),
        in_specs=[
            pl.BlockSpec((1, TOK_TILE, 256), lambda i, h, t: (i, t, h)),
            pl.BlockSpec((1, lt, 256), lambda i, h, t: (i, 0, h)),
            pl.BlockSpec((1, lt, 256), lambda i, h, t: (i, 0, h)),
        ],
        out_specs=pl.BlockSpec((1, TOK_TILE, 2 * MLA_V), lambda i, h, t: (i, t, h)),
        out_shape=jax.ShapeDtypeStruct((b, lt, MLA_HEADS * MLA_V), BF16),
        compiler_params=_cp(("arbitrary", "arbitrary", "arbitrary")),
        name="mla_attention",
    )(q, k, v)


def _rev_tile(s, n):
    return jnp.where(s == 0, 0, n - s)


def _time_index(shape, dim, reverse):
    i = lax.broadcasted_iota(jnp.int32, shape, dim)
    return (shape[dim] - 1 - i) if reverse else i


def _blk(i, size):
    return lax.shift_right_logical(i, jnp.int32(int(math.log2(size))))


def _gla_chunks(items):
    c_len = items[0]["q"].shape[0]
    for it in items:
        it["c"] = it["g"]
        it["t"] = it["g"]
        it["a"] = jnp.zeros((c_len, c_len), F32)
    hb = 1
    while hb < c_len:
        masks = {}
        for rev in {it["reverse"] for it in items}:
            ri = _time_index((c_len, c_len), 0, rev)
            ci = _time_index((c_len, c_len), 1, rev)
            pair = jnp.logical_and(_blk(ri, hb) == _blk(ci, hb) + 1, (_blk(ri, hb) & 1) == 1)
            odd = (_blk(_time_index((c_len, REC_D), 0, rev), hb) & 1) == 1
            masks[rev] = (pair, odd)
        for it in items:
            rev = it["reverse"]
            pair, odd = masks[rev]
            c, t = it["c"], it["t"]
            qt = (it["q"] * jnp.exp(c)).astype(BF16)
            kt = (it["k"] * jnp.exp(t - c)).astype(BF16)
            it["a"] = it["a"] + jnp.where(pair, _dot_nt(qt, kt), 0.0)
            t_lo = pltpu.roll(t, hb, 0)
            t_hi = pltpu.roll(t, c_len - hb, 0)
            prev, nxt = (t_hi, t_lo) if rev else (t_lo, t_hi)
            it["c"] = c + jnp.where(odd, prev, 0.0)
            it["t"] = t + jnp.where(odd, prev, nxt)
        hb *= 2
    outs = []
    for it in items:
        q, k, v, c, t = it["q"], it["k"], it["v"], it["c"], it["t"]
        st = it["st_ref"][...]
        qk = jnp.sum(q * k, axis=1, keepdims=True)
        outs.append(_dot_nt((q * jnp.exp(c)).astype(BF16), st.astype(BF16))
                    + _dot(it["a"].astype(BF16), v.astype(BF16)) + qk * v)
        kd = (k * jnp.exp(t - c)).astype(BF16)
        it["st_ref"][...] = st * jnp.exp(t[0:1, :]) + _dot(v.T.astype(BF16), kd)
    return outs


def _hgrn2_kernel(qf_ref, vf_ref, ff_ref, qb_ref, vb_ref, fb_ref, lb_ref, of_ref, ob_ref, st_ref):
    @pl.when(pl.program_id(1) == 0)
    def _():
        st_ref[...] = jnp.zeros(st_ref.shape, F32)

    n_chunks = TOK_TILE // CHUNK

    def body(ci, carry):
        items, dests = [], []
        for d, (q_ref, v_ref, f_ref, o_ref) in enumerate(((qf_ref, vf_ref, ff_ref, of_ref),
                                                           (qb_ref, vb_ref, fb_ref, ob_ref))):
            cc = ci if d == 0 else n_chunks - 1 - ci
            rows = pl.ds(pl.multiple_of(cc * CHUNK, CHUNK), CHUNK)
            for h in range(REC_HEADS):
                cols = slice(h * REC_D, (h + 1) * REC_D)
                lb = lb_ref[d, :, cols]
                f = lb + (1.0 - lb) * _sigmoid(f_ref[0, rows, cols])
                items.append(dict(q=_silu(q_ref[0, rows, cols].astype(F32)), k=1.0 - f,
                                  v=v_ref[0, rows, cols].astype(F32),
                                  g=jnp.log(f), st_ref=st_ref.at[d, h], reverse=(d == 1)))
                dests.append((o_ref, rows, cols))
        for (o_ref, rows, cols), o in zip(dests, _gla_chunks(items)):
            o_ref[0, rows, cols] = o
        return carry

    lax.fori_loop(0, n_chunks, body, 0)


def _hgrn2(z_main, z_f, lb):
    b, lt, _ = z_main.shape
    n = lt // TOK_TILE
    fwd = lambda k: pl.BlockSpec((1, TOK_TILE, REC_W), lambda i, s: (i, s, k))
    bwd = lambda k: pl.BlockSpec((1, TOK_TILE, REC_W), lambda i, s: (i, _rev_tile(s, n), k))
    return pl.pallas_call(
        _hgrn2_kernel,
        grid=(b, n),
        in_specs=[fwd(ZM_Q), fwd(ZM_I), fwd(0), bwd(ZM_Q), bwd(ZM_I), bwd(1),
                  pl.BlockSpec((2, 1, REC_W), lambda i, s: (0, 0, 0))],
        out_specs=[pl.BlockSpec((1, TOK_TILE, REC_W), lambda i, s: (i, s, 0)),
                   pl.BlockSpec((1, TOK_TILE, REC_W), lambda i, s: (i, _rev_tile(s, n), 0))],
        out_shape=[jax.ShapeDtypeStruct((b, lt, REC_W), F32)] * 2,
        scratch_shapes=[pltpu.VMEM((2, REC_HEADS, REC_D, REC_D), F32)],
        compiler_params=_cp(("arbitrary", "arbitrary")),
        name="hgrn2_scan",
    )(z_main, z_main, z_f, z_main, z_main, z_f, lb)


def _gdn_prep_kernel(x_ref, w_ref, o_ref, *, n_ctx):
    x = x_ref[0]
    lt = x.shape[0]
    t = lax.broadcasted_iota(jnp.int32, x.shape, 0)
    lo = jnp.where(t < n_ctx, 0, n_ctx)
    hi = jnp.where(t < n_ctx, n_ctx, lt)
    acc = x * w_ref[CONV_K // 2:CONV_K // 2 + 1, :]
    for kk in range(CONV_K):
        off = kk - CONV_K // 2
        if off == 0:
            continue
        xs = pltpu.roll(x, (-off) % lt, 0)
        ok = jnp.logical_and(t + off >= lo, t + off < hi)
        acc = acc + jnp.where(ok, xs, 0.0) * w_ref[kk:kk + 1, :]
    y = _silu(acc)
    inv = lax.rsqrt(jnp.sum(y * y, axis=1, keepdims=True) + 1e-6)
    is_qk = pl.program_id(1) < 2 * REC_HEADS
    o_ref[0] = y * jnp.where(is_qk, inv, 1.0)


def _gdn_prep(z_gdn, conv_w, n_ctx):
    b, lt, _ = z_gdn.shape
    return pl.pallas_call(
        functools.partial(_gdn_prep_kernel, n_ctx=n_ctx),
        grid=(b, ZG_CONV // REC_D),
        in_specs=[pl.BlockSpec((1, lt, REC_D), lambda i, j: (i, 0, j)),
                  pl.BlockSpec((CONV_K, REC_D), lambda i, j: (0, j))],
        out_specs=pl.BlockSpec((1, lt, REC_D), lambda i, j: (i, 0, j)),
        out_shape=jax.ShapeDtypeStruct((b, lt, ZG_CONV), F32),
        compiler_params=_cp(("arbitrary", "arbitrary")),
        name="gdn_prep",
    )(z_gdn, conv_w)


def _gdn_tile(items):
    t_len = items[0]["q"].shape[0]
    n_chunks = t_len // CHUNK
    for it in items:
        ri = _time_index((t_len, t_len), 0, it["reverse"])
        ci = _time_index((t_len, t_len), 1, it["reverse"])
        it["ri"], it["ci"] = ri, ci
        same = _blk(ri, CHUNK) == _blk(ci, CHUNK)
        it["dec"] = jnp.where(jnp.logical_and(same, ri >= ci),
                              jnp.exp(jnp.minimum(it["gcol"] - it["grow"], 0.0)), 0.0)
        it["kb"] = it["k"] * it["beta"]
        it["kbf"] = it["k"].astype(BF16)
    for it in items:
        ri, ci = it["ri"], it["ci"]
        lm = jnp.where(ri > ci, _dot_nt(it["kb"].astype(BF16), it["kbf"]) * it["dec"], 0.0)
        it["lm"] = lm
        it["tinv"] = jnp.where(ri == ci, 1.0, 0.0) - jnp.where(_blk(ri, 2) == _blk(ci, 2), lm, 0.0)
    hb = 2
    while hb < CHUNK:
        for it in items:
            ri, ci = it["ri"], it["ci"]
            off = jnp.where(_blk(ri, 2 * hb) == _blk(ci, 2 * hb),
                            jnp.where(_blk(ri, hb) == _blk(ci, hb), 0.0, it["lm"]), 0.0)
            it["tb"] = it["tinv"].astype(BF16)
            it["to"] = _dot(it["tb"], off.astype(BF16)).astype(BF16)
        for it in items:
            it["tinv"] = it["tinv"] - _dot(it["to"], it["tb"])
        hb *= 2
    for it in items:
        eg = jnp.exp(it["gcol"])
        tb = it["tinv"].astype(BF16)
        it["u"] = _dot(tb, (it["v"] * it["beta"]).astype(BF16))
        it["w"] = _dot(tb, (it["kb"] * eg).astype(BF16)).astype(BF16)
        qs = it["q"] * (REC_D ** -0.5)
        it["aqk"] = (_dot_nt(qs.astype(BF16), it["kbf"]) * it["dec"]).astype(BF16)
        it["qd"] = (qs * eg).astype(BF16)
        kd = it["k"] * jnp.exp(it["gtot"] - it["gcol"])
        it["kdt"] = [kd[c * CHUNK:(c + 1) * CHUNK].T.astype(BF16) for c in range(n_chunks)]
        it["s"] = it["s_ref"][...]
        it["o"], it["vn"] = [None] * n_chunks, [None] * n_chunks
    for step in range(n_chunks):
        for it in items:
            c = n_chunks - 1 - step if it["reverse"] else step
            it["c"] = c
            rows = slice(c * CHUNK, (c + 1) * CHUNK)
            it["sb"] = it["s"].astype(BF16)
            it["vn"][c] = (it["u"][rows] - _dot(it["w"][rows], it["sb"])).astype(BF16)
        for it in items:
            c = it["c"]
            rows = slice(c * CHUNK, (c + 1) * CHUNK)
            it["o"][c] = _dot(it["qd"][rows], it["sb"])
            it["s"] = (it["s"] * jnp.exp(it["gtot"][c * CHUNK:c * CHUNK + 1])
                       + _dot(it["kdt"][c], it["vn"][c]))
    outs = []
    for it in items:
        it["s_ref"][...] = it["s"]
        outs.append(jnp.concatenate(it["o"], axis=0) + _dot(it["aqk"], jnp.concatenate(it["vn"], axis=0)))
    return outs


def _gdn_kernel(qf_ref, kf_ref, vf_ref, gf_ref, qb_ref, kb_ref, vb_ref, gb_ref, par_ref, of_ref, ob_ref, s_ref):
    @pl.when(pl.program_id(1) == 0)
    def _():
        s_ref[...] = jnp.zeros(s_ref.shape, F32)

    neg_a = -jnp.exp(par_ref[0:1, :])
    dt_bias = par_ref[1:2, :]
    r2 = lax.broadcasted_iota(jnp.int32, (TOK_TILE, TOK_TILE), 0)
    c2 = lax.broadcasted_iota(jnp.int32, (TOK_TILE, TOK_TILE), 1)
    same = _blk(r2, CHUNK) == _blk(c2, CHUNK)
    items, dests = [], []
    for d, (q_ref, k_ref, v_ref, g_ref, o_ref) in enumerate(((qf_ref, kf_ref, vf_ref, gf_ref, of_ref),
                                                              (qb_ref, kb_ref, vb_ref, gb_ref, ob_ref))):
        ab = g_ref[0]
        xa = ab + dt_bias
        g = neg_a * (jnp.maximum(xa, 0.0) + jnp.log(1.0 + jnp.exp(-jnp.abs(xa))))
        tri = jnp.where(jnp.logical_and(same, (r2 <= c2) if d == 1 else (r2 >= c2)), 1.0, 0.0)
        gc = _dot(tri, g, HIGHEST)
        gtot = _dot(jnp.where(same, 1.0, 0.0), g, HIGHEST)
        gct = gc.T
        beta = _sigmoid(ab)
        for h in range(REC_HEADS):
            cols = slice(h * REC_D, (h + 1) * REC_D)
            ln = d * REC_HEADS + h
            items.append(dict(
                q=q_ref[0, :, cols], k=k_ref[0, :, cols], v=v_ref[0, :, cols],
                gcol=gc[:, ln:ln + 1], grow=gct[ln:ln + 1, :], gtot=gtot[:, ln:ln + 1],
                beta=beta[:, 2 * REC_HEADS + ln:2 * REC_HEADS + ln + 1], s_ref=s_ref.at[d, h],
                reverse=(d == 1)))
            dests.append((o_ref, cols))
    for (o_ref, cols), o in zip(dests, _gdn_tile(items)):
        o_ref[0, :, cols] = o


def _gdn(qkv, z_gdn, par):
    b, lt, _ = qkv.shape
    n = lt // TOK_TILE
    fwd = lambda k: pl.BlockSpec((1, TOK_TILE, REC_W), lambda i, s: (i, s, k))
    bwd = lambda k: pl.BlockSpec((1, TOK_TILE, REC_W), lambda i, s: (i, _rev_tile(s, n), k))
    gcol = ZG_CONV // 128
    return pl.pallas_call(
        _gdn_kernel,
        grid=(b, n),
        in_specs=[fwd(0), fwd(1), fwd(2), pl.BlockSpec((1, TOK_TILE, 128), lambda i, s: (i, s, gcol)),
                  bwd(0), bwd(1), bwd(2), pl.BlockSpec((1, TOK_TILE, 128), lambda i, s: (i, _rev_tile(s, n), gcol)),
                  pl.BlockSpec((8, 128), lambda i, s: (0, 0))],
        out_specs=[pl.BlockSpec((1, TOK_TILE, REC_W), lambda i, s: (i, s, 0)),
                   pl.BlockSpec((1, TOK_TILE, REC_W), lambda i, s: (i, _rev_tile(s, n), 0))],
        out_shape=[jax.ShapeDtypeStruct((b, lt, REC_W), F32)] * 2,
        scratch_shapes=[pltpu.VMEM((2, REC_HEADS, REC_D, REC_D), F32)],
        compiler_params=_cp(("arbitrary", "arbitrary")),
        name="gdn_scan",
    )(qkv, qkv, qkv, z_gdn, qkv, qkv, qkv, z_gdn, par)


def _layer_norm(x, g, b):
    mu = jnp.mean(x, axis=-1, keepdims=True)
    xc = x - mu
    var = jnp.mean(xc * xc, axis=-1, keepdims=True)
    return xc * lax.rsqrt(var + LN_EPS) * g + b


def _head_norm_gate(o, gate, w):
    outs = []
    for h in range(REC_HEADS):
        cols = slice(h * REC_D, (h + 1) * REC_D)
        oh = o[:, cols]
        n = oh * lax.rsqrt(jnp.mean(oh * oh, axis=-1, keepdims=True) + RMS_EPS) * w
        outs.append(n * _silu(gate[:, cols].astype(F32)))
    return jnp.concatenate(outs, axis=1)


def _merge_kernel(mla_ref, hf_ref, hb_ref, gf_ref, gb_ref, gates_ref, hgate_ref, ggate_ref, x_ref,
                  m2_ref, m3_ref, m4_ref, hw_ref, gw_ref, wb_ref, wo_ref, lg_ref, lbias_ref, wr_ref,
                  x1_ref, h2_ref, lt_ref):
    hg = _head_norm_gate(hf_ref[0] + hb_ref[0], hgate_ref[0], hw_ref[...]).astype(BF16)
    gd = _head_norm_gate(gf_ref[0] + gb_ref[0], ggate_ref[0], gw_ref[...]).astype(BF16)
    y = jnp.zeros((TOK_TILE, D_MODEL), F32)
    for n, o in enumerate((mla_ref[0], hg, gd)):
        y = y + _sigmoid(gates_ref[0, :, n * D_MODEL:(n + 1) * D_MODEL].astype(F32)) * _dot(o, wb_ref[n])
    y = _dot(y.astype(BF16), wo_ref[...])
    x1 = _layer_norm(DEEPNORM_ALPHA * x_ref[0] + m2_ref[0] * y, lg_ref[...], lbias_ref[...])
    x1_ref[0] = x1
    h2 = x1 * (1.0 + m4_ref[0]) + m3_ref[0]
    for p, part in enumerate(_pack_parts(h2)):
        h2_ref[p, 0] = part
    lt_ref[...] = _dot_nt(wr_ref[...], h2, HIGHEST)


def _merge(mla_o, hg_f, hg_b, gd_f, gd_b, z_main, x_all, mod_l, hg_w, gdn_w, wb, wo, ln_g, ln_b, wr_t):
    b, lt, d = x_all.shape
    n = lt // TOK_TILE
    tok = lambda w, k=0: pl.BlockSpec((1, TOK_TILE, w), lambda i, t: (i, t, k))
    full = lambda a: pl.BlockSpec(a.shape, lambda i, t: (0,) * a.ndim)
    return pl.pallas_call(
        _merge_kernel,
        grid=(b, n),
        in_specs=[tok(BRANCH_W), tok(REC_W), tok(REC_W), tok(REC_W), tok(REC_W),
                  tok(N_BRANCH * D_MODEL, 0), tok(REC_W, ZM_HGATE), tok(REC_W, ZM_GGATE),
                  tok(d), _mod_spec(b, 2), _mod_spec(b, 3), _mod_spec(b, 4),
                  full(hg_w), full(gdn_w), full(wb), full(wo), full(ln_g), full(ln_b), full(wr_t)],
        out_specs=[tok(d), pl.BlockSpec((ROW_PARTS, 1, TOK_TILE, SC_ROW), lambda i, t: (0, i, t, 0)),
                   pl.BlockSpec((N_EXPERTS, TOK_TILE), lambda i, t: (0, i * n + t))],
        out_shape=[jax.ShapeDtypeStruct((b, lt, d), F32),
                   jax.ShapeDtypeStruct((ROW_PARTS, b, lt, SC_ROW), jnp.int32),
                   jax.ShapeDtypeStruct((N_EXPERTS, b * lt), F32)],
        compiler_params=_cp(("arbitrary", "arbitrary")),
        name="merge",
    )(mla_o, hg_f, hg_b, gd_f, gd_b, z_main, z_main, z_main, x_all, mod_l, mod_l, mod_l,
      hg_w, gdn_w, wb, wo, ln_g, ln_b, wr_t)


def _first_max(x, idx, axes):
    m = x
    for ax in axes:
        m = jnp.max(m, axis=ax, keepdims=True)
    first = jnp.where(x == m, idx, jnp.int32(2 ** 30))
    for ax in axes:
        first = jnp.min(first, axis=ax, keepdims=True)
    return m, first


def _route_kernel(lt_ref, bias_ref, e_ref, r_ref, w_ref, cnt_ref, run_ref):
    @pl.when(pl.program_id(0) == 0)
    def _():
        run_ref[...] = jnp.zeros(run_ref.shape, F32)

    n_tok = lt_ref.shape[1]
    per = N_EXPERTS // N_GROUPS
    scores = _sigmoid(lt_ref[...]).reshape(N_GROUPS, per, n_tok)
    sel = scores + bias_ref[...]
    ig = lax.broadcasted_iota(jnp.int32, sel.shape, 0)
    ij = lax.broadcasted_iota(jnp.int32, sel.shape, 1)
    top1, a1 = _first_max(sel, ij, (1,))
    top2 = jnp.max(jnp.where(ij == a1, -jnp.inf, sel), axis=1, keepdims=True)
    grp = top1 + top2
    igg = lax.broadcasted_iota(jnp.int32, grp.shape, 0)
    gsel = jnp.zeros(grp.shape, F32)
    for _ in range(TOPK_GROUPS):
        _, a = _first_max(grp, igg, (0,))
        hit = igg == a
        gsel = jnp.where(hit, 1.0, gsel)
        grp = jnp.where(hit, -jnp.inf, grp)
    cur = jnp.where(gsel > 0.5, sel, -jnp.inf)
    ie = ig * per + ij
    esel = jnp.zeros(cur.shape, F32)
    for _ in range(TOP_K):
        _, a = _first_max(cur, ie, (1, 0))
        hit = ie == a
        esel = jnp.where(hit, 1.0, esel)
        cur = jnp.where(hit, -jnp.inf, cur)
    w = scores * esel
    tot = jnp.sum(jnp.sum(w, axis=1, keepdims=True), axis=0, keepdims=True)
    w = (w / tot * ROUTED_SCALE).reshape(N_EXPERTS, n_tok)
    m = esel.reshape(N_EXPERTS, n_tok)
    mb = m.astype(BF16)
    ti = lax.broadcasted_iota(jnp.int32, (n_tok, n_tok), 0)
    tj = lax.broadcasted_iota(jnp.int32, (n_tok, n_tok), 1)
    upto = _dot(mb, jnp.where(ti <= tj, 1.0, 0.0).astype(BF16))
    ei = lax.broadcasted_iota(jnp.int32, (N_EXPERTS, N_EXPERTS), 0)
    ej = lax.broadcasted_iota(jnp.int32, (N_EXPERTS, N_EXPERTS), 1)
    lower = _dot(jnp.where(ej < ei, 1.0, 0.0).astype(BF16), mb)
    run = run_ref[:, 0:1]
    rank = run + upto - 1.0
    run_ref[...] = jnp.broadcast_to(run + upto[:, n_tok - 1:n_tok], run_ref.shape)
    cnt_ref[...] = run_ref[...]
    eid = lax.broadcasted_iota(jnp.int32, m.shape, 0).astype(F32)
    rows_e, rows_r, rows_w = [], [], []
    for k in range(TOP_K):
        pick = jnp.where(lower == float(k), m, 0.0)
        rows_e.append(jnp.sum(pick * eid, axis=0, keepdims=True))
        rows_r.append(jnp.sum(pick * rank, axis=0, keepdims=True))
        rows_w.append(jnp.sum(pick * w, axis=0, keepdims=True))
    e_ref[...] = jnp.concatenate(rows_e, axis=0).astype(jnp.int32)
    r_ref[...] = jnp.concatenate(rows_r, axis=0).astype(jnp.int32)
    w_ref[...] = jnp.concatenate(rows_w, axis=0).T


ROUTE_TILE = 512


def _route(logits_t, bias):
    n_tok = logits_t.shape[1]
    tt = ROUTE_TILE
    return pl.pallas_call(
        _route_kernel,
        grid=(n_tok // tt,),
        in_specs=[pl.BlockSpec((N_EXPERTS, tt), lambda i: (0, i)),
                  pl.BlockSpec((N_GROUPS, N_EXPERTS // N_GROUPS, 1), lambda i: (0, 0, 0))],
        out_specs=[pl.BlockSpec((TOP_K, tt), lambda i: (0, i)),
                   pl.BlockSpec((TOP_K, tt), lambda i: (0, i)),
                   pl.BlockSpec((tt, TOP_K), lambda i: (i, 0)),
                   pl.BlockSpec((N_EXPERTS, 128), lambda i: (0, 0))],
        out_shape=[jax.ShapeDtypeStruct((TOP_K, n_tok), jnp.int32),
                   jax.ShapeDtypeStruct((TOP_K, n_tok), jnp.int32),
                   jax.ShapeDtypeStruct((n_tok, TOP_K), F32),
                   jax.ShapeDtypeStruct((N_EXPERTS, 128), F32)],
        scratch_shapes=[pltpu.VMEM((N_EXPERTS, 128), F32)],
        compiler_params=_cp(("arbitrary",)),
        name="route",
    )(logits_t, bias)


def _slot_kernel(start_ref, e_ref, r_ref, p_ref, *, n_rows):
    e = e_ref[...]
    pos = r_ref[...]
    for j in range(N_EXPERTS):
        pos = pos + jnp.where(e == j, start_ref[j], 0)
    for p in range(ROW_PARTS):
        p_ref[p] = pos + p * n_rows


def _slots(starts, e_k, r_k, n_rows):
    n_tok = e_k.shape[1]
    tt = ROUTE_TILE
    spec = pl.BlockSpec((TOP_K, tt), lambda i, s: (0, i))
    return pl.pallas_call(
        functools.partial(_slot_kernel, n_rows=n_rows),
        grid_spec=pltpu.PrefetchScalarGridSpec(
            num_scalar_prefetch=1, grid=(n_tok // tt,), in_specs=[spec, spec],
            out_specs=pl.BlockSpec((ROW_PARTS, TOP_K, tt), lambda i, s: (0, 0, i))),
        out_shape=jax.ShapeDtypeStruct((ROW_PARTS, TOP_K, n_tok), jnp.int32),
        compiler_params=_cp(("arbitrary",)),
        name="moe_slots",
    )(starts, e_k, r_k)


def _sc_mesh():
    return plsc.VectorSubcoreMesh(core_axis_name="c", subcore_axis_name="s")


def _sc_scatter(x, idx, n_out):
    n = idx.shape[0]
    nbt = x.shape[0] // ROW_PARTS // SC_WINDOW
    reps = n // x.shape[0]

    @functools.partial(pl.kernel, out_type=jax.ShapeDtypeStruct((n_out, SC_ROW), x.dtype), mesh=_sc_mesh())
    def k(x_hbm, i_hbm, o_hbm):
        def body(x_vmem, i_vmem):
            pltpu.sync_copy(x_vmem, o_hbm.at[i_vmem.at[0]])

        pltpu.emit_pipeline(
            body,
            grid=(n // SC_WINDOW,),
            in_specs=[pl.BlockSpec((SC_WINDOW, SC_ROW), lambda i: (i // (reps * nbt) * nbt + i % nbt, 0)),
                      pl.BlockSpec((1, SC_WINDOW), lambda i: (0, i))],
            out_specs=[],
            core_axis_name=("c", "s"),
            dimension_semantics=(pltpu.PARALLEL,),
        )(x_hbm, i_hbm)

    return k(x, idx.reshape(1, n))


def _sc_gather(table, idx):
    n = idx.shape[0]

    @functools.partial(pl.kernel, out_type=jax.ShapeDtypeStruct((n, SC_ROW), table.dtype), mesh=_sc_mesh())
    def k(x_hbm, i_hbm, o_hbm):
        def body(i_vmem, o_vmem):
            pltpu.sync_copy(x_hbm.at[i_vmem.at[0]], o_vmem)

        pltpu.emit_pipeline(
            body,
            grid=(n // SC_WINDOW,),
            in_specs=[pl.BlockSpec((1, SC_WINDOW), lambda i: (0, i))],
            out_specs=[pl.BlockSpec((SC_WINDOW, SC_ROW), lambda i: (i, 0))],
            core_axis_name=("c", "s"),
            dimension_semantics=(pltpu.PARALLEL,),
        )(i_hbm, o_hbm)

    return k(table, idx.reshape(1, n))


MOE_TM = 512


def _expert_kernel(te_ref, tr_ref, x_ref, wgu_ref, wd_ref, y_ref, wgu_bf, wd_bf):
    i = pl.program_id(0)
    valid = tr_ref[i]

    @pl.when(jnp.logical_or(i == 0, te_ref[i] != te_ref[jnp.maximum(i - 1, 0)]))
    def _():
        wgu_bf[...] = wgu_ref[0].astype(BF16)
        wd_bf[...] = wd_ref[0].astype(BF16)

    @pl.when(valid > 0)
    def _():
        live = lax.broadcasted_iota(jnp.int32, x_ref.shape[1:], 0) < valid
        x = _unpack_parts([jnp.where(live, x_ref[p], 0) for p in range(ROW_PARTS)])
        gu = _dot(x, wgu_bf[...])
        act = _silu(gu[:, :EXPERT_FF]) * gu[:, EXPERT_FF:]
        for p, part in enumerate(_pack_parts(_dot(act.astype(BF16), wd_bf[...]))):
            y_ref[p] = part

    @pl.when(valid == 0)
    def _():
        y_ref[...] = jnp.zeros(y_ref.shape, y_ref.dtype)


def _experts(tile_expert, tile_rows, xs, wgu, wd):
    _, n_rows, _ = xs.shape
    d = wgu.shape[1]
    rows = pl.BlockSpec((ROW_PARTS, MOE_TM, SC_ROW), lambda i, te, tr: (0, i, 0))
    return pl.pallas_call(
        _expert_kernel,
        grid_spec=pltpu.PrefetchScalarGridSpec(
            num_scalar_prefetch=2,
            grid=(n_rows // MOE_TM,),
            in_specs=[rows,
                      pl.BlockSpec((1, d, 2 * EXPERT_FF), lambda i, te, tr: (te[i], 0, 0)),
                      pl.BlockSpec((1, EXPERT_FF, d), lambda i, te, tr: (te[i], 0, 0))],
            out_specs=rows,
            scratch_shapes=[pltpu.VMEM((d, 2 * EXPERT_FF), BF16), pltpu.VMEM((EXPERT_FF, d), BF16)]),
        out_shape=jax.ShapeDtypeStruct(xs.shape, jnp.int32),
        compiler_params=_cp(("arbitrary",)),
        name="moe_experts",
    )(tile_expert, tile_rows, xs, wgu, wd)


def _combine_kernel(g_ref, w_ref, h_ref, x_ref, m5_ref, wsgu_ref, wsd_ref, lg_ref, lb_ref, *rest):
    h = _unpack_parts([h_ref[p, 0] for p in range(ROW_PARTS)])
    gu = _dot(h, wsgu_ref[...])
    act = _silu(gu[:, :SHARED_FF]) * gu[:, SHARED_FF:]
    f = _dot(act.astype(BF16), wsd_ref[...])
    w = w_ref[...]
    for k in range(TOP_K):
        f = f + w[:, k:k + 1] * _unpack_parts([g_ref[p, k] for p in range(ROW_PARTS)]).astype(F32)
    x_new = _layer_norm(DEEPNORM_ALPHA * x_ref[0] + m5_ref[0] * f, lg_ref[...], lb_ref[...])
    if len(rest) == 1:
        rest[0][0] = x_new
    else:
        sh_ref, sc_ref, o_ref, hn_ref = rest
        o_ref[0] = x_new
        hn_ref[0] = (x_new * (1.0 + sc_ref[0]) + sh_ref[0]).astype(hn_ref.dtype)


def _combine(g, w_k, h2p, x1, mod_l, wsgu, wsd, ln_g, ln_b, mod_next):
    b, lt, d = x1.shape
    n = lt // TOK_TILE
    tok = pl.BlockSpec((1, TOK_TILE, d), lambda i, t: (i, t, 0))
    full = lambda a: pl.BlockSpec(a.shape, lambda i, t: (0,) * a.ndim)
    in_specs = [pl.BlockSpec((ROW_PARTS, TOP_K, TOK_TILE, SC_ROW), lambda i, t: (0, 0, i * n + t, 0)),
                pl.BlockSpec((TOK_TILE, TOP_K), lambda i, t: (i * n + t, 0)),
                pl.BlockSpec((ROW_PARTS, 1, TOK_TILE, SC_ROW), lambda i, t: (0, i, t, 0)),
                tok, _mod_spec(b, 5), full(wsgu), full(wsd), full(ln_g), full(ln_b)]
    args = [g, w_k, h2p, x1, mod_l, wsgu, wsd, ln_g, ln_b]
    out_specs, out_shape = tok, jax.ShapeDtypeStruct((b, lt, d), F32)
    if mod_next is not None:
        in_specs += [_mod_spec(b, 0), _mod_spec(b, 1)]
        args += [mod_next, mod_next]
        out_specs, out_shape = [tok, tok], [out_shape, jax.ShapeDtypeStruct((b, lt, d), BF16)]
    return pl.pallas_call(
        _combine_kernel,
        grid=(b, n),
        in_specs=in_specs,
        out_specs=out_specs,
        out_shape=out_shape,
        compiler_params=_cp(("arbitrary", "arbitrary")),
        name="moe_combine",
    )(*args)


def _moe_sparse(h2p, logits_t, bias, wgu, wd, wsgu, wsd, x1, mod_l, ln_g, ln_b, mod_next):
    n_tok = h2p.shape[1] * h2p.shape[2]
    e_k, r_k, w_k, counts = _route(logits_t, bias)
    counts = counts[:, 0].astype(jnp.int32)
    padded = (counts + MOE_TM - 1) // MOE_TM * MOE_TM
    ends = jnp.cumsum(padded)
    starts = ends - padded
    n_rows = (n_tok * TOP_K + N_EXPERTS * (MOE_TM - 1)) // MOE_TM * MOE_TM
    tile0 = jnp.arange(n_rows // MOE_TM, dtype=jnp.int32) * MOE_TM
    tile_expert = jnp.minimum(jnp.sum(tile0[:, None] >= ends[None, :], axis=1), N_EXPERTS - 1).astype(jnp.int32)
    tile_rows = jnp.clip(counts[tile_expert] - (tile0 - starts[tile_expert]), 0, MOE_TM).astype(jnp.int32)
    idx = _slots(starts.astype(jnp.int32), e_k, r_k, n_rows).reshape(-1)
    xs = _sc_scatter(h2p.reshape(ROW_PARTS * n_tok, SC_ROW), idx, ROW_PARTS * n_rows)
    ys = _experts(tile_expert, tile_rows, xs.reshape(ROW_PARTS, n_rows, SC_ROW), wgu, wd)
    g = _sc_gather(ys.reshape(ROW_PARTS * n_rows, SC_ROW), idx).reshape(ROW_PARTS, TOP_K, n_tok, SC_ROW)
    return _combine(g, w_k, h2p, x1, mod_l, wsgu, wsd, ln_g, ln_b, mod_next)


def _rope_tables(n_ctx, n_lat):
    rows = n_lat // GRID_W
    row = jnp.broadcast_to(jnp.arange(rows, dtype=F32)[:, None], (rows, GRID_W)).reshape(-1)
    col = jnp.broadcast_to(jnp.arange(GRID_W, dtype=F32)[None, :], (rows, GRID_W)).reshape(-1)
    inv = ROPE_BASE ** (-jnp.arange(ROPE_FREQS, dtype=F32) / ROPE_FREQS)
    ang = jnp.stack([row[:, None] * inv, col[:, None] * inv], axis=1)
    cos = jnp.cos(ang)[:, :, None, :]
    sin = jnp.sin(ang)[:, :, None, :]
    cos32 = jnp.broadcast_to(cos, (n_lat, 2, 2, ROPE_FREQS)).reshape(n_lat, MLA_ROPE)
    sin32 = jnp.concatenate([-sin, sin], axis=2).reshape(n_lat, MLA_ROPE)
    cos32 = jnp.concatenate([jnp.ones((n_ctx, MLA_ROPE), F32), cos32], axis=0)
    sin32 = jnp.concatenate([jnp.zeros((n_ctx, MLA_ROPE), F32), sin32], axis=0)
    lt = n_ctx + n_lat
    ca = jnp.concatenate([jnp.ones((lt, MLA_NOPE), F32), cos32, jnp.zeros((lt, 32), F32)], axis=1)
    sb = jnp.concatenate([jnp.zeros((lt, MLA_NOPE), F32), sin32, jnp.zeros((lt, 32), F32)], axis=1)
    return ca, sb


def _rope_partner():
    idx = np.arange(MLA_ROPE).reshape(2, 2, ROPE_FREQS)
    return idx[:, ::-1, :].reshape(-1)


def _pack_layer(w_in, w_q_b, w_kv_b):
    sizes = (MLA_Q_LORA, MLA_KV_LORA + MLA_ROPE, REC_W, REC_W, REC_W, REC_W, REC_W,
             3 * REC_W, REC_W, 2 * REC_HEADS, 2 * REC_HEADS, N_BRANCH * D_MODEL)
    offs = np.cumsum((0,) + sizes)
    w_in = w_in.astype(BF16)
    seg = lambda i: w_in[:, offs[i]:offs[i + 1]]
    d = w_in.shape[0]
    zeros = lambda n: jnp.zeros((d, n), w_in.dtype)
    partner = _rope_partner()
    kva = seg(1)
    k_rope = kva[:, MLA_KV_LORA:]
    w_main = jnp.concatenate([seg(11), seg(2), seg(3), seg(6), seg(8)], axis=1)
    w_f = jnp.concatenate([seg(4), seg(5)], axis=1)
    w_mla = jnp.concatenate([seg(0), kva[:, :MLA_KV_LORA],
                             zeros(MLA_NOPE), k_rope, zeros(32),
                             zeros(MLA_NOPE), k_rope[:, partner], zeros(32)], axis=1)
    w_gdn = jnp.concatenate([seg(7), seg(9), seg(10), zeros(256 - 4 * REC_HEADS)], axis=1)
    r = w_q_b.shape[0]
    qb = w_q_b.reshape(r, MLA_HEADS, MLA_NOPE + MLA_ROPE)
    zq = lambda n: jnp.zeros((r, MLA_HEADS, n), w_q_b.dtype)
    wq1 = jnp.concatenate([qb, zq(32)], axis=2).reshape(r, MLA_HEADS * 128)
    wq2 = jnp.concatenate([zq(MLA_NOPE), qb[:, :, MLA_NOPE:][:, :, partner], zq(32)], axis=2).reshape(r, MLA_HEADS * 128)
    rk = w_kv_b.shape[0]
    kvb = w_kv_b.reshape(rk, MLA_HEADS, MLA_NOPE + MLA_V)
    wk = jnp.concatenate([kvb[:, :, :MLA_NOPE], jnp.zeros((rk, MLA_HEADS, 64), w_kv_b.dtype)], axis=2)
    wk = wk.reshape(rk, MLA_HEADS * 128)
    wv = jnp.concatenate([kvb[:, :, MLA_NOPE:], jnp.zeros((rk, MLA_HEADS, 128 - MLA_V), w_kv_b.dtype)], axis=2)
    wv = wv.reshape(rk, MLA_HEADS * 128)
    bf = lambda a: a.astype(BF16)
    return bf(w_main), bf(w_f), bf(w_mla), bf(w_gdn), bf(wq1), bf(wq2), bf(wk), bf(wv)


def kernel(x, c, ctx, c_ctx, w_mod, b_mod, w_in, q_a_norm, w_q_b, kv_a_norm, w_kv_b, hg_lb_logits, hg_norm,
           gdn_conv, gdn_a_log, gdn_dt_bias, gdn_norm, w_branch, w_out, ln1_g, ln1_b, ln2_g, ln2_b,
           w_router, router_bias, w_gu, w_down, w_sh_gu, w_sh_down):
    batch, n_lat, d = x.shape
    n_ctx = ctx.shape[1]
    assert n_ctx == TOK_TILE and n_lat % TOK_TILE == 0 and batch < MOD_ROWS and d == D_MODEL
    lt = n_ctx + n_lat

    c_all = jnp.zeros((MOD_ROWS, d), F32).at[:batch].set(c).at[batch].set(c_ctx)
    mod = _mod_all(c_all, w_mod, b_mod).reshape(DEPTH, MOD_ROWS, 1, 6 * d)
    ca, sb = _rope_tables(n_ctx, n_lat)
    lb_soft = jax.nn.softmax(hg_lb_logits.astype(F32), axis=0)
    lower = (jnp.cumsum(lb_soft, axis=0) - lb_soft[0]).reshape(DEPTH, 2, 1, REC_W)
    row = lambda a: a.reshape(1, -1)

    x_all = jnp.concatenate([ctx, x], axis=1)
    h = _modulate(x_all, mod[0])
    for l in range(DEPTH):
        w_main, w_f, w_mla, w_gdn, wq1, wq2, wk, wv = _pack_layer(w_in[l], w_q_b[l], w_kv_b[l])
        mod_l = mod[l]
        z_main = _proj(h, w_main, 2 * REC_W, BF16)
        z_f = _proj(h, w_f, REC_W)
        z_mla = _proj(h, w_mla, ZA_WIDTH)
        z_gdn = _proj(h, w_gdn, ZG_WIDTH // 2)

        q, k, v = _mla_prep(z_mla, row(q_a_norm[l]), row(kv_a_norm[l]), wq1, wq2, wk, wv, ca, sb)
        mla_o = _attention(q, k, v, n_ctx)

        hg_f, hg_b = _hgrn2(z_main, z_f, lower[l])

        qkv = _gdn_prep(z_gdn, gdn_conv[l], n_ctx)
        par = jnp.zeros((8, 128), F32)
        par = par.at[0, :2 * REC_HEADS].set(gdn_a_log[l].reshape(-1))
        par = par.at[1, :2 * REC_HEADS].set(gdn_dt_bias[l].reshape(-1))
        gd_f, gd_b = _gdn(qkv, z_gdn, par)

        x1, h2, logits_t = _merge(mla_o, hg_f, hg_b, gd_f, gd_b, z_main, x_all, mod_l,
                                  row(hg_norm[l]), row(gdn_norm[l]), w_branch[l].astype(BF16),
                                  w_out[l].astype(BF16), row(ln1_g[l]), row(ln1_b[l]), w_router[l].T)
        out = _moe_sparse(h2, logits_t,
                          router_bias[l].reshape(N_GROUPS, N_EXPERTS // N_GROUPS, 1),
                          w_gu[l], w_down[l],
                          w_sh_gu[l].astype(BF16), w_sh_down[l].astype(BF16),
                          x1, mod_l, row(ln2_g[l]), row(ln2_b[l]), mod[l + 1] if l + 1 < DEPTH else None)
        x_all, h = out if l + 1 < DEPTH else (out, None)
    return x_all[:, n_ctx:, :]
```

```python
import functools
import math

import numpy as np
import jax
import jax.numpy as jnp
from jax import lax
from jax.experimental import pallas as pl
from jax.experimental.pallas import tpu as pltpu
from jax.experimental.pallas import tpu_sc as plsc

F32 = jnp.float32
BF16 = jnp.bfloat16
HIGHEST = lax.Precision.HIGHEST

D_MODEL = 1024
DEPTH = 4
GRID_W = 64
MLA_HEADS = 8
MLA_Q_LORA = 384
MLA_KV_LORA = 256
MLA_NOPE = 64
MLA_ROPE = 32
MLA_V = 64
MLA_SCALE = (MLA_NOPE + MLA_ROPE) ** -0.5
ROPE_BASE = 10000.0
ROPE_FREQS = MLA_ROPE // 4
REC_HEADS = 4
REC_D = 128
REC_W = REC_HEADS * REC_D
CONV_K = 5
N_BRANCH = 3
BRANCH_W = 512
N_EXPERTS = 64
TOP_K = 8
N_GROUPS = 8
TOPK_GROUPS = 4
EXPERT_FF = 256
SHARED_FF = 256
ROUTED_SCALE = 2.5
DEEPNORM_ALPHA = (2 * DEPTH) ** 0.25
LN_EPS = 1e-6
RMS_EPS = 1e-6

TOK_TILE = 256
CHUNK = 64
MOD_ROWS = 16
VMEM_LIMIT = 56 * 1024 * 1024

ZM_Q, ZM_I, ZM_HGATE, ZM_GGATE = (3 * D_MODEL // REC_W + k for k in range(4))
ZA_WIDTH = MLA_Q_LORA + MLA_KV_LORA + 256
ZG_CONV = 3 * REC_W
ZG_WIDTH = ZG_CONV + 256


def _cp(sem, vmem=VMEM_LIMIT):
    return pltpu.CompilerParams(dimension_semantics=sem, vmem_limit_bytes=vmem)


def _dot(a, b, precision=None):
    return jnp.dot(a, b, preferred_element_type=F32, precision=precision)


def _dot_nt(a, b, precision=None):
    return lax.dot_general(a, b, (((1,), (1,)), ((), ())), preferred_element_type=F32, precision=precision)


def _sigmoid(x):
    return 1.0 / (1.0 + jnp.exp(-x))


def _silu(x):
    return x * _sigmoid(x)


ROW_PARTS = 2
SC_ROW = D_MODEL // (2 * ROW_PARTS)
SC_WINDOW = 128


def _pack_parts(x):
    q = x.shape[1] // (2 * ROW_PARTS)
    bits = lambda a: lax.bitcast_convert_type(a.astype(BF16).astype(F32), jnp.uint32)
    parts = []
    for p in range(ROW_PARTS):
        hi = bits(x[:, p * q:(p + 1) * q])
        lo = bits(x[:, (ROW_PARTS + p) * q:(ROW_PARTS + p + 1) * q])
        parts.append(lax.bitcast_convert_type(hi | lax.shift_right_logical(lo, jnp.uint32(16)), jnp.int32))
    return parts


def _unpack_parts(parts):
    his, los = [], []
    for p in parts:
        u = lax.bitcast_convert_type(p, jnp.uint32)
        his.append(lax.bitcast_convert_type(u & jnp.uint32(0xFFFF0000), F32).astype(BF16))
        los.append(lax.bitcast_convert_type(lax.shift_left(u, jnp.uint32(16)), F32).astype(BF16))
    return jnp.concatenate(his + los, axis=1)


def _mod_kernel(c_ref, w_ref, b_ref, o_ref):
    s = _silu(c_ref[...])
    o_ref[0] = _dot(s, w_ref[0], HIGHEST) + b_ref[0]


def _mod_all(c_all, w_mod, b_mod):
    tn = 1024
    n = w_mod.shape[-1]
    return pl.pallas_call(
        _mod_kernel,
        grid=(DEPTH, n // tn),
        in_specs=[
            pl.BlockSpec((MOD_ROWS, D_MODEL), lambda l, j: (0, 0)),
            pl.BlockSpec((1, D_MODEL, tn), lambda l, j: (l, 0, j)),
            pl.BlockSpec((1, 1, tn), lambda l, j: (l, 0, j)),
        ],
        out_specs=pl.BlockSpec((1, MOD_ROWS, tn), lambda l, j: (l, 0, j)),
        out_shape=jax.ShapeDtypeStruct((DEPTH, MOD_ROWS, n), F32),
        compiler_params=_cp(("arbitrary", "arbitrary")),
        name="mod_all",
    )(c_all, w_mod, b_mod.reshape(DEPTH, 1, n))


def _mod_spec(batch, k):
    return pl.BlockSpec((1, 1, D_MODEL), lambda b, t: (jnp.where(t == 0, batch, b), 0, k))


def _modulate_kernel(x_ref, sh_ref, sc_ref, o_ref):
    o_ref[0] = (x_ref[0] * (1.0 + sc_ref[0]) + sh_ref[0]).astype(o_ref.dtype)


def _modulate(x_all, mod_l):
    b, lt, d = x_all.shape
    return pl.pallas_call(
        _modulate_kernel,
        grid=(b, lt // TOK_TILE),
        in_specs=[
            pl.BlockSpec((1, TOK_TILE, d), lambda i, t: (i, t, 0)),
            _mod_spec(b, 0),
            _mod_spec(b, 1),
        ],
        out_specs=pl.BlockSpec((1, TOK_TILE, d), lambda i, t: (i, t, 0)),
        out_shape=jax.ShapeDtypeStruct((b, lt, d), BF16),
        compiler_params=_cp(("arbitrary", "arbitrary")),
        name="modulate",
    )(x_all, mod_l, mod_l)


PROJ_ROWS = 256


def _proj_kernel(h_ref, w_ref, o_ref, *, rows):
    def body(r, carry):
        sl = pl.ds(pl.multiple_of(r * rows, rows), rows)
        o_ref[0, sl, :] = _dot(h_ref[0, sl, :], w_ref[...]).astype(o_ref.dtype)
        return carry

    lax.fori_loop(0, h_ref.shape[1] // rows, body, 0)


def _proj(h, w, layer, tn, dtype=F32):
    b, lt, d = h.shape
    n = w.shape[2]
    return pl.pallas_call(
        functools.partial(_proj_kernel, rows=PROJ_ROWS),
        grid=(b, n // tn),
        in_specs=[
            pl.BlockSpec((1, lt, d), lambda i, j: (i, 0, 0)),
            pl.BlockSpec((None, d, tn), lambda i, j: (layer, 0, j)),
        ],
        out_specs=pl.BlockSpec((1, lt, tn), lambda i, j: (i, 0, j)),
        out_shape=jax.ShapeDtypeStruct((b, lt, n), dtype),
        compiler_params=_cp(("arbitrary", "arbitrary")),
        name="proj",
    )(h, w)


def _rms(x, g):
    return x * lax.rsqrt(jnp.mean(x * x, axis=-1, keepdims=True) + RMS_EPS) * g


def _mla_prep_kernel(z_ref, qg_ref, kg_ref, wq1_ref, wq2_ref, wk_ref, wv_ref, ca_ref, sb_ref,
                     q_ref, k_ref, v_ref):
    z = z_ref[0]
    ca = ca_ref[...]
    sb = sb_ref[...]
    qn = _rms(z[:, :MLA_Q_LORA], qg_ref[...]).astype(BF16)
    qa = _dot(qn, wq1_ref[...])
    qb = _dot(qn, wq2_ref[...])
    kvn = _rms(z[:, MLA_Q_LORA:MLA_Q_LORA + MLA_KV_LORA], kg_ref[...]).astype(BF16)
    kn = _dot(kvn, wk_ref[...])
    v = _dot(kvn, wv_ref[...])
    lane = lax.broadcasted_iota(jnp.int32, v.shape, 1)
    v_ref[0] = jnp.where((lane & 127) == MLA_V, 1.0, v).astype(v_ref.dtype)
    o = MLA_Q_LORA + MLA_KV_LORA
    kr = z[:, o:o + 128] * ca + z[:, o + 128:o + 256] * sb
    for h in range(MLA_HEADS):
        sl = slice(h * 128, (h + 1) * 128)
        q_ref[0, :, sl] = ((qa[:, sl] * ca + qb[:, sl] * sb) * MLA_SCALE).astype(q_ref.dtype)
        k_ref[0, :, sl] = (kn[:, sl] + kr).astype(k_ref.dtype)


def _mla_prep(z_mla, qg, kg, wq1, wq2, wk, wv, ca, sb):
    b, lt, _ = z_mla.shape
    full = lambda a: pl.BlockSpec(a.shape, lambda i, t: (0,) * a.ndim)
    tok = lambda w: pl.BlockSpec((1, TOK_TILE, w), lambda i, t: (i, t, 0))
    return pl.pallas_call(
        _mla_prep_kernel,
        grid=(b, lt // TOK_TILE),
        in_specs=[tok(ZA_WIDTH), full(qg), full(kg), full(wq1), full(wq2), full(wk), full(wv),
                  pl.BlockSpec((TOK_TILE, 128), lambda i, t: (t, 0)),
                  pl.BlockSpec((TOK_TILE, 128), lambda i, t: (t, 0))],
        out_specs=[tok(MLA_HEADS * 128)] * 3,
        out_shape=[jax.ShapeDtypeStruct((b, lt, MLA_HEADS * 128), BF16)] * 3,
        compiler_params=_cp(("arbitrary", "arbitrary")),
        name="mla_prep",
    )(z_mla, qg, kg, wq1, wq2, wk, wv, ca, sb)


ATTN_KEY_CHUNK = 768


def _attn_kernel(q_ref, k_ref, v_ref, o_ref, *, n_ctx):
    def attend(nk):
        kc = min(nk, ATTN_KEY_CHUNK)
        cols = [slice(h * 128, (h + 1) * 128) for h in range(2)]
        s = [[_dot_nt(q_ref[0, :, cols[h]], k_ref[0, c:c + kc, cols[h]]) for c in range(0, nk, kc)]
             for h in range(2)]
        m = [functools.reduce(jnp.maximum, [jnp.max(sc, axis=1, keepdims=True) for sc in s[h]]) for h in range(2)]
        o = [jnp.zeros((TOK_TILE, 128), F32) for _ in range(2)]
        for j, c in enumerate(range(0, nk, kc)):
            for h in range(2):
                p = jnp.exp(s[h][j] - m[h]).astype(BF16)
                o[h] = o[h] + _dot(p, v_ref[0, c:c + kc, cols[h]])
        outs = [o[h][:, :MLA_V] / o[h][:, MLA_V:MLA_V + 1] for h in range(2)]
        o_ref[0] = jnp.concatenate(outs, axis=1).astype(o_ref.dtype)

    @pl.when(pl.program_id(2) == 0)
    def _():
        attend(n_ctx)

    @pl.when(pl.program_id(2) > 0)
    def _():
        attend(k_ref.shape[1])


def _attention(q, k, v, n_ctx):
    b, lt, _ = q.shape
    return pl.pallas_call(
        functools.partial(_attn_kernel, n_ctx=n_ctx),
        grid=(b, MLA_HEADS // 2, lt // TOK_TILE),
        in_specs=[
            pl.BlockSpec((1, TOK_TILE, 256), lambda i, h, t: (i, t, h)),
            pl.BlockSpec((1, lt, 256), lambda i, h, t: (i, 0, h)),
            pl.BlockSpec((1, lt, 256), lambda i, h, t: (i, 0, h)),
        ],
        out_specs=pl.BlockSpec((1, TOK_TILE, 2 * MLA_V), lambda i, h, t: (i, t, h)),
        out_shape=jax.ShapeDtypeStruct((b, lt, MLA_HEADS * MLA_V), BF16),
        compiler_params=_cp(("arbitrary", "arbitrary", "arbitrary")),
        name="mla_attention",
    )(q, k, v)


def _rev_tile(s, n):
    return jnp.where(s == 0, 0, n - s)


def _time_index(shape, dim, reverse):
    i = lax.broadcasted_iota(jnp.int32, shape, dim)
    return (shape[dim] - 1 - i) if reverse else i


def _blk(i, size):
    return lax.shift_right_logical(i, jnp.int32(int(math.log2(size))))


def _gla_chunks(items):
    c_len = items[0]["q"].shape[0]
    for it in items:
        it["c"] = it["g"]
        it["t"] = it["g"]
        it["a"] = jnp.zeros((c_len, c_len), F32)
    hb = 1
    while hb < c_len:
        masks = {}
        for rev in {it["reverse"] for it in items}:
            ri = _time_index((c_len, c_len), 0, rev)
            ci = _time_index((c_len, c_len), 1, rev)
            pair = jnp.logical_and(_blk(ri, hb) == _blk(ci, hb) + 1, (_blk(ri, hb) & 1) == 1)
            odd = (_blk(_time_index((c_len, REC_D), 0, rev), hb) & 1) == 1
            masks[rev] = (pair, odd)
        for it in items:
            rev = it["reverse"]
            pair, odd = masks[rev]
            c, t = it["c"], it["t"]
            qt = (it["q"] * jnp.exp(c)).astype(BF16)
            kt = (it["k"] * jnp.exp(t - c)).astype(BF16)
            it["a"] = it["a"] + jnp.where(pair, _dot_nt(qt, kt), 0.0)
            t_lo = pltpu.roll(t, hb, 0)
            t_hi = pltpu.roll(t, c_len - hb, 0)
            prev, nxt = (t_hi, t_lo) if rev else (t_lo, t_hi)
            it["c"] = c + jnp.where(odd, prev, 0.0)
            it["t"] = t + jnp.where(odd, prev, nxt)
        hb *= 2
    outs = []
    for it in items:
        q, k, v, c, t = it["q"], it["k"], it["v"], it["c"], it["t"]
        st = it["st_ref"][...]
        qk = jnp.sum(q * k, axis=1, keepdims=True)
        outs.append(_dot_nt((q * jnp.exp(c)).astype(BF16), st.astype(BF16))
                    + _dot(it["a"].astype(BF16), v.astype(BF16)) + qk * v)
        kd = (k * jnp.exp(t - c)).astype(BF16)
        it["st_ref"][...] = st * jnp.exp(t[0:1, :]) + _dot(v.T.astype(BF16), kd)
    return outs


def _hgrn2_kernel(qf_ref, vf_ref, ff_ref, qb_ref, vb_ref, fb_ref, lb_ref, of_ref, ob_ref, st_ref):
    @pl.when(pl.program_id(1) == 0)
    def _():
        st_ref[...] = jnp.zeros(st_ref.shape, F32)

    n_chunks = TOK_TILE // CHUNK

    def body(ci, carry):
        items, dests = [], []
        for d, (q_ref, v_ref, f_ref, o_ref) in enumerate(((qf_ref, vf_ref, ff_ref, of_ref),
                                                           (qb_ref, vb_ref, fb_ref, ob_ref))):
            cc = ci if d == 0 else n_chunks - 1 - ci
            rows = pl.ds(pl.multiple_of(cc * CHUNK, CHUNK), CHUNK)
            for h in range(REC_HEADS):
                cols = slice(h * REC_D, (h + 1) * REC_D)
                lb = lb_ref[d, :, cols]
                f = lb + (1.0 - lb) * _sigmoid(f_ref[0, rows, cols])
                items.append(dict(q=_silu(q_ref[0, rows, cols].astype(F32)), k=1.0 - f,
                                  v=v_ref[0, rows, cols].astype(F32),
                                  g=jnp.log(f), st_ref=st_ref.at[d, h], reverse=(d == 1)))
                dests.append((o_ref, rows, cols))
        for (o_ref, rows, cols), o in zip(dests, _gla_chunks(items)):
            o_ref[0, rows, cols] = o
        return carry

    lax.fori_loop(0, n_chunks, body, 0)


def _hgrn2(z_main, z_f, lb):
    b, lt, _ = z_main.shape
    n = lt // TOK_TILE
    fwd = lambda k: pl.BlockSpec((1, TOK_TILE, REC_W), lambda i, s: (i, s, k))
    bwd = lambda k: pl.BlockSpec((1, TOK_TILE, REC_W), lambda i, s: (i, _rev_tile(s, n), k))
    return pl.pallas_call(
        _hgrn2_kernel,
        grid=(b, n),
        in_specs=[fwd(ZM_Q), fwd(ZM_I), fwd(0), bwd(ZM_Q), bwd(ZM_I), bwd(1),
                  pl.BlockSpec((2, 1, REC_W), lambda i, s: (0, 0, 0))],
        out_specs=[pl.BlockSpec((1, TOK_TILE, REC_W), lambda i, s: (i, s, 0)),
                   pl.BlockSpec((1, TOK_TILE, REC_W), lambda i, s: (i, _rev_tile(s, n), 0))],
        out_shape=[jax.ShapeDtypeStruct((b, lt, REC_W), F32)] * 2,
        scratch_shapes=[pltpu.VMEM((2, REC_HEADS, REC_D, REC_D), F32)],
        compiler_params=_cp(("arbitrary", "arbitrary")),
        name="hgrn2_scan",
    )(z_main, z_main, z_f, z_main, z_main, z_f, lb)


def _gdn_prep_kernel(x_ref, w_ref, o_ref, *, n_ctx):
    x = x_ref[0]
    lt = x.shape[0]
    t = lax.broadcasted_iota(jnp.int32, x.shape, 0)
    lo = jnp.where(t < n_ctx, 0, n_ctx)
    hi = jnp.where(t < n_ctx, n_ctx, lt)
    acc = x * w_ref[CONV_K // 2:CONV_K // 2 + 1, :]
    for kk in range(CONV_K):
        off = kk - CONV_K // 2
        if off == 0:
            continue
        xs = pltpu.roll(x, (-off) % lt, 0)
        ok = jnp.logical_and(t + off >= lo, t + off < hi)
        acc = acc + jnp.where(ok, xs, 0.0) * w_ref[kk:kk + 1, :]
    y = _silu(acc)
    inv = lax.rsqrt(jnp.sum(y * y, axis=1, keepdims=True) + 1e-6)
    is_qk = pl.program_id(1) < 2 * REC_HEADS
    o_ref[0] = y * jnp.where(is_qk, inv, 1.0)


def _gdn_prep(z_gdn, conv_w, n_ctx):
    b, lt, _ = z_gdn.shape
    return pl.pallas_call(
        functools.partial(_gdn_prep_kernel, n_ctx=n_ctx),
        grid=(b, ZG_CONV // REC_D),
        in_specs=[pl.BlockSpec((1, lt, REC_D), lambda i, j: (i, 0, j)),
                  pl.BlockSpec((CONV_K, REC_D), lambda i, j: (0, j))],
        out_specs=pl.BlockSpec((1, lt, REC_D), lambda i, j: (i, 0, j)),
        out_shape=jax.ShapeDtypeStruct((b, lt, ZG_CONV), F32),
        compiler_params=_cp(("arbitrary", "arbitrary")),
        name="gdn_prep",
    )(z_gdn, conv_w)


def _gdn_tile(items):
    t_len = items[0]["q"].shape[0]
    n_chunks = t_len // CHUNK
    for it in items:
        ri = _time_index((t_len, t_len), 0, it["reverse"])
        ci = _time_index((t_len, t_len), 1, it["reverse"])
        it["ri"], it["ci"] = ri, ci
        same = _blk(ri, CHUNK) == _blk(ci, CHUNK)
        it["dec"] = jnp.where(jnp.logical_and(same, ri >= ci),
                              jnp.exp(jnp.minimum(it["gcol"] - it["grow"], 0.0)), 0.0)
        it["kb"] = it["k"] * it["beta"]
        it["kbf"] = it["k"].astype(BF16)
    for it in items:
        ri, ci = it["ri"], it["ci"]
        lm = jnp.where(ri > ci, _dot_nt(it["kb"].astype(BF16), it["kbf"]) * it["dec"], 0.0)
        it["lm"] = lm
        it["tinv"] = jnp.where(ri == ci, 1.0, 0.0) - jnp.where(_blk(ri, 2) == _blk(ci, 2), lm, 0.0)
    hb = 2
    while hb < CHUNK:
        for it in items:
            ri, ci = it["ri"], it["ci"]
            off = jnp.where(_blk(ri, 2 * hb) == _blk(ci, 2 * hb),
                            jnp.where(_blk(ri, hb) == _blk(ci, hb), 0.0, it["lm"]), 0.0)
            it["tb"] = it["tinv"].astype(BF16)
            it["to"] = _dot(it["tb"], off.astype(BF16)).astype(BF16)
        for it in items:
            it["tinv"] = it["tinv"] - _dot(it["to"], it["tb"])
        hb *= 2
    for it in items:
        eg = jnp.exp(it["gcol"])
        tb = it["tinv"].astype(BF16)
        it["u"] = _dot(tb, (it["v"] * it["beta"]).astype(BF16))
        it["w"] = _dot(tb, (it["kb"] * eg).astype(BF16)).astype(BF16)
        qs = it["q"] * (REC_D ** -0.5)
        it["aqk"] = (_dot_nt(qs.astype(BF16), it["kbf"]) * it["dec"]).astype(BF16)
        it["qd"] = (qs * eg).astype(BF16)
        kd = it["k"] * jnp.exp(it["gtot"] - it["gcol"])
        it["kdt"] = [kd[c * CHUNK:(c + 1) * CHUNK].T.astype(BF16) for c in range(n_chunks)]
        it["s"] = it["s_ref"][...]
        it["o"], it["vn"] = [None] * n_chunks, [None] * n_chunks
    for step in range(n_chunks):
        for it in items:
            c = n_chunks - 1 - step if it["reverse"] else step
            it["c"] = c
            rows = slice(c * CHUNK, (c + 1) * CHUNK)
            it["sb"] = it["s"].astype(BF16)
            it["vn"][c] = (it["u"][rows] - _dot(it["w"][rows], it["sb"])).astype(BF16)
        for it in items:
            c = it["c"]
            rows = slice(c * CHUNK, (c + 1) * CHUNK)
            it["o"][c] = _dot(it["qd"][rows], it["sb"])
            it["s"] = (it["s"] * jnp.exp(it["gtot"][c * CHUNK:c * CHUNK + 1])
                       + _dot(it["kdt"][c], it["vn"][c]))
    outs = []
    for it in items:
        it["s_ref"][...] = it["s"]
        outs.append(jnp.concatenate(it["o"], axis=0) + _dot(it["aqk"], jnp.concatenate(it["vn"], axis=0)))
    return outs


def _gdn_kernel(qf_ref, kf_ref, vf_ref, gf_ref, qb_ref, kb_ref, vb_ref, gb_ref, par_ref, of_ref, ob_ref, s_ref):
    @pl.when(pl.program_id(1) == 0)
    def _():
        s_ref[...] = jnp.zeros(s_ref.shape, F32)

    neg_a = -jnp.exp(par_ref[0:1, :])
    dt_bias = par_ref[1:2, :]
    r2 = lax.broadcasted_iota(jnp.int32, (TOK_TILE, TOK_TILE), 0)
    c2 = lax.broadcasted_iota(jnp.int32, (TOK_TILE, TOK_TILE), 1)
    same = _blk(r2, CHUNK) == _blk(c2, CHUNK)
    items, dests = [], []
    for d, (q_ref, k_ref, v_ref, g_ref, o_ref) in enumerate(((qf_ref, kf_ref, vf_ref, gf_ref, of_ref),
                                                              (qb_ref, kb_ref, vb_ref, gb_ref, ob_ref))):
        ab = g_ref[0]
        xa = ab + dt_bias
        g = neg_a * (jnp.maximum(xa, 0.0) + jnp.log(1.0 + jnp.exp(-jnp.abs(xa))))
        tri = jnp.where(jnp.logical_and(same, (r2 <= c2) if d == 1 else (r2 >= c2)), 1.0, 0.0)
        gc = _dot(tri, g, HIGHEST)
        gtot = _dot(jnp.where(same, 1.0, 0.0), g, HIGHEST)
        gct = gc.T
        beta = _sigmoid(ab)
        for h in range(REC_HEADS):
            cols = slice(h * REC_D, (h + 1) * REC_D)
            ln = d * REC_HEADS + h
            items.append(dict(
                q=q_ref[0, :, cols], k=k_ref[0, :, cols], v=v_ref[0, :, cols],
                gcol=gc[:, ln:ln + 1], grow=gct[ln:ln + 1, :], gtot=gtot[:, ln:ln + 1],
                beta=beta[:, 2 * REC_HEADS + ln:2 * REC_HEADS + ln + 1], s_ref=s_ref.at[d, h],
                reverse=(d == 1)))
            dests.append((o_ref, cols))
    for (o_ref, cols), o in zip(dests, _gdn_tile(items)):
        o_ref[0, :, cols] = o


def _gdn(qkv, z_gdn, par):
    b, lt, _ = qkv.shape
    n = lt // TOK_TILE
    fwd = lambda k: pl.BlockSpec((1, TOK_TILE, REC_W), lambda i, s: (i, s, k))
    bwd = lambda k: pl.BlockSpec((1, TOK_TILE, REC_W), lambda i, s: (i, _rev_tile(s, n), k))
    gcol = ZG_CONV // 128
    return pl.pallas_call(
        _gdn_kernel,
        grid=(b, n),
        in_specs=[fwd(0), fwd(1), fwd(2), pl.BlockSpec((1, TOK_TILE, 128), lambda i, s: (i, s, gcol)),
                  bwd(0), bwd(1), bwd(2), pl.BlockSpec((1, TOK_TILE, 128), lambda i, s: (i, _rev_tile(s, n), gcol)),
                  pl.BlockSpec((8, 128), lambda i, s: (0, 0))],
        out_specs=[pl.BlockSpec((1, TOK_TILE, REC_W), lambda i, s: (i, s, 0)),
                   pl.BlockSpec((1, TOK_TILE, REC_W), lambda i, s: (i, _rev_tile(s, n), 0))],
        out_shape=[jax.ShapeDtypeStruct((b, lt, REC_W), F32)] * 2,
        scratch_shapes=[pltpu.VMEM((2, REC_HEADS, REC_D, REC_D), F32)],
        compiler_params=_cp(("arbitrary", "arbitrary")),
        name="gdn_scan",
    )(qkv, qkv, qkv, z_gdn, qkv, qkv, qkv, z_gdn, par)


def _layer_norm(x, g, b):
    mu = jnp.mean(x, axis=-1, keepdims=True)
    xc = x - mu
    var = jnp.mean(xc * xc, axis=-1, keepdims=True)
    return xc * lax.rsqrt(var + LN_EPS) * g + b


def _head_norm_gate(o, gate, w):
    outs = []
    for h in range(REC_HEADS):
        cols = slice(h * REC_D, (h + 1) * REC_D)
        oh = o[:, cols]
        n = oh * lax.rsqrt(jnp.mean(oh * oh, axis=-1, keepdims=True) + RMS_EPS) * w
        outs.append(n * _silu(gate[:, cols].astype(F32)))
    return jnp.concatenate(outs, axis=1)


def _merge_kernel(mla_ref, hf_ref, hb_ref, gf_ref, gb_ref, gates_ref, hgate_ref, ggate_ref, x_ref,
                  m2_ref, m3_ref, m4_ref, hw_ref, gw_ref, wb_ref, wo_ref, lg_ref, lbias_ref, wr_ref,
                  x1_ref, h2_ref, lt_ref):
    hg = _head_norm_gate(hf_ref[0] + hb_ref[0], hgate_ref[0], hw_ref[...]).astype(BF16)
    gd = _head_norm_gate(gf_ref[0] + gb_ref[0], ggate_ref[0], gw_ref[...]).astype(BF16)
    y = jnp.zeros((TOK_TILE, D_MODEL), F32)
    for n, o in enumerate((mla_ref[0], hg, gd)):
        y = y + _sigmoid(gates_ref[0, :, n * D_MODEL:(n + 1) * D_MODEL].astype(F32)) * _dot(o, wb_ref[n])
    y = _dot(y.astype(BF16), wo_ref[...])
    x1 = _layer_norm(DEEPNORM_ALPHA * x_ref[0] + m2_ref[0] * y, lg_ref[...], lbias_ref[...])
    x1_ref[0] = x1
    h2 = x1 * (1.0 + m4_ref[0]) + m3_ref[0]
    for p, part in enumerate(_pack_parts(h2)):
        h2_ref[p, 0] = part
    lt_ref[...] = _dot_nt(wr_ref[...], h2, HIGHEST)


def _merge(mla_o, hg_f, hg_b, gd_f, gd_b, z_main, x_all, mod_l, hg_w, gdn_w, wb, wo, ln_g, ln_b, wr_t):
    b, lt, d = x_all.shape
    n = lt // TOK_TILE
    tok = lambda w, k=0: pl.BlockSpec((1, TOK_TILE, w), lambda i, t: (i, t, k))
    full = lambda a: pl.BlockSpec(a.shape, lambda i, t: (0,) * a.ndim)
    return pl.pallas_call(
        _merge_kernel,
        grid=(b, n),
        in_specs=[tok(BRANCH_W), tok(REC_W), tok(REC_W), tok(REC_W), tok(REC_W),
                  tok(N_BRANCH * D_MODEL, 0), tok(REC_W, ZM_HGATE), tok(REC_W, ZM_GGATE),
                  tok(d), _mod_spec(b, 2), _mod_spec(b, 3), _mod_spec(b, 4),
                  full(hg_w), full(gdn_w), full(wb), full(wo), full(ln_g), full(ln_b), full(wr_t)],
        out_specs=[tok(d), pl.BlockSpec((ROW_PARTS, 1, TOK_TILE, SC_ROW), lambda i, t: (0, i, t, 0)),
                   pl.BlockSpec((N_EXPERTS, TOK_TILE), lambda i, t: (0, i * n + t))],
        out_shape=[jax.ShapeDtypeStruct((b, lt, d), F32),
                   jax.ShapeDtypeStruct((ROW_PARTS, b, lt, SC_ROW), jnp.int32),
                   jax.ShapeDtypeStruct((N_EXPERTS, b * lt), F32)],
        compiler_params=_cp(("arbitrary", "arbitrary")),
        name="merge",
    )(mla_o, hg_f, hg_b, gd_f, gd_b, z_main, z_main, z_main, x_all, mod_l, mod_l, mod_l,
      hg_w, gdn_w, wb, wo, ln_g, ln_b, wr_t)


def _first_max(x, idx, axes):
    m = x
    for ax in axes:
        m = jnp.max(m, axis=ax, keepdims=True)
    first = jnp.where(x == m, idx, jnp.int32(2 ** 30))
    for ax in axes:
        first = jnp.min(first, axis=ax, keepdims=True)
    return m, first


def _route_kernel(lt_ref, bias_ref, e_ref, r_ref, w_ref, cnt_ref, run_ref):
    @pl.when(pl.program_id(0) == 0)
    def _():
        run_ref[...] = jnp.zeros(run_ref.shape, F32)

    n_tok = lt_ref.shape[1]
    per = N_EXPERTS // N_GROUPS
    scores = _sigmoid(lt_ref[...]).reshape(N_GROUPS, per, n_tok)
    sel = scores + bias_ref[...]
    ig = lax.broadcasted_iota(jnp.int32, sel.shape, 0)
    ij = lax.broadcasted_iota(jnp.int32, sel.shape, 1)
    top1, a1 = _first_max(sel, ij, (1,))
    top2 = jnp.max(jnp.where(ij == a1, -jnp.inf, sel), axis=1, keepdims=True)
    grp = top1 + top2
    igg = lax.broadcasted_iota(jnp.int32, grp.shape, 0)
    gsel = jnp.zeros(grp.shape, F32)
    for _ in range(TOPK_GROUPS):
        _, a = _first_max(grp, igg, (0,))
        hit = igg == a
        gsel = jnp.where(hit, 1.0, gsel)
        grp = jnp.where(hit, -jnp.inf, grp)
    cur = jnp.where(gsel > 0.5, sel, -jnp.inf)
    ie = ig * per + ij
    esel = jnp.zeros(cur.shape, F32)
    for _ in range(TOP_K):
        _, a = _first_max(cur, ie, (1, 0))
        hit = ie == a
        esel = jnp.where(hit, 1.0, esel)
        cur = jnp.where(hit, -jnp.inf, cur)
    w = scores * esel
    tot = jnp.sum(jnp.sum(w, axis=1, keepdims=True), axis=0, keepdims=True)
    w = (w / tot * ROUTED_SCALE).reshape(N_EXPERTS, n_tok)
    m = esel.reshape(N_EXPERTS, n_tok)
    mb = m.astype(BF16)
    ti = lax.broadcasted_iota(jnp.int32, (n_tok, n_tok), 0)
    tj = lax.broadcasted_iota(jnp.int32, (n_tok, n_tok), 1)
    upto = _dot(mb, jnp.where(ti <= tj, 1.0, 0.0).astype(BF16))
    ei = lax.broadcasted_iota(jnp.int32, (N_EXPERTS, N_EXPERTS), 0)
    ej = lax.broadcasted_iota(jnp.int32, (N_EXPERTS, N_EXPERTS), 1)
    lower = _dot(jnp.where(ej < ei, 1.0, 0.0).astype(BF16), mb)
    run = run_ref[:, 0:1]
    rank = run + upto - 1.0
    run_ref[...] = jnp.broadcast_to(run + upto[:, n_tok - 1:n_tok], run_ref.shape)
    cnt_ref[...] = run_ref[...]
    eid = lax.broadcasted_iota(jnp.int32, m.shape, 0).astype(F32)
    rows_e, rows_r, rows_w = [], [], []
    for k in range(TOP_K):
        pick = jnp.where(lower == float(k), m, 0.0)
        rows_e.append(jnp.sum(pick * eid, axis=0, keepdims=True))
        rows_r.append(jnp.sum(pick * rank, axis=0, keepdims=True))
        rows_w.append(jnp.sum(pick * w, axis=0, keepdims=True))
    e_ref[...] = jnp.concatenate(rows_e, axis=0).astype(jnp.int32)
    r_ref[...] = jnp.concatenate(rows_r, axis=0).astype(jnp.int32)
    w_ref[...] = jnp.concatenate(rows_w, axis=0).T


ROUTE_TILE = 512


def _route(logits_t, bias):
    n_tok = logits_t.shape[1]
    tt = ROUTE_TILE
    return pl.pallas_call(
        _route_kernel,
        grid=(n_tok // tt,),
        in_specs=[pl.BlockSpec((N_EXPERTS, tt), lambda i: (0, i)),
                  pl.BlockSpec((N_GROUPS, N_EXPERTS // N_GROUPS, 1), lambda i: (0, 0, 0))],
        out_specs=[pl.BlockSpec((TOP_K, tt), lambda i: (0, i)),
                   pl.BlockSpec((TOP_K, tt), lambda i: (0, i)),
                   pl.BlockSpec((tt, TOP_K), lambda i: (i, 0)),
                   pl.BlockSpec((N_EXPERTS, 128), lambda i: (0, 0))],
        out_shape=[jax.ShapeDtypeStruct((TOP_K, n_tok), jnp.int32),
                   jax.ShapeDtypeStruct((TOP_K, n_tok), jnp.int32),
                   jax.ShapeDtypeStruct((n_tok, TOP_K), F32),
                   jax.ShapeDtypeStruct((N_EXPERTS, 128), F32)],
        scratch_shapes=[pltpu.VMEM((N_EXPERTS, 128), F32)],
        compiler_params=_cp(("arbitrary",)),
        name="route",
    )(logits_t, bias)


def _slot_kernel(start_ref, e_ref, r_ref, p_ref, *, n_rows):
    e = e_ref[...]
    pos = r_ref[...]
    for j in range(N_EXPERTS):
        pos = pos + jnp.where(e == j, start_ref[j], 0)
    for p in range(ROW_PARTS):
        p_ref[p] = pos + p * n_rows


def _slots(starts, e_k, r_k, n_rows):
    n_tok = e_k.shape[1]
    tt = ROUTE_TILE
    spec = pl.BlockSpec((TOP_K, tt), lambda i, s: (0, i))
    return pl.pallas_call(
        functools.partial(_slot_kernel, n_rows=n_rows),
        grid_spec=pltpu.PrefetchScalarGridSpec(
            num_scalar_prefetch=1, grid=(n_tok // tt,), in_specs=[spec, spec],
            out_specs=pl.BlockSpec((ROW_PARTS, TOP_K, tt), lambda i, s: (0, 0, i))),
        out_shape=jax.ShapeDtypeStruct((ROW_PARTS, TOP_K, n_tok), jnp.int32),
        compiler_params=_cp(("arbitrary",)),
        name="moe_slots",
    )(starts, e_k, r_k)


def _sc_mesh():
    return plsc.VectorSubcoreMesh(core_axis_name="c", subcore_axis_name="s")


def _sc_scatter(x, idx, n_out):
    n = idx.shape[0]
    nbt = x.shape[0] // ROW_PARTS // SC_WINDOW
    reps = n // x.shape[0]

    @functools.partial(pl.kernel, out_type=jax.ShapeDtypeStruct((n_out, SC_ROW), x.dtype), mesh=_sc_mesh())
    def k(x_hbm, i_hbm, o_hbm):
        def body(x_vmem, i_vmem):
            pltpu.sync_copy(x_vmem, o_hbm.at[i_vmem.at[0]])

        pltpu.emit_pipeline(
            body,
            grid=(n // SC_WINDOW,),
            in_specs=[pl.BlockSpec((SC_WINDOW, SC_ROW), lambda i: (i // (reps * nbt) * nbt + i % nbt, 0)),
                      pl.BlockSpec((1, SC_WINDOW), lambda i: (0, i))],
            out_specs=[],
            core_axis_name=("c", "s"),
            dimension_semantics=(pltpu.PARALLEL,),
        )(x_hbm, i_hbm)

    return k(x, idx.reshape(1, n))


def _sc_gather(table, idx):
    n = idx.shape[0]

    @functools.partial(pl.kernel, out_type=jax.ShapeDtypeStruct((n, SC_ROW), table.dtype), mesh=_sc_mesh())
    def k(x_hbm, i_hbm, o_hbm):
        def body(i_vmem, o_vmem):
            pltpu.sync_copy(x_hbm.at[i_vmem.at[0]], o_vmem)

        pltpu.emit_pipeline(
            body,
            grid=(n // SC_WINDOW,),
            in_specs=[pl.BlockSpec((1, SC_WINDOW), lambda i: (0, i))],
            out_specs=[pl.BlockSpec((SC_WINDOW, SC_ROW), lambda i: (i, 0))],
            core_axis_name=("c", "s"),
            dimension_semantics=(pltpu.PARALLEL,),
        )(i_hbm, o_hbm)

    return k(table, idx.reshape(1, n))


MOE_TM = 512


def _expert_kernel(te_ref, tr_ref, x_ref, wgu_ref, wd_ref, y_ref, wgu_bf, wd_bf):
    i = pl.program_id(0)
    valid = tr_ref[i]

    @pl.when(jnp.logical_or(i == 0, te_ref[i] != te_ref[jnp.maximum(i - 1, 0)]))
    def _():
        wgu_bf[...] = wgu_ref[0].astype(BF16)
        wd_bf[...] = wd_ref[0].astype(BF16)

    @pl.when(valid > 0)
    def _():
        live = lax.broadcasted_iota(jnp.int32, x_ref.shape[1:], 0) < valid
        x = _unpack_parts([jnp.where(live, x_ref[p], 0) for p in range(ROW_PARTS)])
        gu = _dot(x, wgu_bf[...])
        act = _silu(gu[:, :EXPERT_FF]) * gu[:, EXPERT_FF:]
        for p, part in enumerate(_pack_parts(_dot(act.astype(BF16), wd_bf[...]))):
            y_ref[p] = part

    @pl.when(valid == 0)
    def _():
        y_ref[...] = jnp.zeros(y_ref.shape, y_ref.dtype)


def _experts(tile_expert, tile_rows, xs, wgu, wd, layer):
    _, n_rows, _ = xs.shape
    d = wgu.shape[2]
    rows = pl.BlockSpec((ROW_PARTS, MOE_TM, SC_ROW), lambda i, te, tr: (0, i, 0))
    return pl.pallas_call(
        _expert_kernel,
        grid_spec=pltpu.PrefetchScalarGridSpec(
            num_scalar_prefetch=2,
            grid=(n_rows // MOE_TM,),
            in_specs=[rows,
                      pl.BlockSpec((None, 1, d, 2 * EXPERT_FF), lambda i, te, tr: (layer, te[i], 0, 0)),
                      pl.BlockSpec((None, 1, EXPERT_FF, d), lambda i, te, tr: (layer, te[i], 0, 0))],
            out_specs=rows,
            scratch_shapes=[pltpu.VMEM((d, 2 * EXPERT_FF), BF16), pltpu.VMEM((EXPERT_FF, d), BF16)]),
        out_shape=jax.ShapeDtypeStruct(xs.shape, jnp.int32),
        compiler_params=_cp(("arbitrary",)),
        name="moe_experts",
    )(tile_expert, tile_rows, xs, wgu, wd)


def _combine_kernel(g_ref, w_ref, h_ref, x_ref, m5_ref, wsgu_ref, wsd_ref, lg_ref, lb_ref, *rest):
    h = _unpack_parts([h_ref[p, 0] for p in range(ROW_PARTS)])
    gu = _dot(h, wsgu_ref[...])
    act = _silu(gu[:, :SHARED_FF]) * gu[:, SHARED_FF:]
    f = _dot(act.astype(BF16), wsd_ref[...])
    w = w_ref[...]
    for k in range(TOP_K):
        f = f + w[:, k:k + 1] * _unpack_parts([g_ref[p, k] for p in range(ROW_PARTS)]).astype(F32)
    x_new = _layer_norm(DEEPNORM_ALPHA * x_ref[0] + m5_ref[0] * f, lg_ref[...], lb_ref[...])
    if len(rest) == 1:
        rest[0][0] = x_new
    else:
        sh_ref, sc_ref, o_ref, hn_ref = rest
        o_ref[0] = x_new
        hn_ref[0] = (x_new * (1.0 + sc_ref[0]) + sh_ref[0]).astype(hn_ref.dtype)


def _combine(g, w_k, h2p, x1, mod_l, wsgu, wsd, ln_g, ln_b, mod_next):
    b, lt, d = x1.shape
    n = lt // TOK_TILE
    tok = pl.BlockSpec((1, TOK_TILE, d), lambda i, t: (i, t, 0))
    full = lambda a: pl.BlockSpec(a.shape, lambda i, t: (0,) * a.ndim)
    in_specs = [pl.BlockSpec((ROW_PARTS, TOP_K, TOK_TILE, SC_ROW), lambda i, t: (0, 0, i * n + t, 0)),
                pl.BlockSpec((TOK_TILE, TOP_K), lambda i, t: (i * n + t, 0)),
                pl.BlockSpec((ROW_PARTS, 1, TOK_TILE, SC_ROW), lambda i, t: (0, i, t, 0)),
                tok, _mod_spec(b, 5), full(wsgu), full(wsd), full(ln_g), full(ln_b)]
    args = [g, w_k, h2p, x1, mod_l, wsgu, wsd, ln_g, ln_b]
    out_specs, out_shape = tok, jax.ShapeDtypeStruct((b, lt, d), F32)
    if mod_next is not None:
        in_specs += [_mod_spec(b, 0), _mod_spec(b, 1)]
        args += [mod_next, mod_next]
        out_specs, out_shape = [tok, tok], [out_shape, jax.ShapeDtypeStruct((b, lt, d), BF16)]
    return pl.pallas_call(
        _combine_kernel,
        grid=(b, n),
        in_specs=in_specs,
        out_specs=out_specs,
        out_shape=out_shape,
        compiler_params=_cp(("arbitrary", "arbitrary")),
        name="moe_combine",
    )(*args)


def _moe_sparse(h2p, logits_t, bias, wgu, wd, layer, wsgu, wsd, x1, mod_l, ln_g, ln_b, mod_next):
    n_tok = h2p.shape[1] * h2p.shape[2]
    e_k, r_k, w_k, counts = _route(logits_t, bias)
    counts = counts[:, 0].astype(jnp.int32)
    padded = (counts + MOE_TM - 1) // MOE_TM * MOE_TM
    ends = jnp.cumsum(padded)
    starts = ends - padded
    n_rows = (n_tok * TOP_K + N_EXPERTS * (MOE_TM - 1)) // MOE_TM * MOE_TM
    tile0 = jnp.arange(n_rows // MOE_TM, dtype=jnp.int32) * MOE_TM
    tile_expert = jnp.minimum(jnp.sum(tile0[:, None] >= ends[None, :], axis=1), N_EXPERTS - 1).astype(jnp.int32)
    tile_rows = jnp.clip(counts[tile_expert] - (tile0 - starts[tile_expert]), 0, MOE_TM).astype(jnp.int32)
    idx = _slots(starts.astype(jnp.int32), e_k, r_k, n_rows).reshape(-1)
    xs = _sc_scatter(h2p.reshape(ROW_PARTS * n_tok, SC_ROW), idx, ROW_PARTS * n_rows)
    ys = _experts(tile_expert, tile_rows, xs.reshape(ROW_PARTS, n_rows, SC_ROW), wgu, wd, layer)
    g = _sc_gather(ys.reshape(ROW_PARTS * n_rows, SC_ROW), idx).reshape(ROW_PARTS, TOP_K, n_tok, SC_ROW)
    return _combine(g, w_k, h2p, x1, mod_l, wsgu, wsd, ln_g, ln_b, mod_next)


def _rope_tables(n_ctx, n_lat):
    rows = n_lat // GRID_W
    row = jnp.broadcast_to(jnp.arange(rows, dtype=F32)[:, None], (rows, GRID_W)).reshape(-1)
    col = jnp.broadcast_to(jnp.arange(GRID_W, dtype=F32)[None, :], (rows, GRID_W)).reshape(-1)
    inv = ROPE_BASE ** (-jnp.arange(ROPE_FREQS, dtype=F32) / ROPE_FREQS)
    ang = jnp.stack([row[:, None] * inv, col[:, None] * inv], axis=1)
    cos = jnp.cos(ang)[:, :, None, :]
    sin = jnp.sin(ang)[:, :, None, :]
    cos32 = jnp.broadcast_to(cos, (n_lat, 2, 2, ROPE_FREQS)).reshape(n_lat, MLA_ROPE)
    sin32 = jnp.concatenate([-sin, sin], axis=2).reshape(n_lat, MLA_ROPE)
    cos32 = jnp.concatenate([jnp.ones((n_ctx, MLA_ROPE), F32), cos32], axis=0)
    sin32 = jnp.concatenate([jnp.zeros((n_ctx, MLA_ROPE), F32), sin32], axis=0)
    lt = n_ctx + n_lat
    ca = jnp.concatenate([jnp.ones((lt, MLA_NOPE), F32), cos32, jnp.zeros((lt, 32), F32)], axis=1)
    sb = jnp.concatenate([jnp.zeros((lt, MLA_NOPE), F32), sin32, jnp.zeros((lt, 32), F32)], axis=1)
    return ca, sb


def _rope_partner():
    idx = np.arange(MLA_ROPE).reshape(2, 2, ROPE_FREQS)
    return idx[:, ::-1, :].reshape(-1)


IN_SIZES = (MLA_Q_LORA, MLA_KV_LORA + MLA_ROPE, REC_W, REC_W, REC_W, REC_W, REC_W,
            3 * REC_W, REC_W, 2 * REC_HEADS, 2 * REC_HEADS, N_BRANCH * D_MODEL)
IN_OFFS = tuple(int(o) for o in np.cumsum((0,) + IN_SIZES))
REPACK_ROWS = 128


def _repack_kernel(w_ref, main_ref, f_ref, mla_ref, gdn_ref):
    rows = w_ref.shape[1]
    seg = lambda i: w_ref[0, :, IN_OFFS[i]:IN_OFFS[i + 1]]
    zeros = lambda n: jnp.zeros((rows, n), F32)
    k_rope = w_ref[0, :, IN_OFFS[1] + MLA_KV_LORA:IN_OFFS[2]]
    h = ROPE_FREQS
    k_partner = jnp.concatenate([k_rope[:, h:2 * h], k_rope[:, :h], k_rope[:, 3 * h:], k_rope[:, 2 * h:3 * h]], axis=1)
    main_ref[0] = jnp.concatenate([seg(11), seg(2), seg(3), seg(6), seg(8)], axis=1).astype(BF16)
    f_ref[0] = jnp.concatenate([seg(4), seg(5)], axis=1).astype(BF16)
    mla_ref[0] = jnp.concatenate([seg(0), w_ref[0, :, IN_OFFS[1]:IN_OFFS[1] + MLA_KV_LORA],
                                  zeros(MLA_NOPE), k_rope, zeros(32),
                                  zeros(MLA_NOPE), k_partner, zeros(32)], axis=1).astype(BF16)
    gdn_ref[0] = jnp.concatenate([seg(7), seg(9), seg(10), zeros(256 - 4 * REC_HEADS)], axis=1).astype(BF16)


def _repack_w_in(w_in):
    depth, d, width = w_in.shape
    widths = (N_BRANCH * D_MODEL + 4 * REC_W, 2 * REC_W, ZA_WIDTH, ZG_WIDTH)
    return pl.pallas_call(
        _repack_kernel,
        grid=(depth, d // REPACK_ROWS),
        in_specs=[pl.BlockSpec((1, REPACK_ROWS, width), lambda l, r: (l, r, 0))],
        out_specs=[pl.BlockSpec((1, REPACK_ROWS, w), lambda l, r: (l, r, 0)) for w in widths],
        out_shape=[jax.ShapeDtypeStruct((depth, d, w), BF16) for w in widths],
        compiler_params=_cp(("arbitrary", "arbitrary")),
        name="repack_w_in",
    )(w_in)


def _pack_mla(w_q_b, w_kv_b):
    partner = _rope_partner()
    r = w_q_b.shape[0]
    qb = w_q_b.reshape(r, MLA_HEADS, MLA_NOPE + MLA_ROPE)
    zq = lambda n: jnp.zeros((r, MLA_HEADS, n), w_q_b.dtype)
    wq1 = jnp.concatenate([qb, zq(32)], axis=2).reshape(r, MLA_HEADS * 128)
    wq2 = jnp.concatenate([zq(MLA_NOPE), qb[:, :, MLA_NOPE:][:, :, partner], zq(32)], axis=2).reshape(r, MLA_HEADS * 128)
    rk = w_kv_b.shape[0]
    kvb = w_kv_b.reshape(rk, MLA_HEADS, MLA_NOPE + MLA_V)
    wk = jnp.concatenate([kvb[:, :, :MLA_NOPE], jnp.zeros((rk, MLA_HEADS, 64), w_kv_b.dtype)], axis=2)
    wk = wk.reshape(rk, MLA_HEADS * 128)
    wv = jnp.concatenate([kvb[:, :, MLA_NOPE:], jnp.zeros((rk, MLA_HEADS, 128 - MLA_V), w_kv_b.dtype)], axis=2)
    wv = wv.reshape(rk, MLA_HEADS * 128)
    bf = lambda a: a.astype(BF16)
    return bf(wq1), bf(wq2), bf(wk), bf(wv)


def kernel(x, c, ctx, c_ctx, w_mod, b_mod, w_in, q_a_norm, w_q_b, kv_a_norm, w_kv_b, hg_lb_logits, hg_norm,
           gdn_conv, gdn_a_log, gdn_dt_bias, gdn_norm, w_branch, w_out, ln1_g, ln1_b, ln2_g, ln2_b,
           w_router, router_bias, w_gu, w_down, w_sh_gu, w_sh_down):
    batch, n_lat, d = x.shape
    n_ctx = ctx.shape[1]
    assert n_ctx == TOK_TILE and n_lat % TOK_TILE == 0 and batch < MOD_ROWS and d == D_MODEL
    lt = n_ctx + n_lat

    c_all = jnp.zeros((MOD_ROWS, d), F32).at[:batch].set(c).at[batch].set(c_ctx)
    mod = _mod_all(c_all, w_mod, b_mod).reshape(DEPTH, MOD_ROWS, 1, 6 * d)
    ca, sb = _rope_tables(n_ctx, n_lat)
    lb_soft = jax.nn.softmax(hg_lb_logits.astype(F32), axis=0)
    lower = (jnp.cumsum(lb_soft, axis=0) - lb_soft[0]).reshape(DEPTH, 2, 1, REC_W)
    row = lambda a: a.reshape(1, -1)

    w_main, w_f, w_mla, w_gdn = _repack_w_in(w_in)
    x_all = jnp.concatenate([ctx, x], axis=1)
    h = _modulate(x_all, mod[0])
    for l in range(DEPTH):
        wq1, wq2, wk, wv = _pack_mla(w_q_b[l], w_kv_b[l])
        mod_l = mod[l]
        z_main = _proj(h, w_main, l, 2 * REC_W, BF16)
        z_f = _proj(h, w_f, l, REC_W)
        z_mla = _proj(h, w_mla, l, ZA_WIDTH)
        z_gdn = _proj(h, w_gdn, l, ZG_WIDTH // 2)

        q, k, v = _mla_prep(z_mla, row(q_a_norm[l]), row(kv_a_norm[l]), wq1, wq2, wk, wv, ca, sb)
        mla_o = _attention(q, k, v, n_ctx)

        hg_f, hg_b = _hgrn2(z_main, z_f, lower[l])

        qkv = _gdn_prep(z_gdn, gdn_conv[l], n_ctx)
        par = jnp.zeros((8, 128), F32)
        par = par.at[0, :2 * REC_HEADS].set(gdn_a_log[l].reshape(-1))
        par = par.at[1, :2 * REC_HEADS].set(gdn_dt_bias[l].reshape(-1))
        gd_f, gd_b = _gdn(qkv, z_gdn, par)

        x1, h2, logits_t = _merge(mla_o, hg_f, hg_b, gd_f, gd_b, z_main, x_all, mod_l,
                                  row(hg_norm[l]), row(gdn_norm[l]), w_branch[l].astype(BF16),
                                  w_out[l].astype(BF16), row(ln1_g[l]), row(ln1_b[l]), w_router[l].T)
        out = _moe_sparse(h2, logits_t,
                          router_bias[l].reshape(N_GROUPS, N_EXPERTS // N_GROUPS, 1),
                          w_gu, w_down, l,
                          w_sh_gu[l].astype(BF16), w_sh_down[l].astype(BF16),
                          x1, mod_l, row(ln2_g[l]), row(ln2_b[l]), mod[l + 1] if l + 1 < DEPTH else None)
        x_all, h = out if l + 1 < DEPTH else (out, None)
    return x_all[:, n_ctx:, :]
```

```python
import functools
import math

import numpy as np
import jax
import jax.numpy as jnp
from jax import lax
from jax.experimental import pallas as pl
from jax.experimental.pallas import tpu as pltpu
from jax.experimental.pallas import tpu_sc as plsc

F32 = jnp.float32
BF16 = jnp.bfloat16
HIGHEST = lax.Precision.HIGHEST
LOG2E = math.log2(math.e)

D_MODEL = 1024
DEPTH = 4
GRID_W = 64
MLA_HEADS = 8
MLA_Q_LORA = 384
MLA_KV_LORA = 256
MLA_NOPE = 64
MLA_ROPE = 32
MLA_V = 64
MLA_SCALE = (MLA_NOPE + MLA_ROPE) ** -0.5
ROPE_BASE = 10000.0
ROPE_FREQS = MLA_ROPE // 4
REC_HEADS = 4
REC_D = 128
REC_W = REC_HEADS * REC_D
CONV_K = 5
N_BRANCH = 3
BRANCH_W = 512
N_EXPERTS = 64
TOP_K = 8
N_GROUPS = 8
TOPK_GROUPS = 4
EXPERT_FF = 256
SHARED_FF = 256
ROUTED_SCALE = 2.5
DEEPNORM_ALPHA = (2 * DEPTH) ** 0.25
LN_EPS = 1e-6
RMS_EPS = 1e-6

TOK_TILE = 256
CHUNK = 64
MOD_ROWS = 16
VMEM_LIMIT = 56 * 1024 * 1024

ZM_Q, ZM_I, ZM_HGATE, ZM_GGATE = (3 * D_MODEL // REC_W + k for k in range(4))
ZA_WIDTH = MLA_Q_LORA + MLA_KV_LORA + 256
ZG_CONV = 3 * REC_W
ZG_WIDTH = ZG_CONV + 256


def _cp(sem, vmem=VMEM_LIMIT):
    return pltpu.CompilerParams(dimension_semantics=sem, vmem_limit_bytes=vmem)


def _dot(a, b, precision=None):
    return jnp.dot(a, b, preferred_element_type=F32, precision=precision)


def _dot_nt(a, b, precision=None):
    return lax.dot_general(a, b, (((1,), (1,)), ((), ())), preferred_element_type=F32, precision=precision)


def _sigmoid(x):
    return 1.0 / (1.0 + jnp.exp(-x))


def _silu(x):
    return x * _sigmoid(x)


ROW_PARTS = 2
SC_ROW = D_MODEL // (2 * ROW_PARTS)
SC_WINDOW = 128


def _pack_parts(x):
    q = x.shape[1] // (2 * ROW_PARTS)
    bits = lambda a: lax.bitcast_convert_type(a.astype(BF16).astype(F32), jnp.uint32)
    parts = []
    for p in range(ROW_PARTS):
        hi = bits(x[:, p * q:(p + 1) * q])
        lo = bits(x[:, (ROW_PARTS + p) * q:(ROW_PARTS + p + 1) * q])
        parts.append(lax.bitcast_convert_type(hi | lax.shift_right_logical(lo, jnp.uint32(16)), jnp.int32))
    return parts


def _unpack_parts(parts):
    his, los = [], []
    for p in parts:
        u = lax.bitcast_convert_type(p, jnp.uint32)
        his.append(lax.bitcast_convert_type(u & jnp.uint32(0xFFFF0000), F32).astype(BF16))
        los.append(lax.bitcast_convert_type(lax.shift_left(u, jnp.uint32(16)), F32).astype(BF16))
    return jnp.concatenate(his + los, axis=1)


def _mod_kernel(c_ref, w_ref, b_ref, o_ref):
    s = _silu(c_ref[...])
    o_ref[0] = _dot(s, w_ref[0], HIGHEST) + b_ref[0]


def _mod_all(c_all, w_mod, b_mod):
    tn = 1024
    n = w_mod.shape[-1]
    return pl.pallas_call(
        _mod_kernel,
        grid=(DEPTH, n // tn),
        in_specs=[
            pl.BlockSpec((MOD_ROWS, D_MODEL), lambda l, j: (0, 0)),
            pl.BlockSpec((1, D_MODEL, tn), lambda l, j: (l, 0, j)),
            pl.BlockSpec((1, 1, tn), lambda l, j: (l, 0, j)),
        ],
        out_specs=pl.BlockSpec((1, MOD_ROWS, tn), lambda l, j: (l, 0, j)),
        out_shape=jax.ShapeDtypeStruct((DEPTH, MOD_ROWS, n), F32),
        compiler_params=_cp(("arbitrary", "arbitrary")),
        name="mod_all",
    )(c_all, w_mod, b_mod.reshape(DEPTH, 1, n))


def _mod_spec(batch, k):
    return pl.BlockSpec((1, 1, D_MODEL), lambda b, t: (jnp.where(t == 0, batch, b), 0, k))


def _modulate_kernel(x_ref, sh_ref, sc_ref, o_ref):
    o_ref[0] = (x_ref[0] * (1.0 + sc_ref[0]) + sh_ref[0]).astype(o_ref.dtype)


def _modulate(x_all, mod_l):
    b, lt, d = x_all.shape
    return pl.pallas_call(
        _modulate_kernel,
        grid=(b, lt // TOK_TILE),
        in_specs=[
            pl.BlockSpec((1, TOK_TILE, d), lambda i, t: (i, t, 0)),
            _mod_spec(b, 0),
            _mod_spec(b, 1),
        ],
        out_specs=pl.BlockSpec((1, TOK_TILE, d), lambda i, t: (i, t, 0)),
        out_shape=jax.ShapeDtypeStruct((b, lt, d), BF16),
        compiler_params=_cp(("arbitrary", "arbitrary")),
        name="modulate",
    )(x_all, mod_l, mod_l)


PROJ_ROWS = 256


def _proj_kernel(h_ref, w_ref, o_ref, *, rows):
    def body(r, carry):
        sl = pl.ds(pl.multiple_of(r * rows, rows), rows)
        o_ref[0, sl, :] = _dot(h_ref[0, sl, :], w_ref[...]).astype(o_ref.dtype)
        return carry

    lax.fori_loop(0, h_ref.shape[1] // rows, body, 0)


def _proj(h, w, layer, tn, dtype=F32):
    b, lt, d = h.shape
    n = w.shape[2]
    return pl.pallas_call(
        functools.partial(_proj_kernel, rows=PROJ_ROWS),
        grid=(b, n // tn),
        in_specs=[
            pl.BlockSpec((1, lt, d), lambda i, j: (i, 0, 0)),
            pl.BlockSpec((None, d, tn), lambda i, j: (layer, 0, j)),
        ],
        out_specs=pl.BlockSpec((1, lt, tn), lambda i, j: (i, 0, j)),
        out_shape=jax.ShapeDtypeStruct((b, lt, n), dtype),
        compiler_params=_cp(("arbitrary", "arbitrary")),
        name="proj",
    )(h, w)


def _rms(x, g):
    return x * lax.rsqrt(jnp.mean(x * x, axis=-1, keepdims=True) + RMS_EPS) * g


def _mla_prep_kernel(z_ref, qg_ref, kg_ref, wq1_ref, wq2_ref, wk_ref, wv_ref, ca_ref, sb_ref,
                     q_ref, k_ref, v_ref):
    z = z_ref[0]
    ca = ca_ref[...]
    sb = sb_ref[...]
    qn = _rms(z[:, :MLA_Q_LORA], qg_ref[...]).astype(BF16)
    qa = _dot(qn, wq1_ref[...])
    qb = _dot(qn, wq2_ref[...])
    kvn = _rms(z[:, MLA_Q_LORA:MLA_Q_LORA + MLA_KV_LORA], kg_ref[...]).astype(BF16)
    kn = _dot(kvn, wk_ref[...])
    v = _dot(kvn, wv_ref[...])
    lane = lax.broadcasted_iota(jnp.int32, v.shape, 1)
    v_ref[0] = jnp.where((lane & 127) == MLA_V, 1.0, v).astype(v_ref.dtype)
    o = MLA_Q_LORA + MLA_KV_LORA
    kr = z[:, o:o + 128] * ca + z[:, o + 128:o + 256] * sb
    for h in range(MLA_HEADS):
        sl = slice(h * 128, (h + 1) * 128)
        q_ref[0, :, sl] = ((qa[:, sl] * ca + qb[:, sl] * sb) * (MLA_SCALE * LOG2E)).astype(q_ref.dtype)
        k_ref[0, :, sl] = (kn[:, sl] + kr).astype(k_ref.dtype)


def _mla_prep(z_mla, qg, kg, wq1, wq2, wk, wv, ca, sb):
    b, lt, _ = z_mla.shape
    full = lambda a: pl.BlockSpec(a.shape, lambda i, t: (0,) * a.ndim)
    tok = lambda w: pl.BlockSpec((1, TOK_TILE, w), lambda i, t: (i, t, 0))
    return pl.pallas_call(
        _mla_prep_kernel,
        grid=(b, lt // TOK_TILE),
        in_specs=[tok(ZA_WIDTH), full(qg), full(kg), full(wq1), full(wq2), full(wk), full(wv),
                  pl.BlockSpec((TOK_TILE, 128), lambda i, t: (t, 0)),
                  pl.BlockSpec((TOK_TILE, 128), lambda i, t: (t, 0))],
        out_specs=[tok(MLA_HEADS * 128)] * 3,
        out_shape=[jax.ShapeDtypeStruct((b, lt, MLA_HEADS * 128), BF16)] * 3,
        compiler_params=_cp(("arbitrary", "arbitrary")),
        name="mla_prep",
    )(z_mla, qg, kg, wq1, wq2, wk, wv, ca, sb)


ATTN_KEY_CHUNK = 768


def _attn_kernel(q_ref, k_ref, v_ref, o_ref, *, n_ctx):
    def attend(nk):
        kc = min(nk, ATTN_KEY_CHUNK)
        cols = [slice(h * 128, (h + 1) * 128) for h in range(2)]
        s = [[_dot_nt(q_ref[0, :, cols[h]], k_ref[0, c:c + kc, cols[h]]) for c in range(0, nk, kc)]
             for h in range(2)]
        m = [functools.reduce(jnp.maximum, [jnp.max(sc, axis=1, keepdims=True) for sc in s[h]]) for h in range(2)]
        o = [jnp.zeros((TOK_TILE, 128), F32) for _ in range(2)]
        for j, c in enumerate(range(0, nk, kc)):
            for h in range(2):
                p = jnp.exp2(s[h][j] - m[h]).astype(BF16)
                o[h] = o[h] + _dot(p, v_ref[0, c:c + kc, cols[h]])
        outs = [o[h][:, :MLA_V] / o[h][:, MLA_V:MLA_V + 1] for h in range(2)]
        o_ref[0] = jnp.concatenate(outs, axis=1).astype(o_ref.dtype)

    @pl.when(pl.program_id(2) == 0)
    def _():
        attend(n_ctx)

    @pl.when(pl.program_id(2) > 0)
    def _():
        attend(k_ref.shape[1])


def _attention(q, k, v, n_ctx):
    b, lt, _ = q.shape
    return pl.pallas_call(
        functools.partial(_attn_kernel, n_ctx=n_ctx),
        grid=(b, MLA_HEADS // 2, lt // TOK_TILE),
        in_specs=[
            pl.BlockSpec((1, TOK_TILE, 256), lambda i, h, t: (i, t, h)),
            pl.BlockSpec((1, lt, 256), lambda i, h, t: (i, 0, h)),
            pl.BlockSpec((1, lt, 256), lambda i, h, t: (i, 0, h)),
        ],
        out_specs=pl.BlockSpec((1, TOK_TILE, 2 * MLA_V), lambda i, h, t: (i, t, h)),
        out_shape=jax.ShapeDtypeStruct((b, lt, MLA_HEADS * MLA_V), BF16),
        compiler_params=_cp(("arbitrary", "arbitrary", "arbitrary")),
        name="mla_attention",
    )(q, k, v)


def _rev_tile(s, n):
    return jnp.where(s == 0, 0, n - s)


def _time_index(shape, dim, reverse):
    i = lax.broadcasted_iota(jnp.int32, shape, dim)
    return (shape[dim] - 1 - i) if reverse else i


def _blk(i, size):
    return lax.shift_right_logical(i, jnp.int32(int(math.log2(size))))


def _gla_chunks(items):
    c_len = items[0]["q"].shape[0]
    for it in items:
        g2 = it["g"] * LOG2E
        it["c"] = g2
        it["t"] = g2
        it["a"] = jnp.zeros((c_len, c_len), F32)
    hb = 1
    while hb < c_len:
        masks = {}
        for rev in {it["reverse"] for it in items}:
            ri = _time_index((c_len, c_len), 0, rev)
            ci = _time_index((c_len, c_len), 1, rev)
            pair = jnp.logical_and(_blk(ri, hb) == _blk(ci, hb) + 1, (_blk(ri, hb) & 1) == 1)
            odd = (_blk(_time_index((c_len, REC_D), 0, rev), hb) & 1) == 1
            masks[rev] = (pair, odd)
        for it in items:
            rev = it["reverse"]
            pair, odd = masks[rev]
            c, t = it["c"], it["t"]
            qt = (it["q"] * jnp.exp2(c)).astype(BF16)
            kt = (it["k"] * jnp.exp2(t - c)).astype(BF16)
            it["a"] = it["a"] + jnp.where(pair, _dot_nt(qt, kt), 0.0)
            t_lo = pltpu.roll(t, hb, 0)
            t_hi = pltpu.roll(t, c_len - hb, 0)
            prev, nxt = (t_hi, t_lo) if rev else (t_lo, t_hi)
            it["c"] = c + jnp.where(odd, prev, 0.0)
            it["t"] = t + jnp.where(odd, prev, nxt)
        hb *= 2
    outs = []
    for it in items:
        q, k, v, c, t = it["q"], it["k"], it["v"], it["c"], it["t"]
        st = it["st_ref"][...]
        qk = jnp.sum(q * k, axis=1, keepdims=True)
        outs.append(_dot_nt((q * jnp.exp2(c)).astype(BF16), st.astype(BF16))
                    + _dot(it["a"].astype(BF16), v.astype(BF16)) + qk * v)
        kd = (k * jnp.exp2(t - c)).astype(BF16)
        it["st_ref"][...] = st * jnp.exp2(t[0:1, :]) + _dot(v.T.astype(BF16), kd)
    return outs


def _hgrn2_kernel(qf_ref, vf_ref, ff_ref, qb_ref, vb_ref, fb_ref, lb_ref, of_ref, ob_ref, st_ref):
    @pl.when(pl.program_id(1) == 0)
    def _():
        st_ref[...] = jnp.zeros(st_ref.shape, F32)

    n_chunks = TOK_TILE // CHUNK

    def body(ci, carry):
        items, dests = [], []
        for d, (q_ref, v_ref, f_ref, o_ref) in enumerate(((qf_ref, vf_ref, ff_ref, of_ref),
                                                           (qb_ref, vb_ref, fb_ref, ob_ref))):
            cc = ci if d == 0 else n_chunks - 1 - ci
            rows = pl.ds(pl.multiple_of(cc * CHUNK, CHUNK), CHUNK)
            for h in range(REC_HEADS):
                cols = slice(h * REC_D, (h + 1) * REC_D)
                lb = lb_ref[d, :, cols]
                f = lb + (1.0 - lb) * _sigmoid(f_ref[0, rows, cols])
                items.append(dict(q=_silu(q_ref[0, rows, cols].astype(F32)), k=1.0 - f,
                                  v=v_ref[0, rows, cols].astype(F32),
                                  g=jnp.log(f), st_ref=st_ref.at[d, h], reverse=(d == 1)))
                dests.append((o_ref, rows, cols))
        for (o_ref, rows, cols), o in zip(dests, _gla_chunks(items)):
            o_ref[0, rows, cols] = o
        return carry

    lax.fori_loop(0, n_chunks, body, 0)


def _hgrn2(z_main, z_f, lb):
    b, lt, _ = z_main.shape
    n = lt // TOK_TILE
    fwd = lambda k: pl.BlockSpec((1, TOK_TILE, REC_W), lambda i, s: (i, s, k))
    bwd = lambda k: pl.BlockSpec((1, TOK_TILE, REC_W), lambda i, s: (i, _rev_tile(s, n), k))
    return pl.pallas_call(
        _hgrn2_kernel,
        grid=(b, n),
        in_specs=[fwd(ZM_Q), fwd(ZM_I), fwd(0), bwd(ZM_Q), bwd(ZM_I), bwd(1),
                  pl.BlockSpec((2, 1, REC_W), lambda i, s: (0, 0, 0))],
        out_specs=[pl.BlockSpec((1, TOK_TILE, REC_W), lambda i, s: (i, s, 0)),
                   pl.BlockSpec((1, TOK_TILE, REC_W), lambda i, s: (i, _rev_tile(s, n), 0))],
        out_shape=[jax.ShapeDtypeStruct((b, lt, REC_W), F32)] * 2,
        scratch_shapes=[pltpu.VMEM((2, REC_HEADS, REC_D, REC_D), F32)],
        compiler_params=_cp(("arbitrary", "arbitrary")),
        name="hgrn2_scan",
    )(z_main, z_main, z_f, z_main, z_main, z_f, lb)


CONV_GAP = 8


def _gdn_prep_kernel(x_ref, w_ref, o_ref, pad_ref, *, n_ctx):
    lt = x_ref.shape[1]
    segs = ((0, n_ctx, CONV_GAP), (n_ctx, lt, 2 * CONV_GAP))
    gap = jnp.zeros((CONV_GAP, REC_D), F32)
    pad_ref[0:CONV_GAP, :] = gap
    pad_ref[n_ctx + CONV_GAP:n_ctx + 2 * CONV_GAP, :] = gap
    pad_ref[lt + 2 * CONV_GAP:lt + 3 * CONV_GAP, :] = gap
    for a, b, sh in segs:
        pad_ref[a + sh:b + sh, :] = x_ref[0, a:b, :]
    is_qk = pl.program_id(1) < 2 * REC_HEADS
    for a, b, sh in segs:
        acc = None
        for kk in range(CONV_K):
            s0 = a + sh + kk - CONV_K // 2
            tap = pad_ref[s0:s0 + (b - a), :] * w_ref[kk:kk + 1, :]
            acc = tap if acc is None else acc + tap
        y = _silu(acc)
        inv = lax.rsqrt(jnp.sum(y * y, axis=1, keepdims=True) + 1e-6)
        o_ref[0, a:b, :] = y * jnp.where(is_qk, inv, 1.0)


def _gdn_prep(z_gdn, conv_w, n_ctx):
    b, lt, _ = z_gdn.shape
    return pl.pallas_call(
        functools.partial(_gdn_prep_kernel, n_ctx=n_ctx),
        grid=(b, ZG_CONV // REC_D),
        in_specs=[pl.BlockSpec((1, lt, REC_D), lambda i, j: (i, 0, j)),
                  pl.BlockSpec((CONV_K, REC_D), lambda i, j: (0, j))],
        out_specs=pl.BlockSpec((1, lt, REC_D), lambda i, j: (i, 0, j)),
        out_shape=jax.ShapeDtypeStruct((b, lt, ZG_CONV), F32),
        scratch_shapes=[pltpu.VMEM((lt + 3 * CONV_GAP, REC_D), F32)],
        compiler_params=_cp(("arbitrary", "arbitrary")),
        name="gdn_prep",
    )(z_gdn, conv_w)


def _gdn_tile(items):
    t_len = items[0]["q"].shape[0]
    n_chunks = t_len // CHUNK
    for it in items:
        ri = _time_index((t_len, t_len), 0, it["reverse"])
        ci = _time_index((t_len, t_len), 1, it["reverse"])
        it["ri"], it["ci"] = ri, ci
        same = _blk(ri, CHUNK) == _blk(ci, CHUNK)
        it["dec"] = jnp.where(jnp.logical_and(same, ri >= ci),
                              jnp.exp2(jnp.minimum(it["gcol"] - it["grow"], 0.0)), 0.0)
        it["kb"] = it["k"] * it["beta"]
        it["kbf"] = it["k"].astype(BF16)
    for it in items:
        ri, ci = it["ri"], it["ci"]
        lm = jnp.where(ri > ci, _dot_nt(it["kb"].astype(BF16), it["kbf"]) * it["dec"], 0.0)
        it["lm"] = lm
        it["tinv"] = jnp.where(ri == ci, 1.0, 0.0) - jnp.where(_blk(ri, 2) == _blk(ci, 2), lm, 0.0)
    hb = 2
    while hb < CHUNK:
        for it in items:
            ri, ci = it["ri"], it["ci"]
            off = jnp.where(_blk(ri, 2 * hb) == _blk(ci, 2 * hb),
                            jnp.where(_blk(ri, hb) == _blk(ci, hb), 0.0, it["lm"]), 0.0)
            it["tb"] = it["tinv"].astype(BF16)
            it["to"] = _dot(it["tb"], off.astype(BF16)).astype(BF16)
        for it in items:
            it["tinv"] = it["tinv"] - _dot(it["to"], it["tb"])
        hb *= 2
    for it in items:
        eg = jnp.exp2(it["gcol"])
        tb = it["tinv"].astype(BF16)
        it["u"] = _dot(tb, (it["v"] * it["beta"]).astype(BF16))
        it["w"] = _dot(tb, (it["kb"] * eg).astype(BF16)).astype(BF16)
        qs = it["q"] * (REC_D ** -0.5)
        it["aqk"] = (_dot_nt(qs.astype(BF16), it["kbf"]) * it["dec"]).astype(BF16)
        it["qd"] = (qs * eg).astype(BF16)
        kd = it["k"] * jnp.exp2(it["gtot"] - it["gcol"])
        it["kdt"] = [kd[c * CHUNK:(c + 1) * CHUNK].T.astype(BF16) for c in range(n_chunks)]
        it["s"] = it["s_ref"][...]
        it["o"], it["vn"] = [None] * n_chunks, [None] * n_chunks
    for step in range(n_chunks):
        for it in items:
            c = n_chunks - 1 - step if it["reverse"] else step
            it["c"] = c
            rows = slice(c * CHUNK, (c + 1) * CHUNK)
            it["sb"] = it["s"].astype(BF16)
            it["vn"][c] = (it["u"][rows] - _dot(it["w"][rows], it["sb"])).astype(BF16)
        for it in items:
            c = it["c"]
            rows = slice(c * CHUNK, (c + 1) * CHUNK)
            it["o"][c] = _dot(it["qd"][rows], it["sb"])
            it["s"] = (it["s"] * jnp.exp2(it["gtot"][c * CHUNK:c * CHUNK + 1])
                       + _dot(it["kdt"][c], it["vn"][c]))
    outs = []
    for it in items:
        it["s_ref"][...] = it["s"]
        outs.append(jnp.concatenate(it["o"], axis=0) + _dot(it["aqk"], jnp.concatenate(it["vn"], axis=0)))
    return outs


def _gdn_kernel(qf_ref, kf_ref, vf_ref, gf_ref, qb_ref, kb_ref, vb_ref, gb_ref, par_ref, of_ref, ob_ref, s_ref):
    @pl.when(pl.program_id(1) == 0)
    def _():
        s_ref[...] = jnp.zeros(s_ref.shape, F32)

    neg_a = -jnp.exp(par_ref[0:1, :])
    dt_bias = par_ref[1:2, :]
    r2 = lax.broadcasted_iota(jnp.int32, (TOK_TILE, TOK_TILE), 0)
    c2 = lax.broadcasted_iota(jnp.int32, (TOK_TILE, TOK_TILE), 1)
    same = _blk(r2, CHUNK) == _blk(c2, CHUNK)
    items, dests = [], []
    for d, (q_ref, k_ref, v_ref, g_ref, o_ref) in enumerate(((qf_ref, kf_ref, vf_ref, gf_ref, of_ref),
                                                              (qb_ref, kb_ref, vb_ref, gb_ref, ob_ref))):
        ab = g_ref[0]
        xa = ab + dt_bias
        g = (neg_a * LOG2E) * (jnp.maximum(xa, 0.0) + jnp.log(1.0 + jnp.exp(-jnp.abs(xa))))
        tri = jnp.where(jnp.logical_and(same, (r2 <= c2) if d == 1 else (r2 >= c2)), 1.0, 0.0)
        gc = _dot(tri, g, HIGHEST)
        gtot = _dot(jnp.where(same, 1.0, 0.0), g, HIGHEST)
        gct = gc.T
        beta = _sigmoid(ab)
        for h in range(REC_HEADS):
            cols = slice(h * REC_D, (h + 1) * REC_D)
            ln = d * REC_HEADS + h
            items.append(dict(
                q=q_ref[0, :, cols], k=k_ref[0, :, cols], v=v_ref[0, :, cols],
                gcol=gc[:, ln:ln + 1], grow=gct[ln:ln + 1, :], gtot=gtot[:, ln:ln + 1],
                beta=beta[:, 2 * REC_HEADS + ln:2 * REC_HEADS + ln + 1], s_ref=s_ref.at[d, h],
                reverse=(d == 1)))
            dests.append((o_ref, cols))
    for (o_ref, cols), o in zip(dests, _gdn_tile(items)):
        o_ref[0, :, cols] = o


def _gdn(qkv, z_gdn, par):
    b, lt, _ = qkv.shape
    n = lt // TOK_TILE
    fwd = lambda k: pl.BlockSpec((1, TOK_TILE, REC_W), lambda i, s: (i, s, k))
    bwd = lambda k: pl.BlockSpec((1, TOK_TILE, REC_W), lambda i, s: (i, _rev_tile(s, n), k))
    gcol = ZG_CONV // 128
    return pl.pallas_call(
        _gdn_kernel,
        grid=(b, n),
        in_specs=[fwd(0), fwd(1), fwd(2), pl.BlockSpec((1, TOK_TILE, 128), lambda i, s: (i, s, gcol)),
                  bwd(0), bwd(1), bwd(2), pl.BlockSpec((1, TOK_TILE, 128), lambda i, s: (i, _rev_tile(s, n), gcol)),
                  pl.BlockSpec((8, 128), lambda i, s: (0, 0))],
        out_specs=[pl.BlockSpec((1, TOK_TILE, REC_W), lambda i, s: (i, s, 0)),
                   pl.BlockSpec((1, TOK_TILE, REC_W), lambda i, s: (i, _rev_tile(s, n), 0))],
        out_shape=[jax.ShapeDtypeStruct((b, lt, REC_W), F32)] * 2,
        scratch_shapes=[pltpu.VMEM((2, REC_HEADS, REC_D, REC_D), F32)],
        compiler_params=_cp(("arbitrary", "arbitrary")),
        name="gdn_scan",
    )(qkv, qkv, qkv, z_gdn, qkv, qkv, qkv, z_gdn, par)


def _layer_norm(x, g, b):
    mu = jnp.mean(x, axis=-1, keepdims=True)
    xc = x - mu
    var = jnp.mean(xc * xc, axis=-1, keepdims=True)
    return xc * lax.rsqrt(var + LN_EPS) * g + b


def _head_norm_gate(o, gate, w):
    outs = []
    for h in range(REC_HEADS):
        cols = slice(h * REC_D, (h + 1) * REC_D)
        oh = o[:, cols]
        n = oh * lax.rsqrt(jnp.mean(oh * oh, axis=-1, keepdims=True) + RMS_EPS) * w
        outs.append(n * _silu(gate[:, cols].astype(F32)))
    return jnp.concatenate(outs, axis=1)


def _merge_kernel(mla_ref, hf_ref, hb_ref, gf_ref, gb_ref, gates_ref, hgate_ref, ggate_ref, x_ref,
                  m2_ref, m3_ref, m4_ref, hw_ref, gw_ref, wb_ref, wo_ref, lg_ref, lbias_ref, wr_ref,
                  x1_ref, h2_ref, lt_ref):
    hg = _head_norm_gate(hf_ref[0] + hb_ref[0], hgate_ref[0], hw_ref[...]).astype(BF16)
    gd = _head_norm_gate(gf_ref[0] + gb_ref[0], ggate_ref[0], gw_ref[...]).astype(BF16)
    y = jnp.zeros((TOK_TILE, D_MODEL), F32)
    for n, o in enumerate((mla_ref[0], hg, gd)):
        y = y + _sigmoid(gates_ref[0, :, n * D_MODEL:(n + 1) * D_MODEL].astype(F32)) * _dot(o, wb_ref[n])
    y = _dot(y.astype(BF16), wo_ref[...])
    x1 = _layer_norm(DEEPNORM_ALPHA * x_ref[0] + m2_ref[0] * y, lg_ref[...], lbias_ref[...])
    x1_ref[0] = x1
    h2 = x1 * (1.0 + m4_ref[0]) + m3_ref[0]
    for p, part in enumerate(_pack_parts(h2)):
        h2_ref[p, 0] = part
    lt_ref[...] = _dot_nt(wr_ref[...], h2, HIGHEST)


def _merge(mla_o, hg_f, hg_b, gd_f, gd_b, z_main, x_all, mod_l, hg_w, gdn_w, wb, wo, ln_g, ln_b, wr_t):
    b, lt, d = x_all.shape
    n = lt // TOK_TILE
    tok = lambda w, k=0: pl.BlockSpec((1, TOK_TILE, w), lambda i, t: (i, t, k))
    full = lambda a: pl.BlockSpec(a.shape, lambda i, t: (0,) * a.ndim)
    return pl.pallas_call(
        _merge_kernel,
        grid=(b, n),
        in_specs=[tok(BRANCH_W), tok(REC_W), tok(REC_W), tok(REC_W), tok(REC_W),
                  tok(N_BRANCH * D_MODEL, 0), tok(REC_W, ZM_HGATE), tok(REC_W, ZM_GGATE),
                  tok(d), _mod_spec(b, 2), _mod_spec(b, 3), _mod_spec(b, 4),
                  full(hg_w), full(gdn_w), full(wb), full(wo), full(ln_g), full(ln_b), full(wr_t)],
        out_specs=[tok(d), pl.BlockSpec((ROW_PARTS, 1, TOK_TILE, SC_ROW), lambda i, t: (0, i, t, 0)),
                   pl.BlockSpec((N_EXPERTS, TOK_TILE), lambda i, t: (0, i * n + t))],
        out_shape=[jax.ShapeDtypeStruct((b, lt, d), F32),
                   jax.ShapeDtypeStruct((ROW_PARTS, b, lt, SC_ROW), jnp.int32),
                   jax.ShapeDtypeStruct((N_EXPERTS, b * lt), F32)],
        compiler_params=_cp(("arbitrary", "arbitrary")),
        name="merge",
    )(mla_o, hg_f, hg_b, gd_f, gd_b, z_main, z_main, z_main, x_all, mod_l, mod_l, mod_l,
      hg_w, gdn_w, wb, wo, ln_g, ln_b, wr_t)


def _first_max(x, idx, axes):
    m = x
    for ax in axes:
        m = jnp.max(m, axis=ax, keepdims=True)
    first = jnp.where(x == m, idx, jnp.int32(2 ** 30))
    for ax in axes:
        first = jnp.min(first, axis=ax, keepdims=True)
    return m, first


def _route_kernel(lt_ref, bias_ref, e_ref, r_ref, w_ref, cnt_ref, run_ref):
    @pl.when(pl.program_id(0) == 0)
    def _():
        run_ref[...] = jnp.zeros(run_ref.shape, F32)

    n_tok = lt_ref.shape[1]
    per = N_EXPERTS // N_GROUPS
    scores = _sigmoid(lt_ref[...]).reshape(N_GROUPS, per, n_tok)
    sel = scores + bias_ref[...]
    ig = lax.broadcasted_iota(jnp.int32, sel.shape, 0)
    ij = lax.broadcasted_iota(jnp.int32, sel.shape, 1)
    top1, a1 = _first_max(sel, ij, (1,))
    top2 = jnp.max(jnp.where(ij == a1, -jnp.inf, sel), axis=1, keepdims=True)
    grp = top1 + top2
    igg = lax.broadcasted_iota(jnp.int32, grp.shape, 0)
    gsel = jnp.zeros(grp.shape, F32)
    for _ in range(TOPK_GROUPS):
        _, a = _first_max(grp, igg, (0,))
        hit = igg == a
        gsel = jnp.where(hit, 1.0, gsel)
        grp = jnp.where(hit, -jnp.inf, grp)
    cur = jnp.where(gsel > 0.5, sel, -jnp.inf)
    ie = ig * per + ij
    esel = jnp.zeros(cur.shape, F32)
    for _ in range(TOP_K):
        _, a = _first_max(cur, ie, (1, 0))
        hit = ie == a
        esel = jnp.where(hit, 1.0, esel)
        cur = jnp.where(hit, -jnp.inf, cur)
    w = scores * esel
    tot = jnp.sum(jnp.sum(w, axis=1, keepdims=True), axis=0, keepdims=True)
    w = (w / tot * ROUTED_SCALE).reshape(N_EXPERTS, n_tok)
    m = esel.reshape(N_EXPERTS, n_tok)
    mb = m.astype(BF16)
    ti = lax.broadcasted_iota(jnp.int32, (n_tok, n_tok), 0)
    tj = lax.broadcasted_iota(jnp.int32, (n_tok, n_tok), 1)
    upto = _dot(mb, jnp.where(ti <= tj, 1.0, 0.0).astype(BF16))
    ei = lax.broadcasted_iota(jnp.int32, (N_EXPERTS, N_EXPERTS), 0)
    ej = lax.broadcasted_iota(jnp.int32, (N_EXPERTS, N_EXPERTS), 1)
    lower = _dot(jnp.where(ej < ei, 1.0, 0.0).astype(BF16), mb)
    run = run_ref[:, 0:1]
    rank = run + upto - 1.0
    run_ref[...] = jnp.broadcast_to(run + upto[:, n_tok - 1:n_tok], run_ref.shape)
    cnt_ref[...] = run_ref[...]
    eid = lax.broadcasted_iota(jnp.int32, m.shape, 0).astype(F32)
    rows_e, rows_r, rows_w = [], [], []
    for k in range(TOP_K):
        pick = jnp.where(lower == float(k), m, 0.0)
        rows_e.append(jnp.sum(pick * eid, axis=0, keepdims=True))
        rows_r.append(jnp.sum(pick * rank, axis=0, keepdims=True))
        rows_w.append(jnp.sum(pick * w, axis=0, keepdims=True))
    e_ref[...] = jnp.concatenate(rows_e, axis=0).astype(jnp.int32)
    r_ref[...] = jnp.concatenate(rows_r, axis=0).astype(jnp.int32)
    w_ref[...] = jnp.concatenate(rows_w, axis=0).T


ROUTE_TILE = 512


def _route(logits_t, bias):
    n_tok = logits_t.shape[1]
    tt = ROUTE_TILE
    return pl.pallas_call(
        _route_kernel,
        grid=(n_tok // tt,),
        in_specs=[pl.BlockSpec((N_EXPERTS, tt), lambda i: (0, i)),
                  pl.BlockSpec((N_GROUPS, N_EXPERTS // N_GROUPS, 1), lambda i: (0, 0, 0))],
        out_specs=[pl.BlockSpec((TOP_K, tt), lambda i: (0, i)),
                   pl.BlockSpec((TOP_K, tt), lambda i: (0, i)),
                   pl.BlockSpec((tt, TOP_K), lambda i: (i, 0)),
                   pl.BlockSpec((N_EXPERTS, 128), lambda i: (0, 0))],
        out_shape=[jax.ShapeDtypeStruct((TOP_K, n_tok), jnp.int32),
                   jax.ShapeDtypeStruct((TOP_K, n_tok), jnp.int32),
                   jax.ShapeDtypeStruct((n_tok, TOP_K), F32),
                   jax.ShapeDtypeStruct((N_EXPERTS, 128), F32)],
        scratch_shapes=[pltpu.VMEM((N_EXPERTS, 128), F32)],
        compiler_params=_cp(("arbitrary",)),
        name="route",
    )(logits_t, bias)


def _slot_kernel(start_ref, e_ref, r_ref, p_ref, *, n_rows):
    e = e_ref[...]
    pos = r_ref[...]
    for j in range(N_EXPERTS):
        pos = pos + jnp.where(e == j, start_ref[j], 0)
    for p in range(ROW_PARTS):
        p_ref[p] = pos + p * n_rows


def _slots(starts, e_k, r_k, n_rows):
    n_tok = e_k.shape[1]
    tt = ROUTE_TILE
    spec = pl.BlockSpec((TOP_K, tt), lambda i, s: (0, i))
    return pl.pallas_call(
        functools.partial(_slot_kernel, n_rows=n_rows),
        grid_spec=pltpu.PrefetchScalarGridSpec(
            num_scalar_prefetch=1, grid=(n_tok // tt,), in_specs=[spec, spec],
            out_specs=pl.BlockSpec((ROW_PARTS, TOP_K, tt), lambda i, s: (0, 0, i))),
        out_shape=jax.ShapeDtypeStruct((ROW_PARTS, TOP_K, n_tok), jnp.int32),
        compiler_params=_cp(("arbitrary",)),
        name="moe_slots",
    )(starts, e_k, r_k)


def _sc_mesh():
    return plsc.VectorSubcoreMesh(core_axis_name="c", subcore_axis_name="s")


def _sc_scatter(x, idx, n_out):
    parts, reps, t = idx.shape
    nbt = t // SC_WINDOW

    @functools.partial(pl.kernel, out_type=jax.ShapeDtypeStruct((n_out, SC_ROW), x.dtype), mesh=_sc_mesh())
    def k(x_hbm, i_hbm, o_hbm):
        def body(x_vmem, i_vmem):
            for r in range(reps):
                pltpu.sync_copy(x_vmem, o_hbm.at[i_vmem.at[0, r]])

        pltpu.emit_pipeline(
            body,
            grid=(parts * nbt,),
            in_specs=[pl.BlockSpec((SC_WINDOW, SC_ROW), lambda i: (i, 0)),
                      pl.BlockSpec((1, reps, SC_WINDOW), lambda i: (i // nbt, 0, i % nbt))],
            out_specs=[],
            core_axis_name=("c", "s"),
            dimension_semantics=(pltpu.PARALLEL,),
        )(x_hbm, i_hbm)

    return k(x, idx)


def _sc_gather(table, idx):
    n = idx.shape[0]

    @functools.partial(pl.kernel, out_type=jax.ShapeDtypeStruct((n, SC_ROW), table.dtype), mesh=_sc_mesh())
    def k(x_hbm, i_hbm, o_hbm):
        def body(i_vmem, o_vmem):
            pltpu.sync_copy(x_hbm.at[i_vmem.at[0]], o_vmem)

        pltpu.emit_pipeline(
            body,
            grid=(n // SC_WINDOW,),
            in_specs=[pl.BlockSpec((1, SC_WINDOW), lambda i: (0, i))],
            out_specs=[pl.BlockSpec((SC_WINDOW, SC_ROW), lambda i: (i, 0))],
            core_axis_name=("c", "s"),
            dimension_semantics=(pltpu.PARALLEL,),
        )(i_hbm, o_hbm)

    return k(table, idx.reshape(1, n))


MOE_TM = 512


def _expert_kernel(te_ref, tr_ref, x_ref, wgu_ref, wd_ref, y_ref, wgu_bf, wd_bf):
    i = pl.program_id(0)
    valid = tr_ref[i]

    @pl.when(jnp.logical_or(i == 0, te_ref[i] != te_ref[jnp.maximum(i - 1, 0)]))
    def _():
        wgu_bf[...] = wgu_ref[0].astype(BF16)
        wd_bf[...] = wd_ref[0].astype(BF16)

    @pl.when(valid > 0)
    def _():
        live = lax.broadcasted_iota(jnp.int32, x_ref.shape[1:], 0) < valid
        x = _unpack_parts([jnp.where(live, x_ref[p], 0) for p in range(ROW_PARTS)])
        gu = _dot(x, wgu_bf[...])
        act = _silu(gu[:, :EXPERT_FF]) * gu[:, EXPERT_FF:]
        for p, part in enumerate(_pack_parts(_dot(act.astype(BF16), wd_bf[...]))):
            y_ref[p] = part

    @pl.when(valid == 0)
    def _():
        y_ref[...] = jnp.zeros(y_ref.shape, y_ref.dtype)


def _experts(tile_expert, tile_rows, xs, wgu, wd, layer):
    _, n_rows, _ = xs.shape
    d = wgu.shape[2]
    rows = pl.BlockSpec((ROW_PARTS, MOE_TM, SC_ROW), lambda i, te, tr: (0, i, 0))
    return pl.pallas_call(
        _expert_kernel,
        grid_spec=pltpu.PrefetchScalarGridSpec(
            num_scalar_prefetch=2,
            grid=(n_rows // MOE_TM,),
            in_specs=[rows,
                      pl.BlockSpec((None, 1, d, 2 * EXPERT_FF), lambda i, te, tr: (layer, te[i], 0, 0)),
                      pl.BlockSpec((None, 1, EXPERT_FF, d), lambda i, te, tr: (layer, te[i], 0, 0))],
            out_specs=rows,
            scratch_shapes=[pltpu.VMEM((d, 2 * EXPERT_FF), BF16), pltpu.VMEM((EXPERT_FF, d), BF16)]),
        out_shape=jax.ShapeDtypeStruct(xs.shape, jnp.int32),
        compiler_params=_cp(("arbitrary",)),
        name="moe_experts",
    )(tile_expert, tile_rows, xs, wgu, wd)


def _combine_kernel(g_ref, w_ref, h_ref, x_ref, m5_ref, wsgu_ref, wsd_ref, lg_ref, lb_ref, *rest):
    h = _unpack_parts([h_ref[p, 0] for p in range(ROW_PARTS)])
    gu = _dot(h, wsgu_ref[...])
    act = _silu(gu[:, :SHARED_FF]) * gu[:, SHARED_FF:]
    f = _dot(act.astype(BF16), wsd_ref[...])
    w = w_ref[...]
    for k in range(TOP_K):
        f = f + w[:, k:k + 1] * _unpack_parts([g_ref[p, k] for p in range(ROW_PARTS)]).astype(F32)
    x_new = _layer_norm(DEEPNORM_ALPHA * x_ref[0] + m5_ref[0] * f, lg_ref[...], lb_ref[...])
    if len(rest) == 1:
        rest[0][0] = x_new
    else:
        sh_ref, sc_ref, o_ref, hn_ref = rest
        o_ref[0] = x_new
        hn_ref[0] = (x_new * (1.0 + sc_ref[0]) + sh_ref[0]).astype(hn_ref.dtype)


def _combine(g, w_k, h2p, x1, mod_l, wsgu, wsd, ln_g, ln_b, mod_next):
    b, lt, d = x1.shape
    n = lt // TOK_TILE
    tok = pl.BlockSpec((1, TOK_TILE, d), lambda i, t: (i, t, 0))
    full = lambda a: pl.BlockSpec(a.shape, lambda i, t: (0,) * a.ndim)
    in_specs = [pl.BlockSpec((ROW_PARTS, TOP_K, TOK_TILE, SC_ROW), lambda i, t: (0, 0, i * n + t, 0)),
                pl.BlockSpec((TOK_TILE, TOP_K), lambda i, t: (i * n + t, 0)),
                pl.BlockSpec((ROW_PARTS, 1, TOK_TILE, SC_ROW), lambda i, t: (0, i, t, 0)),
                tok, _mod_spec(b, 5), full(wsgu), full(wsd), full(ln_g), full(ln_b)]
    args = [g, w_k, h2p, x1, mod_l, wsgu, wsd, ln_g, ln_b]
    out_specs, out_shape = tok, jax.ShapeDtypeStruct((b, lt, d), F32)
    if mod_next is not None:
        in_specs += [_mod_spec(b, 0), _mod_spec(b, 1)]
        args += [mod_next, mod_next]
        out_specs, out_shape = [tok, tok], [out_shape, jax.ShapeDtypeStruct((b, lt, d), BF16)]
    return pl.pallas_call(
        _combine_kernel,
        grid=(b, n),
        in_specs=in_specs,
        out_specs=out_specs,
        out_shape=out_shape,
        compiler_params=_cp(("arbitrary", "arbitrary")),
        name="moe_combine",
    )(*args)


def _moe_sparse(h2p, logits_t, bias, wgu, wd, layer, wsgu, wsd, x1, mod_l, ln_g, ln_b, mod_next):
    n_tok = h2p.shape[1] * h2p.shape[2]
    e_k, r_k, w_k, counts = _route(logits_t, bias)
    counts = counts[:, 0].astype(jnp.int32)
    padded = (counts + MOE_TM - 1) // MOE_TM * MOE_TM
    ends = jnp.cumsum(padded)
    starts = ends - padded
    n_rows = (n_tok * TOP_K + N_EXPERTS * (MOE_TM - 1)) // MOE_TM * MOE_TM
    tile0 = jnp.arange(n_rows // MOE_TM, dtype=jnp.int32) * MOE_TM
    tile_expert = jnp.minimum(jnp.sum(tile0[:, None] >= ends[None, :], axis=1), N_EXPERTS - 1).astype(jnp.int32)
    inside = jnp.logical_and(tile0[:, None] >= starts[None, :], tile0[:, None] < ends[None, :])
    left = jnp.clip(counts[None, :] - (tile0[:, None] - starts[None, :]), 0, MOE_TM)
    tile_rows = jnp.sum(jnp.where(inside, left, 0), axis=1).astype(jnp.int32)
    idx = _slots(starts.astype(jnp.int32), e_k, r_k, n_rows)
    xs = _sc_scatter(h2p.reshape(ROW_PARTS * n_tok, SC_ROW), idx, ROW_PARTS * n_rows)
    ys = _experts(tile_expert, tile_rows, xs.reshape(ROW_PARTS, n_rows, SC_ROW), wgu, wd, layer)
    g = _sc_gather(ys.reshape(ROW_PARTS * n_rows, SC_ROW), idx.reshape(-1))
    g = g.reshape(ROW_PARTS, TOP_K, n_tok, SC_ROW)
    return _combine(g, w_k, h2p, x1, mod_l, wsgu, wsd, ln_g, ln_b, mod_next)


def _rope_tables(n_ctx, n_lat):
    rows = n_lat // GRID_W
    row = jnp.broadcast_to(jnp.arange(rows, dtype=F32)[:, None], (rows, GRID_W)).reshape(-1)
    col = jnp.broadcast_to(jnp.arange(GRID_W, dtype=F32)[None, :], (rows, GRID_W)).reshape(-1)
    inv = ROPE_BASE ** (-jnp.arange(ROPE_FREQS, dtype=F32) / ROPE_FREQS)
    ang = jnp.stack([row[:, None] * inv, col[:, None] * inv], axis=1)
    cos = jnp.cos(ang)[:, :, None, :]
    sin = jnp.sin(ang)[:, :, None, :]
    cos32 = jnp.broadcast_to(cos, (n_lat, 2, 2, ROPE_FREQS)).reshape(n_lat, MLA_ROPE)
    sin32 = jnp.concatenate([-sin, sin], axis=2).reshape(n_lat, MLA_ROPE)
    cos32 = jnp.concatenate([jnp.ones((n_ctx, MLA_ROPE), F32), cos32], axis=0)
    sin32 = jnp.concatenate([jnp.zeros((n_ctx, MLA_ROPE), F32), sin32], axis=0)
    lt = n_ctx + n_lat
    ca = jnp.concatenate([jnp.ones((lt, MLA_NOPE), F32), cos32, jnp.zeros((lt, 32), F32)], axis=1)
    sb = jnp.concatenate([jnp.zeros((lt, MLA_NOPE), F32), sin32, jnp.zeros((lt, 32), F32)], axis=1)
    return ca, sb


def _rope_partner():
    idx = np.arange(MLA_ROPE).reshape(2, 2, ROPE_FREQS)
    return idx[:, ::-1, :].reshape(-1)


IN_SIZES = (MLA_Q_LORA, MLA_KV_LORA + MLA_ROPE, REC_W, REC_W, REC_W, REC_W, REC_W,
            3 * REC_W, REC_W, 2 * REC_HEADS, 2 * REC_HEADS, N_BRANCH * D_MODEL)
IN_OFFS = tuple(int(o) for o in np.cumsum((0,) + IN_SIZES))
REPACK_ROWS = 128


def _repack_kernel(w_ref, main_ref, f_ref, mla_ref, gdn_ref):
    rows = w_ref.shape[1]
    seg = lambda i: w_ref[0, :, IN_OFFS[i]:IN_OFFS[i + 1]]
    zeros = lambda n: jnp.zeros((rows, n), F32)
    k_rope = w_ref[0, :, IN_OFFS[1] + MLA_KV_LORA:IN_OFFS[2]]
    h = ROPE_FREQS
    k_partner = jnp.concatenate([k_rope[:, h:2 * h], k_rope[:, :h], k_rope[:, 3 * h:], k_rope[:, 2 * h:3 * h]], axis=1)
    main_ref[0] = jnp.concatenate([seg(11), seg(2), seg(3), seg(6), seg(8)], axis=1).astype(BF16)
    f_ref[0] = jnp.concatenate([seg(4), seg(5)], axis=1).astype(BF16)
    mla_ref[0] = jnp.concatenate([seg(0), w_ref[0, :, IN_OFFS[1]:IN_OFFS[1] + MLA_KV_LORA],
                                  zeros(MLA_NOPE), k_rope, zeros(32),
                                  zeros(MLA_NOPE), k_partner, zeros(32)], axis=1).astype(BF16)
    gdn_ref[0] = jnp.concatenate([seg(7), seg(9), seg(10), zeros(256 - 4 * REC_HEADS)], axis=1).astype(BF16)


def _repack_w_in(w_in):
    depth, d, width = w_in.shape
    widths = (N_BRANCH * D_MODEL + 4 * REC_W, 2 * REC_W, ZA_WIDTH, ZG_WIDTH)
    return pl.pallas_call(
        _repack_kernel,
        grid=(depth, d // REPACK_ROWS),
        in_specs=[pl.BlockSpec((1, REPACK_ROWS, width), lambda l, r: (l, r, 0))],
        out_specs=[pl.BlockSpec((1, REPACK_ROWS, w), lambda l, r: (l, r, 0)) for w in widths],
        out_shape=[jax.ShapeDtypeStruct((depth, d, w), BF16) for w in widths],
        compiler_params=_cp(("arbitrary", "arbitrary")),
        name="repack_w_in",
    )(w_in)


def _pack_mla(w_q_b, w_kv_b):
    partner = _rope_partner()
    r = w_q_b.shape[0]
    qb = w_q_b.reshape(r, MLA_HEADS, MLA_NOPE + MLA_ROPE)
    zq = lambda n: jnp.zeros((r, MLA_HEADS, n), w_q_b.dtype)
    wq1 = jnp.concatenate([qb, zq(32)], axis=2).reshape(r, MLA_HEADS * 128)
    wq2 = jnp.concatenate([zq(MLA_NOPE), qb[:, :, MLA_NOPE:][:, :, partner], zq(32)], axis=2).reshape(r, MLA_HEADS * 128)
    rk = w_kv_b.shape[0]
    kvb = w_kv_b.reshape(rk, MLA_HEADS, MLA_NOPE + MLA_V)
    wk = jnp.concatenate([kvb[:, :, :MLA_NOPE], jnp.zeros((rk, MLA_HEADS, 64), w_kv_b.dtype)], axis=2)
    wk = wk.reshape(rk, MLA_HEADS * 128)
    wv = jnp.concatenate([kvb[:, :, MLA_NOPE:], jnp.zeros((rk, MLA_HEADS, 128 - MLA_V), w_kv_b.dtype)], axis=2)
    wv = wv.reshape(rk, MLA_HEADS * 128)
    bf = lambda a: a.astype(BF16)
    return bf(wq1), bf(wq2), bf(wk), bf(wv)


def kernel(x, c, ctx, c_ctx, w_mod, b_mod, w_in, q_a_norm, w_q_b, kv_a_norm, w_kv_b, hg_lb_logits, hg_norm,
           gdn_conv, gdn_a_log, gdn_dt_bias, gdn_norm, w_branch, w_out, ln1_g, ln1_b, ln2_g, ln2_b,
           w_router, router_bias, w_gu, w_down, w_sh_gu, w_sh_down):
    batch, n_lat, d = x.shape
    n_ctx = ctx.shape[1]
    assert n_ctx == TOK_TILE and n_lat % TOK_TILE == 0 and batch < MOD_ROWS and d == D_MODEL
    lt = n_ctx + n_lat

    c_all = jnp.zeros((MOD_ROWS, d), F32).at[:batch].set(c).at[batch].set(c_ctx)
    mod = _mod_all(c_all, w_mod, b_mod).reshape(DEPTH, MOD_ROWS, 1, 6 * d)
    ca, sb = _rope_tables(n_ctx, n_lat)
    lb_soft = jax.nn.softmax(hg_lb_logits.astype(F32), axis=0)
    lower = (jnp.cumsum(lb_soft, axis=0) - lb_soft[0]).reshape(DEPTH, 2, 1, REC_W)
    row = lambda a: a.reshape(1, -1)

    w_main, w_f, w_mla, w_gdn = _repack_w_in(w_in)
    x_all = jnp.concatenate([ctx, x], axis=1)
    h = _modulate(x_all, mod[0])
    for l in range(DEPTH):
        wq1, wq2, wk, wv = _pack_mla(w_q_b[l], w_kv_b[l])
        mod_l = mod[l]
        z_main = _proj(h, w_main, l, 2 * REC_W, BF16)
        z_f = _proj(h, w_f, l, REC_W)
        z_mla = _proj(h, w_mla, l, ZA_WIDTH)
        z_gdn = _proj(h, w_gdn, l, ZG_WIDTH // 2)

        q, k, v = _mla_prep(z_mla, row(q_a_norm[l]), row(kv_a_norm[l]), wq1, wq2, wk, wv, ca, sb)
        mla_o = _attention(q, k, v, n_ctx)

        hg_f, hg_b = _hgrn2(z_main, z_f, lower[l])

        qkv = _gdn_prep(z_gdn, gdn_conv[l], n_ctx)
        par = jnp.zeros((8, 128), F32)
        par = par.at[0, :2 * REC_HEADS].set(gdn_a_log[l].reshape(-1))
        par = par.at[1, :2 * REC_HEADS].set(gdn_dt_bias[l].reshape(-1))
        gd_f, gd_b = _gdn(qkv, z_gdn, par)

        x1, h2, logits_t = _merge(mla_o, hg_f, hg_b, gd_f, gd_b, z_main, x_all, mod_l,
                                  row(hg_norm[l]), row(gdn_norm[l]), w_branch[l].astype(BF16),
                                  w_out[l].astype(BF16), row(ln1_g[l]), row(ln1_b[l]), w_router[l].T)
        out = _moe_sparse(h2, logits_t,
                          router_bias[l].reshape(N_GROUPS, N_EXPERTS // N_GROUPS, 1),
                          w_gu, w_down, l,
                          w_sh_gu[l].astype(BF16), w_sh_down[l].astype(BF16),
                          x1, mod_l, row(ln2_g[l]), row(ln2_b[l]), mod[l + 1] if l + 1 < DEPTH else None)
        x_all, h = out if l + 1 < DEPTH else (out, None)
    return x_all[:, n_ctx:, :]
```

```python
import functools
import math

import numpy as np
import jax
import jax.numpy as jnp
from jax import lax
from jax.experimental import pallas as pl
from jax.experimental.pallas import tpu as pltpu
from jax.experimental.pallas import tpu_sc as plsc

F32 = jnp.float32
BF16 = jnp.bfloat16
HIGHEST = lax.Precision.HIGHEST
LOG2E = math.log2(math.e)

D_MODEL = 1024
DEPTH = 4
GRID_W = 64
MLA_HEADS = 8
MLA_Q_LORA = 384
MLA_KV_LORA = 256
MLA_NOPE = 64
MLA_ROPE = 32
MLA_V = 64
MLA_SCALE = (MLA_NOPE + MLA_ROPE) ** -0.5
ROPE_BASE = 10000.0
ROPE_FREQS = MLA_ROPE // 4
REC_HEADS = 4
REC_D = 128
REC_W = REC_HEADS * REC_D
CONV_K = 5
N_BRANCH = 3
BRANCH_W = 512
N_EXPERTS = 64
TOP_K = 8
N_GROUPS = 8
TOPK_GROUPS = 4
EXPERT_FF = 256
SHARED_FF = 256
ROUTED_SCALE = 2.5
DEEPNORM_ALPHA = (2 * DEPTH) ** 0.25
LN_EPS = 1e-6
RMS_EPS = 1e-6

TOK_TILE = 256
CHUNK = 64
MOD_ROWS = 16
VMEM_LIMIT = 56 * 1024 * 1024

ZM_Q, ZM_I, ZM_HGATE, ZM_GGATE = (3 * D_MODEL // REC_W + k for k in range(4))
ZA_WIDTH = MLA_Q_LORA + MLA_KV_LORA + 256
ZG_CONV = 3 * REC_W
ZG_WIDTH = ZG_CONV + 256
ZX_F = 1024
ZX_GDN = ZX_F + 2 * REC_W
ZX_WIDTH = ZX_GDN + ZG_WIDTH


def _cp(sem, vmem=VMEM_LIMIT):
    return pltpu.CompilerParams(dimension_semantics=sem, vmem_limit_bytes=vmem)


def _dot(a, b, precision=None):
    return jnp.dot(a, b, preferred_element_type=F32, precision=precision)


def _dot_nt(a, b, precision=None):
    return lax.dot_general(a, b, (((1,), (1,)), ((), ())), preferred_element_type=F32, precision=precision)


def _sigmoid(x):
    return 1.0 / (1.0 + jnp.exp(-x))


def _silu(x):
    return x * _sigmoid(x)


ROW_PARTS = 2
SC_ROW = D_MODEL // (2 * ROW_PARTS)
SC_WINDOW = 128


def _pack_parts(x):
    q = x.shape[1] // (2 * ROW_PARTS)
    bits = lambda a: lax.bitcast_convert_type(a.astype(BF16).astype(F32), jnp.uint32)
    parts = []
    for p in range(ROW_PARTS):
        hi = bits(x[:, p * q:(p + 1) * q])
        lo = bits(x[:, (ROW_PARTS + p) * q:(ROW_PARTS + p + 1) * q])
        parts.append(lax.bitcast_convert_type(hi | lax.shift_right_logical(lo, jnp.uint32(16)), jnp.int32))
    return parts


def _unpack_parts(parts):
    his, los = [], []
    for p in parts:
        u = lax.bitcast_convert_type(p, jnp.uint32)
        his.append(lax.bitcast_convert_type(u & jnp.uint32(0xFFFF0000), F32).astype(BF16))
        los.append(lax.bitcast_convert_type(lax.shift_left(u, jnp.uint32(16)), F32).astype(BF16))
    return jnp.concatenate(his + los, axis=1)


def _mod_kernel(c_ref, w_ref, b_ref, o_ref):
    s = _silu(c_ref[...])
    o_ref[0] = _dot(s, w_ref[0], HIGHEST) + b_ref[0]


def _mod_all(c_all, w_mod, b_mod):
    tn = 1024
    n = w_mod.shape[-1]
    return pl.pallas_call(
        _mod_kernel,
        grid=(DEPTH, n // tn),
        in_specs=[
            pl.BlockSpec((MOD_ROWS, D_MODEL), lambda l, j: (0, 0)),
            pl.BlockSpec((1, D_MODEL, tn), lambda l, j: (l, 0, j)),
            pl.BlockSpec((1, 1, tn), lambda l, j: (l, 0, j)),
        ],
        out_specs=pl.BlockSpec((1, MOD_ROWS, tn), lambda l, j: (l, 0, j)),
        out_shape=jax.ShapeDtypeStruct((DEPTH, MOD_ROWS, n), F32),
        compiler_params=_cp(("arbitrary", "arbitrary")),
        name="mod_all",
    )(c_all, w_mod, b_mod.reshape(DEPTH, 1, n))


def _mod_spec(batch, k):
    return pl.BlockSpec((1, 1, D_MODEL), lambda b, t: (jnp.where(t == 0, batch, b), 0, k))


def _modulate_kernel(x_ref, sh_ref, sc_ref, o_ref):
    o_ref[0] = (x_ref[0] * (1.0 + sc_ref[0]) + sh_ref[0]).astype(o_ref.dtype)


def _modulate(x_all, mod_l):
    b, lt, d = x_all.shape
    return pl.pallas_call(
        _modulate_kernel,
        grid=(b, lt // TOK_TILE),
        in_specs=[
            pl.BlockSpec((1, TOK_TILE, d), lambda i, t: (i, t, 0)),
            _mod_spec(b, 0),
            _mod_spec(b, 1),
        ],
        out_specs=pl.BlockSpec((1, TOK_TILE, d), lambda i, t: (i, t, 0)),
        out_shape=jax.ShapeDtypeStruct((b, lt, d), BF16),
        compiler_params=_cp(("arbitrary", "arbitrary")),
        name="modulate",
    )(x_all, mod_l, mod_l)


PROJ_ROWS = 256


def _proj_kernel(h_ref, w_ref, o_ref, *, rows):
    def body(r, carry):
        sl = pl.ds(pl.multiple_of(r * rows, rows), rows)
        o_ref[0, sl, :] = _dot(h_ref[0, sl, :], w_ref[...]).astype(o_ref.dtype)
        return carry

    lax.fori_loop(0, h_ref.shape[1] // rows, body, 0)


def _proj(h, w, layer, tn, dtype=F32):
    b, lt, d = h.shape
    n = w.shape[2]
    return pl.pallas_call(
        functools.partial(_proj_kernel, rows=PROJ_ROWS),
        grid=(b, n // tn),
        in_specs=[
            pl.BlockSpec((1, lt, d), lambda i, j: (i, 0, 0)),
            pl.BlockSpec((None, d, tn), lambda i, j: (layer, 0, j)),
        ],
        out_specs=pl.BlockSpec((1, lt, tn), lambda i, j: (i, 0, j)),
        out_shape=jax.ShapeDtypeStruct((b, lt, n), dtype),
        compiler_params=_cp(("arbitrary", "arbitrary")),
        name="proj",
    )(h, w)


def _rms(x, g):
    return x * lax.rsqrt(jnp.mean(x * x, axis=-1, keepdims=True) + RMS_EPS) * g


def _mla_prep_kernel(z_ref, qg_ref, kg_ref, wq1_ref, wq2_ref, wk_ref, wv_ref, ca_ref, sb_ref,
                     q_ref, k_ref, v_ref):
    z = z_ref[0]
    ca = ca_ref[...]
    sb = sb_ref[...]
    qn = _rms(z[:, :MLA_Q_LORA], qg_ref[...]).astype(BF16)
    qa = _dot(qn, wq1_ref[...])
    qb = _dot(qn, wq2_ref[...])
    kvn = _rms(z[:, MLA_Q_LORA:MLA_Q_LORA + MLA_KV_LORA], kg_ref[...]).astype(BF16)
    kn = _dot(kvn, wk_ref[...])
    v = _dot(kvn, wv_ref[...])
    lane = lax.broadcasted_iota(jnp.int32, v.shape, 1)
    v_ref[0] = jnp.where((lane & 127) == MLA_V, 1.0, v).astype(v_ref.dtype)
    o = MLA_Q_LORA + MLA_KV_LORA
    kr = z[:, o:o + 128] * ca + z[:, o + 128:o + 256] * sb
    for h in range(MLA_HEADS):
        sl = slice(h * 128, (h + 1) * 128)
        q_ref[0, :, sl] = ((qa[:, sl] * ca + qb[:, sl] * sb) * (MLA_SCALE * LOG2E)).astype(q_ref.dtype)
        k_ref[0, :, sl] = (kn[:, sl] + kr).astype(k_ref.dtype)


def _mla_prep(z_mla, qg, kg, wq1, wq2, wk, wv, ca, sb):
    b, lt, _ = z_mla.shape
    full = lambda a: pl.BlockSpec(a.shape, lambda i, t: (0,) * a.ndim)
    tok = lambda w: pl.BlockSpec((1, TOK_TILE, w), lambda i, t: (i, t, 0))
    return pl.pallas_call(
        _mla_prep_kernel,
        grid=(b, lt // TOK_TILE),
        in_specs=[tok(ZX_F), full(qg), full(kg), full(wq1), full(wq2), full(wk), full(wv),
                  pl.BlockSpec((TOK_TILE, 128), lambda i, t: (t, 0)),
                  pl.BlockSpec((TOK_TILE, 128), lambda i, t: (t, 0))],
        out_specs=[tok(MLA_HEADS * 128)] * 3,
        out_shape=[jax.ShapeDtypeStruct((b, lt, MLA_HEADS * 128), BF16)] * 3,
        compiler_params=_cp(("arbitrary", "arbitrary")),
        name="mla_prep",
    )(z_mla, qg, kg, wq1, wq2, wk, wv, ca, sb)


ATTN_KEY_CHUNK = 768


def _attn_kernel(q_ref, k_ref, v_ref, o_ref, *, n_ctx):
    def attend(nk):
        kc = min(nk, ATTN_KEY_CHUNK)
        cols = [slice(h * 128, (h + 1) * 128) for h in range(2)]
        s = [[_dot_nt(q_ref[0, :, cols[h]], k_ref[0, c:c + kc, cols[h]]) for c in range(0, nk, kc)]
             for h in range(2)]
        m = [functools.reduce(jnp.maximum, [jnp.max(sc, axis=1, keepdims=True) for sc in s[h]]) for h in range(2)]
        o = [jnp.zeros((TOK_TILE, 128), F32) for _ in range(2)]
        for j, c in enumerate(range(0, nk, kc)):
            for h in range(2):
                p = jnp.exp2(s[h][j] - m[h]).astype(BF16)
                o[h] = o[h] + _dot(p, v_ref[0, c:c + kc, cols[h]])
        outs = [o[h][:, :MLA_V] / o[h][:, MLA_V:MLA_V + 1] for h in range(2)]
        o_ref[0] = jnp.concatenate(outs, axis=1).astype(o_ref.dtype)

    @pl.when(pl.program_id(2) == 0)
    def _():
        attend(n_ctx)

    @pl.when(pl.program_id(2) > 0)
    def _():
        attend(k_ref.shape[1])


def _attention(q, k, v, n_ctx):
    b, lt, _ = q.shape
    return pl.pallas_call(
        functools.partial(_attn_kernel, n_ctx=n_ctx),
        grid=(b, MLA_HEADS // 2, lt // TOK_TILE),
        in_specs=[
            pl.BlockSpec((1, TOK_TILE, 256), lambda i, h, t: (i, t, h)),
            pl.BlockSpec((1, lt, 256), lambda i, h, t: (i, 0, h)),
            pl.BlockSpec((1, lt, 256), lambda i, h, t: (i, 0, h)),
        ],
        out_specs=pl.BlockSpec((1, TOK_TILE, 2 * MLA_V), lambda i, h, t: (i, t, h)),
        out_shape=jax.ShapeDtypeStruct((b, lt, MLA_HEADS * MLA_V), BF16),
        compiler_params=_cp(("arbitrary", "arbitrary", "arbitrary")),
        name="mla_attention",
    )(q, k, v)


def _rev_tile(s, n):
    return jnp.where(s == 0, 0, n - s)


def _time_index(shape, dim, reverse):
    i = lax.broadcasted_iota(jnp.int32, shape, dim)
    return (shape[dim] - 1 - i) if reverse else i


def _blk(i, size):
    return lax.shift_right_logical(i, jnp.int32(int(math.log2(size))))


def _gla_chunks(items):
    c_len = items[0]["q"].shape[0]
    for it in items:
        g2 = it["g"] * LOG2E
        it["c"] = g2
        it["t"] = g2
        it["a"] = jnp.zeros((c_len, c_len), F32)
    hb = 1
    while hb < c_len:
        masks = {}
        for rev in {it["reverse"] for it in items}:
            ri = _time_index((c_len, c_len), 0, rev)
            ci = _time_index((c_len, c_len), 1, rev)
            pair = jnp.logical_and(_blk(ri, hb) == _blk(ci, hb) + 1, (_blk(ri, hb) & 1) == 1)
            odd = (_blk(_time_index((c_len, REC_D), 0, rev), hb) & 1) == 1
            masks[rev] = (pair, odd)
        for it in items:
            rev = it["reverse"]
            pair, odd = masks[rev]
            c, t = it["c"], it["t"]
            qt = (it["q"] * jnp.exp2(c)).astype(BF16)
            kt = (it["k"] * jnp.exp2(t - c)).astype(BF16)
            it["a"] = it["a"] + jnp.where(pair, _dot_nt(qt, kt), 0.0)
            t_lo = pltpu.roll(t, hb, 0)
            t_hi = pltpu.roll(t, c_len - hb, 0)
            prev, nxt = (t_hi, t_lo) if rev else (t_lo, t_hi)
            it["c"] = c + jnp.where(odd, prev, 0.0)
            it["t"] = t + jnp.where(odd, prev, nxt)
        hb *= 2
    outs = []
    for it in items:
        q, k, v, c, t = it["q"], it["k"], it["v"], it["c"], it["t"]
        st = it["st_ref"][...]
        qk = jnp.sum(q * k, axis=1, keepdims=True)
        outs.append(_dot_nt((q * jnp.exp2(c)).astype(BF16), st.astype(BF16))
                    + _dot(it["a"].astype(BF16), v.astype(BF16)) + qk * v)
        kd = (k * jnp.exp2(t - c)).astype(BF16)
        it["st_ref"][...] = st * jnp.exp2(t[0:1, :]) + _dot(v.T.astype(BF16), kd)
    return outs


def _hgrn2_kernel(qf_ref, vf_ref, ff_ref, qb_ref, vb_ref, fb_ref, lb_ref, of_ref, ob_ref, st_ref):
    @pl.when(pl.program_id(1) == 0)
    def _():
        st_ref[...] = jnp.zeros(st_ref.shape, F32)

    n_chunks = TOK_TILE // CHUNK

    def body(ci, carry):
        items, dests = [], []
        for d, (q_ref, v_ref, f_ref, o_ref) in enumerate(((qf_ref, vf_ref, ff_ref, of_ref),
                                                           (qb_ref, vb_ref, fb_ref, ob_ref))):
            cc = ci if d == 0 else n_chunks - 1 - ci
            rows = pl.ds(pl.multiple_of(cc * CHUNK, CHUNK), CHUNK)
            for h in range(REC_HEADS):
                cols = slice(h * REC_D, (h + 1) * REC_D)
                lb = lb_ref[d, :, cols]
                f = lb + (1.0 - lb) * _sigmoid(f_ref[0, rows, cols])
                items.append(dict(q=_silu(q_ref[0, rows, cols].astype(F32)), k=1.0 - f,
                                  v=v_ref[0, rows, cols].astype(F32),
                                  g=jnp.log(f), st_ref=st_ref.at[d, h], reverse=(d == 1)))
                dests.append((o_ref, rows, cols))
        for (o_ref, rows, cols), o in zip(dests, _gla_chunks(items)):
            o_ref[0, rows, cols] = o
        return carry

    lax.fori_loop(0, n_chunks, body, 0)


def _hgrn2(z_main, z_f, lb):
    b, lt, _ = z_main.shape
    n = lt // TOK_TILE
    fwd = lambda k: pl.BlockSpec((1, TOK_TILE, REC_W), lambda i, s: (i, s, k))
    bwd = lambda k: pl.BlockSpec((1, TOK_TILE, REC_W), lambda i, s: (i, _rev_tile(s, n), k))
    return pl.pallas_call(
        _hgrn2_kernel,
        grid=(b, n),
        in_specs=[fwd(ZM_Q), fwd(ZM_I), fwd(ZX_F // REC_W), bwd(ZM_Q), bwd(ZM_I), bwd(ZX_F // REC_W + 1),
                  pl.BlockSpec((2, 1, REC_W), lambda i, s: (0, 0, 0))],
        out_specs=[pl.BlockSpec((1, TOK_TILE, REC_W), lambda i, s: (i, s, 0)),
                   pl.BlockSpec((1, TOK_TILE, REC_W), lambda i, s: (i, _rev_tile(s, n), 0))],
        out_shape=[jax.ShapeDtypeStruct((b, lt, REC_W), F32)] * 2,
        scratch_shapes=[pltpu.VMEM((2, REC_HEADS, REC_D, REC_D), F32)],
        compiler_params=_cp(("arbitrary", "arbitrary")),
        name="hgrn2_scan",
    )(z_main, z_main, z_f, z_main, z_main, z_f, lb)


CONV_GAP = 8


def _gdn_prep_kernel(x_ref, w_ref, o_ref, pad_ref, *, n_ctx):
    lt = x_ref.shape[1]
    segs = ((0, n_ctx, CONV_GAP), (n_ctx, lt, 2 * CONV_GAP))
    gap = jnp.zeros((CONV_GAP, REC_D), F32)
    pad_ref[0:CONV_GAP, :] = gap
    pad_ref[n_ctx + CONV_GAP:n_ctx + 2 * CONV_GAP, :] = gap
    pad_ref[lt + 2 * CONV_GAP:lt + 3 * CONV_GAP, :] = gap
    for a, b, sh in segs:
        pad_ref[a + sh:b + sh, :] = x_ref[0, a:b, :]
    is_qk = pl.program_id(1) < 2 * REC_HEADS
    for a, b, sh in segs:
        acc = None
        for kk in range(CONV_K):
            s0 = a + sh + kk - CONV_K // 2
            tap = pad_ref[s0:s0 + (b - a), :] * w_ref[kk:kk + 1, :]
            acc = tap if acc is None else acc + tap
        y = _silu(acc)
        inv = lax.rsqrt(jnp.sum(y * y, axis=1, keepdims=True) + 1e-6)
        o_ref[0, a:b, :] = y * jnp.where(is_qk, inv, 1.0)


def _gdn_prep(z_gdn, conv_w, n_ctx):
    b, lt, _ = z_gdn.shape
    return pl.pallas_call(
        functools.partial(_gdn_prep_kernel, n_ctx=n_ctx),
        grid=(b, ZG_CONV // REC_D),
        in_specs=[pl.BlockSpec((1, lt, REC_D), lambda i, j: (i, 0, ZX_GDN // REC_D + j)),
                  pl.BlockSpec((CONV_K, REC_D), lambda i, j: (0, j))],
        out_specs=pl.BlockSpec((1, lt, REC_D), lambda i, j: (i, 0, j)),
        out_shape=jax.ShapeDtypeStruct((b, lt, ZG_CONV), F32),
        scratch_shapes=[pltpu.VMEM((lt + 3 * CONV_GAP, REC_D), F32)],
        compiler_params=_cp(("arbitrary", "arbitrary")),
        name="gdn_prep",
    )(z_gdn, conv_w)


def _gdn_tile(items):
    t_len = items[0]["q"].shape[0]
    n_chunks = t_len // CHUNK
    for it in items:
        ri = _time_index((t_len, t_len), 0, it["reverse"])
        ci = _time_index((t_len, t_len), 1, it["reverse"])
        it["ri"], it["ci"] = ri, ci
        same = _blk(ri, CHUNK) == _blk(ci, CHUNK)
        it["dec"] = jnp.where(jnp.logical_and(same, ri >= ci),
                              jnp.exp2(jnp.minimum(it["gcol"] - it["grow"], 0.0)), 0.0)
        it["kb"] = it["k"] * it["beta"]
        it["kbf"] = it["k"].astype(BF16)
    for it in items:
        ri, ci = it["ri"], it["ci"]
        lm = jnp.where(ri > ci, _dot_nt(it["kb"].astype(BF16), it["kbf"]) * it["dec"], 0.0)
        it["lm"] = lm
        it["tinv"] = jnp.where(ri == ci, 1.0, 0.0) - jnp.where(_blk(ri, 2) == _blk(ci, 2), lm, 0.0)
    hb = 2
    while hb < CHUNK:
        for it in items:
            ri, ci = it["ri"], it["ci"]
            off = jnp.where(_blk(ri, 2 * hb) == _blk(ci, 2 * hb),
                            jnp.where(_blk(ri, hb) == _blk(ci, hb), 0.0, it["lm"]), 0.0)
            it["tb"] = it["tinv"].astype(BF16)
            it["to"] = _dot(it["tb"], off.astype(BF16)).astype(BF16)
        for it in items:
            it["tinv"] = it["tinv"] - _dot(it["to"], it["tb"])
        hb *= 2
    for it in items:
        eg = jnp.exp2(it["gcol"])
        tb = it["tinv"].astype(BF16)
        it["u"] = _dot(tb, (it["v"] * it["beta"]).astype(BF16))
        it["w"] = _dot(tb, (it["kb"] * eg).astype(BF16)).astype(BF16)
        qs = it["q"] * (REC_D ** -0.5)
        it["aqk"] = (_dot_nt(qs.astype(BF16), it["kbf"]) * it["dec"]).astype(BF16)
        it["qd"] = (qs * eg).astype(BF16)
        kd = it["k"] * jnp.exp2(it["gtot"] - it["gcol"])
        it["kdt"] = [kd[c * CHUNK:(c + 1) * CHUNK].T.astype(BF16) for c in range(n_chunks)]
        it["s"] = it["s_ref"][...]
        it["o"], it["vn"] = [None] * n_chunks, [None] * n_chunks
    for step in range(n_chunks):
        for it in items:
            c = n_chunks - 1 - step if it["reverse"] else step
            it["c"] = c
            rows = slice(c * CHUNK, (c + 1) * CHUNK)
            it["sb"] = it["s"].astype(BF16)
            it["vn"][c] = (it["u"][rows] - _dot(it["w"][rows], it["sb"])).astype(BF16)
        for it in items:
            c = it["c"]
            rows = slice(c * CHUNK, (c + 1) * CHUNK)
            it["o"][c] = _dot(it["qd"][rows], it["sb"])
            it["s"] = (it["s"] * jnp.exp2(it["gtot"][c * CHUNK:c * CHUNK + 1])
                       + _dot(it["kdt"][c], it["vn"][c]))
    outs = []
    for it in items:
        it["s_ref"][...] = it["s"]
        outs.append(jnp.concatenate(it["o"], axis=0) + _dot(it["aqk"], jnp.concatenate(it["vn"], axis=0)))
    return outs


def _gdn_kernel(qf_ref, kf_ref, vf_ref, gf_ref, qb_ref, kb_ref, vb_ref, gb_ref, par_ref, of_ref, ob_ref, s_ref):
    @pl.when(pl.program_id(1) == 0)
    def _():
        s_ref[...] = jnp.zeros(s_ref.shape, F32)

    neg_a = -jnp.exp(par_ref[0:1, :])
    dt_bias = par_ref[1:2, :]
    r2 = lax.broadcasted_iota(jnp.int32, (TOK_TILE, TOK_TILE), 0)
    c2 = lax.broadcasted_iota(jnp.int32, (TOK_TILE, TOK_TILE), 1)
    same = _blk(r2, CHUNK) == _blk(c2, CHUNK)
    items, dests = [], []
    for d, (q_ref, k_ref, v_ref, g_ref, o_ref) in enumerate(((qf_ref, kf_ref, vf_ref, gf_ref, of_ref),
                                                              (qb_ref, kb_ref, vb_ref, gb_ref, ob_ref))):
        ab = g_ref[0]
        xa = ab + dt_bias
        g = (neg_a * LOG2E) * (jnp.maximum(xa, 0.0) + jnp.log(1.0 + jnp.exp(-jnp.abs(xa))))
        tri = jnp.where(jnp.logical_and(same, (r2 <= c2) if d == 1 else (r2 >= c2)), 1.0, 0.0)
        g_hi = g.astype(BF16)
        r1 = g - g_hi.astype(F32)
        g_mid = r1.astype(BF16)
        g_lo = (r1 - g_mid.astype(F32)).astype(BF16)
        sums = lambda m: _dot(m, g_hi) + (_dot(m, g_mid) + _dot(m, g_lo))
        gc = sums(tri.astype(BF16))
        gtot = sums(jnp.where(same, 1.0, 0.0).astype(BF16))
        gct = gc.T
        beta = _sigmoid(ab)
        for h in range(REC_HEADS):
            cols = slice(h * REC_D, (h + 1) * REC_D)
            ln = d * REC_HEADS + h
            items.append(dict(
                q=q_ref[0, :, cols], k=k_ref[0, :, cols], v=v_ref[0, :, cols],
                gcol=gc[:, ln:ln + 1], grow=gct[ln:ln + 1, :], gtot=gtot[:, ln:ln + 1],
                beta=beta[:, 2 * REC_HEADS + ln:2 * REC_HEADS + ln + 1], s_ref=s_ref.at[d, h],
                reverse=(d == 1)))
            dests.append((o_ref, cols))
    for (o_ref, cols), o in zip(dests, _gdn_tile(items)):
        o_ref[0, :, cols] = o


def _gdn(qkv, z_gdn, par):
    b, lt, _ = qkv.shape
    n = lt // TOK_TILE
    fwd = lambda k: pl.BlockSpec((1, TOK_TILE, REC_W), lambda i, s: (i, s, k))
    bwd = lambda k: pl.BlockSpec((1, TOK_TILE, REC_W), lambda i, s: (i, _rev_tile(s, n), k))
    gcol = (ZX_GDN + ZG_CONV) // 128
    return pl.pallas_call(
        _gdn_kernel,
        grid=(b, n),
        in_specs=[fwd(0), fwd(1), fwd(2), pl.BlockSpec((1, TOK_TILE, 128), lambda i, s: (i, s, gcol)),
                  bwd(0), bwd(1), bwd(2), pl.BlockSpec((1, TOK_TILE, 128), lambda i, s: (i, _rev_tile(s, n), gcol)),
                  pl.BlockSpec((8, 128), lambda i, s: (0, 0))],
        out_specs=[pl.BlockSpec((1, TOK_TILE, REC_W), lambda i, s: (i, s, 0)),
                   pl.BlockSpec((1, TOK_TILE, REC_W), lambda i, s: (i, _rev_tile(s, n), 0))],
        out_shape=[jax.ShapeDtypeStruct((b, lt, REC_W), F32)] * 2,
        scratch_shapes=[pltpu.VMEM((2, REC_HEADS, REC_D, REC_D), F32)],
        compiler_params=_cp(("arbitrary", "arbitrary")),
        name="gdn_scan",
    )(qkv, qkv, qkv, z_gdn, qkv, qkv, qkv, z_gdn, par)


def _layer_norm(x, g, b):
    mu = jnp.mean(x, axis=-1, keepdims=True)
    xc = x - mu
    var = jnp.mean(xc * xc, axis=-1, keepdims=True)
    return xc * lax.rsqrt(var + LN_EPS) * g + b


def _head_norm_gate(o, gate, w):
    outs = []
    for h in range(REC_HEADS):
        cols = slice(h * REC_D, (h + 1) * REC_D)
        oh = o[:, cols]
        n = oh * lax.rsqrt(jnp.mean(oh * oh, axis=-1, keepdims=True) + RMS_EPS) * w
        outs.append(n * _silu(gate[:, cols].astype(F32)))
    return jnp.concatenate(outs, axis=1)


def _merge_kernel(mla_ref, hf_ref, hb_ref, gf_ref, gb_ref, gates_ref, hgate_ref, ggate_ref, x_ref,
                  m2_ref, m3_ref, m4_ref, hw_ref, gw_ref, wb_ref, wo_ref, lg_ref, lbias_ref, wr_ref,
                  x1_ref, h2_ref, lt_ref):
    hg = _head_norm_gate(hf_ref[0] + hb_ref[0], hgate_ref[0], hw_ref[...]).astype(BF16)
    gd = _head_norm_gate(gf_ref[0] + gb_ref[0], ggate_ref[0], gw_ref[...]).astype(BF16)
    y = jnp.zeros((TOK_TILE, D_MODEL), F32)
    for n, o in enumerate((mla_ref[0], hg, gd)):
        y = y + _sigmoid(gates_ref[0, :, n * D_MODEL:(n + 1) * D_MODEL].astype(F32)) * _dot(o, wb_ref[n])
    y = _dot(y.astype(BF16), wo_ref[...])
    x1 = _layer_norm(DEEPNORM_ALPHA * x_ref[0] + m2_ref[0] * y, lg_ref[...], lbias_ref[...])
    x1_ref[0] = x1
    h2 = x1 * (1.0 + m4_ref[0]) + m3_ref[0]
    for p, part in enumerate(_pack_parts(h2)):
        h2_ref[p, 0] = part
    wr = wr_ref[...]
    wr_hi = wr.astype(BF16)
    wr_lo = (wr - wr_hi.astype(F32)).astype(BF16)
    h_hi = h2.astype(BF16)
    h_lo = (h2 - h_hi.astype(F32)).astype(BF16)
    lt_ref[...] = _dot_nt(wr_hi, h_hi) + (_dot_nt(wr_hi, h_lo) + _dot_nt(wr_lo, h_hi))


def _merge(mla_o, hg_f, hg_b, gd_f, gd_b, z_main, x_all, mod_l, hg_w, gdn_w, wb, wo, ln_g, ln_b, wr_t):
    b, lt, d = x_all.shape
    n = lt // TOK_TILE
    tok = lambda w, k=0: pl.BlockSpec((1, TOK_TILE, w), lambda i, t: (i, t, k))
    full = lambda a: pl.BlockSpec(a.shape, lambda i, t: (0,) * a.ndim)
    return pl.pallas_call(
        _merge_kernel,
        grid=(b, n),
        in_specs=[tok(BRANCH_W), tok(REC_W), tok(REC_W), tok(REC_W), tok(REC_W),
                  tok(N_BRANCH * D_MODEL, 0), tok(REC_W, ZM_HGATE), tok(REC_W, ZM_GGATE),
                  tok(d), _mod_spec(b, 2), _mod_spec(b, 3), _mod_spec(b, 4),
                  full(hg_w), full(gdn_w), full(wb), full(wo), full(ln_g), full(ln_b), full(wr_t)],
        out_specs=[tok(d), pl.BlockSpec((ROW_PARTS, 1, TOK_TILE, SC_ROW), lambda i, t: (0, i, t, 0)),
                   pl.BlockSpec((N_EXPERTS, TOK_TILE), lambda i, t: (0, i * n + t))],
        out_shape=[jax.ShapeDtypeStruct((b, lt, d), F32),
                   jax.ShapeDtypeStruct((ROW_PARTS, b, lt, SC_ROW), jnp.int32),
                   jax.ShapeDtypeStruct((N_EXPERTS, b * lt), F32)],
        compiler_params=_cp(("arbitrary", "arbitrary")),
        name="merge",
    )(mla_o, hg_f, hg_b, gd_f, gd_b, z_main, z_main, z_main, x_all, mod_l, mod_l, mod_l,
      hg_w, gdn_w, wb, wo, ln_g, ln_b, wr_t)


def _first_max(x, idx, axes):
    m = x
    for ax in axes:
        m = jnp.max(m, axis=ax, keepdims=True)
    first = jnp.where(x == m, idx, jnp.int32(2 ** 30))
    for ax in axes:
        first = jnp.min(first, axis=ax, keepdims=True)
    return m, first


def _route_kernel(lt_ref, bias_ref, e_ref, r_ref, w_ref, cnt_ref, run_ref):
    @pl.when(pl.program_id(0) == 0)
    def _():
        run_ref[...] = jnp.zeros(run_ref.shape, F32)

    n_tok = lt_ref.shape[1]
    per = N_EXPERTS // N_GROUPS
    scores = _sigmoid(lt_ref[...]).reshape(N_GROUPS, per, n_tok)
    sel = scores + bias_ref[...]
    ig = lax.broadcasted_iota(jnp.int32, sel.shape, 0)
    ij = lax.broadcasted_iota(jnp.int32, sel.shape, 1)
    top1, a1 = _first_max(sel, ij, (1,))
    top2 = jnp.max(jnp.where(ij == a1, -jnp.inf, sel), axis=1, keepdims=True)
    grp = top1 + top2
    igg = lax.broadcasted_iota(jnp.int32, grp.shape, 0)
    gsel = jnp.zeros(grp.shape, F32)
    for _ in range(TOPK_GROUPS):
        _, a = _first_max(grp, igg, (0,))
        hit = igg == a
        gsel = jnp.where(hit, 1.0, gsel)
        grp = jnp.where(hit, -jnp.inf, grp)
    cur = jnp.where(gsel > 0.5, sel, -jnp.inf)
    ie = ig * per + ij
    esel = jnp.zeros(cur.shape, F32)
    for _ in range(TOP_K):
        _, a = _first_max(cur, ie, (1, 0))
        hit = ie == a
        esel = jnp.where(hit, 1.0, esel)
        cur = jnp.where(hit, -jnp.inf, cur)
    w = scores * esel
    tot = jnp.sum(jnp.sum(w, axis=1, keepdims=True), axis=0, keepdims=True)
    w = (w / tot * ROUTED_SCALE).reshape(N_EXPERTS, n_tok)
    m = esel.reshape(N_EXPERTS, n_tok)
    mb = m.astype(BF16)
    ti = lax.broadcasted_iota(jnp.int32, (n_tok, n_tok), 0)
    tj = lax.broadcasted_iota(jnp.int32, (n_tok, n_tok), 1)
    upto = _dot(mb, jnp.where(ti <= tj, 1.0, 0.0).astype(BF16))
    ei = lax.broadcasted_iota(jnp.int32, (N_EXPERTS, N_EXPERTS), 0)
    ej = lax.broadcasted_iota(jnp.int32, (N_EXPERTS, N_EXPERTS), 1)
    lower = _dot(jnp.where(ej < ei, 1.0, 0.0).astype(BF16), mb)
    run = run_ref[:, 0:1]
    rank = run + upto - 1.0
    run_ref[...] = jnp.broadcast_to(run + upto[:, n_tok - 1:n_tok], run_ref.shape)
    cnt_ref[...] = run_ref[...]
    eid = lax.broadcasted_iota(jnp.int32, m.shape, 0).astype(F32)
    rows_e, rows_r, rows_w = [], [], []
    for k in range(TOP_K):
        pick = jnp.where(lower == float(k), m, 0.0)
        rows_e.append(jnp.sum(pick * eid, axis=0, keepdims=True))
        rows_r.append(jnp.sum(pick * rank, axis=0, keepdims=True))
        rows_w.append(jnp.sum(pick * w, axis=0, keepdims=True))
    e_ref[...] = jnp.concatenate(rows_e, axis=0).astype(jnp.int32)
    r_ref[...] = jnp.concatenate(rows_r, axis=0).astype(jnp.int32)
    w_ref[...] = jnp.concatenate(rows_w, axis=0).T


ROUTE_TILE = 512


def _route(logits_t, bias):
    n_tok = logits_t.shape[1]
    tt = ROUTE_TILE
    return pl.pallas_call(
        _route_kernel,
        grid=(n_tok // tt,),
        in_specs=[pl.BlockSpec((N_EXPERTS, tt), lambda i: (0, i)),
                  pl.BlockSpec((N_GROUPS, N_EXPERTS // N_GROUPS, 1), lambda i: (0, 0, 0))],
        out_specs=[pl.BlockSpec((TOP_K, tt), lambda i: (0, i)),
                   pl.BlockSpec((TOP_K, tt), lambda i: (0, i)),
                   pl.BlockSpec((tt, TOP_K), lambda i: (i, 0)),
                   pl.BlockSpec((N_EXPERTS, 128), lambda i: (0, 0))],
        out_shape=[jax.ShapeDtypeStruct((TOP_K, n_tok), jnp.int32),
                   jax.ShapeDtypeStruct((TOP_K, n_tok), jnp.int32),
                   jax.ShapeDtypeStruct((n_tok, TOP_K), F32),
                   jax.ShapeDtypeStruct((N_EXPERTS, 128), F32)],
        scratch_shapes=[pltpu.VMEM((N_EXPERTS, 128), F32)],
        compiler_params=_cp(("arbitrary",)),
        name="route",
    )(logits_t, bias)


def _slot_kernel(start_ref, e_ref, r_ref, p_ref, *, n_rows):
    e = e_ref[...]
    pos = r_ref[...]
    for j in range(N_EXPERTS):
        pos = pos + jnp.where(e == j, start_ref[j], 0)
    for p in range(ROW_PARTS):
        p_ref[p] = pos + p * n_rows


def _slots(starts, e_k, r_k, n_rows):
    n_tok = e_k.shape[1]
    tt = ROUTE_TILE
    spec = pl.BlockSpec((TOP_K, tt), lambda i, s: (0, i))
    return pl.pallas_call(
        functools.partial(_slot_kernel, n_rows=n_rows),
        grid_spec=pltpu.PrefetchScalarGridSpec(
            num_scalar_prefetch=1, grid=(n_tok // tt,), in_specs=[spec, spec],
            out_specs=pl.BlockSpec((ROW_PARTS, TOP_K, tt), lambda i, s: (0, 0, i))),
        out_shape=jax.ShapeDtypeStruct((ROW_PARTS, TOP_K, n_tok), jnp.int32),
        compiler_params=_cp(("arbitrary",)),
        name="moe_slots",
    )(starts, e_k, r_k)


def _sc_mesh():
    return plsc.VectorSubcoreMesh(core_axis_name="c", subcore_axis_name="s")


def _sc_scatter(x, idx, n_out):
    parts, reps, t = idx.shape
    nbt = t // SC_WINDOW

    @functools.partial(pl.kernel, out_type=jax.ShapeDtypeStruct((n_out, SC_ROW), x.dtype), mesh=_sc_mesh())
    def k(x_hbm, i_hbm, o_hbm):
        def body(x_vmem, i_vmem):
            for r in range(reps):
                pltpu.sync_copy(x_vmem, o_hbm.at[i_vmem.at[0, r]])

        pltpu.emit_pipeline(
            body,
            grid=(parts * nbt,),
            in_specs=[pl.BlockSpec((SC_WINDOW, SC_ROW), lambda i: (i, 0)),
                      pl.BlockSpec((1, reps, SC_WINDOW), lambda i: (i // nbt, 0, i % nbt))],
            out_specs=[],
            core_axis_name=("c", "s"),
            dimension_semantics=(pltpu.PARALLEL,),
        )(x_hbm, i_hbm)

    return k(x, idx)


def _sc_gather(table, idx):
    n = idx.shape[0]

    @functools.partial(pl.kernel, out_type=jax.ShapeDtypeStruct((n, SC_ROW), table.dtype), mesh=_sc_mesh())
    def k(x_hbm, i_hbm, o_hbm):
        def body(i_vmem, o_vmem):
            pltpu.sync_copy(x_hbm.at[i_vmem.at[0]], o_vmem)

        pltpu.emit_pipeline(
            body,
            grid=(n // SC_WINDOW,),
            in_specs=[pl.BlockSpec((1, SC_WINDOW), lambda i: (0, i))],
            out_specs=[pl.BlockSpec((SC_WINDOW, SC_ROW), lambda i: (i, 0))],
            core_axis_name=("c", "s"),
            dimension_semantics=(pltpu.PARALLEL,),
        )(i_hbm, o_hbm)

    return k(table, idx.reshape(1, n))


MOE_TM = 512


def _expert_kernel(te_ref, tr_ref, x_ref, wgu_ref, wd_ref, y_ref, wgu_bf, wd_bf):
    i = pl.program_id(0)
    valid = tr_ref[i]

    @pl.when(jnp.logical_or(i == 0, te_ref[i] != te_ref[jnp.maximum(i - 1, 0)]))
    def _():
        wgu_bf[...] = wgu_ref[0].astype(BF16)
        wd_bf[...] = wd_ref[0].astype(BF16)

    @pl.when(valid > 0)
    def _():
        live = lax.broadcasted_iota(jnp.int32, x_ref.shape[1:], 0) < valid
        x = _unpack_parts([jnp.where(live, x_ref[p], 0) for p in range(ROW_PARTS)])
        gu = _dot(x, wgu_bf[...])
        act = _silu(gu[:, :EXPERT_FF]) * gu[:, EXPERT_FF:]
        for p, part in enumerate(_pack_parts(_dot(act.astype(BF16), wd_bf[...]))):
            y_ref[p] = part

    @pl.when(valid == 0)
    def _():
        y_ref[...] = jnp.zeros(y_ref.shape, y_ref.dtype)


def _experts(tile_expert, tile_rows, xs, wgu, wd, layer):
    _, n_rows, _ = xs.shape
    d = wgu.shape[2]
    rows = pl.BlockSpec((ROW_PARTS, MOE_TM, SC_ROW), lambda i, te, tr: (0, i, 0))
    return pl.pallas_call(
        _expert_kernel,
        grid_spec=pltpu.PrefetchScalarGridSpec(
            num_scalar_prefetch=2,
            grid=(n_rows // MOE_TM,),
            in_specs=[rows,
                      pl.BlockSpec((None, 1, d, 2 * EXPERT_FF), lambda i, te, tr: (layer, te[i], 0, 0)),
                      pl.BlockSpec((None, 1, EXPERT_FF, d), lambda i, te, tr: (layer, te[i], 0, 0))],
            out_specs=rows,
            scratch_shapes=[pltpu.VMEM((d, 2 * EXPERT_FF), BF16), pltpu.VMEM((EXPERT_FF, d), BF16)]),
        out_shape=jax.ShapeDtypeStruct(xs.shape, jnp.int32),
        compiler_params=_cp(("arbitrary",)),
        name="moe_experts",
    )(tile_expert, tile_rows, xs, wgu, wd)


def _combine_kernel(g_ref, w_ref, h_ref, x_ref, m5_ref, wsgu_ref, wsd_ref, lg_ref, lb_ref, *rest):
    h = _unpack_parts([h_ref[p, 0] for p in range(ROW_PARTS)])
    gu = _dot(h, wsgu_ref[...])
    act = _silu(gu[:, :SHARED_FF]) * gu[:, SHARED_FF:]
    f = _dot(act.astype(BF16), wsd_ref[...])
    w = w_ref[...]
    for k in range(TOP_K):
        f = f + w[:, k:k + 1] * _unpack_parts([g_ref[p, k] for p in range(ROW_PARTS)]).astype(F32)
    x_new = _layer_norm(DEEPNORM_ALPHA * x_ref[0] + m5_ref[0] * f, lg_ref[...], lb_ref[...])
    if len(rest) == 1:
        rest[0][0] = x_new
    else:
        sh_ref, sc_ref, o_ref, hn_ref = rest
        o_ref[0] = x_new
        hn_ref[0] = (x_new * (1.0 + sc_ref[0]) + sh_ref[0]).astype(hn_ref.dtype)


def _combine(g, w_k, h2p, x1, mod_l, wsgu, wsd, ln_g, ln_b, mod_next):
    b, lt, d = x1.shape
    n = lt // TOK_TILE
    tok = pl.BlockSpec((1, TOK_TILE, d), lambda i, t: (i, t, 0))
    full = lambda a: pl.BlockSpec(a.shape, lambda i, t: (0,) * a.ndim)
    in_specs = [pl.BlockSpec((ROW_PARTS, TOP_K, TOK_TILE, SC_ROW), lambda i, t: (0, 0, i * n + t, 0)),
                pl.BlockSpec((TOK_TILE, TOP_K), lambda i, t: (i * n + t, 0)),
                pl.BlockSpec((ROW_PARTS, 1, TOK_TILE, SC_ROW), lambda i, t: (0, i, t, 0)),
                tok, _mod_spec(b, 5), full(wsgu), full(wsd), full(ln_g), full(ln_b)]
    args = [g, w_k, h2p, x1, mod_l, wsgu, wsd, ln_g, ln_b]
    out_specs, out_shape = tok, jax.ShapeDtypeStruct((b, lt, d), F32)
    if mod_next is not None:
        in_specs += [_mod_spec(b, 0), _mod_spec(b, 1)]
        args += [mod_next, mod_next]
        out_specs, out_shape = [tok, tok], [out_shape, jax.ShapeDtypeStruct((b, lt, d), BF16)]
    return pl.pallas_call(
        _combine_kernel,
        grid=(b, n),
        in_specs=in_specs,
        out_specs=out_specs,
        out_shape=out_shape,
        compiler_params=_cp(("arbitrary", "arbitrary")),
        name="moe_combine",
    )(*args)


def _moe_sparse(h2p, logits_t, bias, wgu, wd, layer, wsgu, wsd, x1, mod_l, ln_g, ln_b, mod_next):
    n_tok = h2p.shape[1] * h2p.shape[2]
    e_k, r_k, w_k, counts = _route(logits_t, bias)
    counts = counts[:, 0].astype(jnp.int32)
    padded = (counts + MOE_TM - 1) // MOE_TM * MOE_TM
    ends = jnp.cumsum(padded)
    starts = ends - padded
    n_rows = (n_tok * TOP_K + N_EXPERTS * (MOE_TM - 1)) // MOE_TM * MOE_TM
    tile0 = jnp.arange(n_rows // MOE_TM, dtype=jnp.int32) * MOE_TM
    tile_expert = jnp.minimum(jnp.sum(tile0[:, None] >= ends[None, :], axis=1), N_EXPERTS - 1).astype(jnp.int32)
    inside = jnp.logical_and(tile0[:, None] >= starts[None, :], tile0[:, None] < ends[None, :])
    left = jnp.clip(counts[None, :] - (tile0[:, None] - starts[None, :]), 0, MOE_TM)
    tile_rows = jnp.sum(jnp.where(inside, left, 0), axis=1).astype(jnp.int32)
    idx = _slots(starts.astype(jnp.int32), e_k, r_k, n_rows)
    xs = _sc_scatter(h2p.reshape(ROW_PARTS * n_tok, SC_ROW), idx, ROW_PARTS * n_rows)
    ys = _experts(tile_expert, tile_rows, xs.reshape(ROW_PARTS, n_rows, SC_ROW), wgu, wd, layer)
    g = _sc_gather(ys.reshape(ROW_PARTS * n_rows, SC_ROW), idx.reshape(-1))
    g = g.reshape(ROW_PARTS, TOP_K, n_tok, SC_ROW)
    return _combine(g, w_k, h2p, x1, mod_l, wsgu, wsd, ln_g, ln_b, mod_next)


def _rope_tables(n_ctx, n_lat):
    rows = n_lat // GRID_W
    row = jnp.broadcast_to(jnp.arange(rows, dtype=F32)[:, None], (rows, GRID_W)).reshape(-1)
    col = jnp.broadcast_to(jnp.arange(GRID_W, dtype=F32)[None, :], (rows, GRID_W)).reshape(-1)
    inv = ROPE_BASE ** (-jnp.arange(ROPE_FREQS, dtype=F32) / ROPE_FREQS)
    ang = jnp.stack([row[:, None] * inv, col[:, None] * inv], axis=1)
    cos = jnp.cos(ang)[:, :, None, :]
    sin = jnp.sin(ang)[:, :, None, :]
    cos32 = jnp.broadcast_to(cos, (n_lat, 2, 2, ROPE_FREQS)).reshape(n_lat, MLA_ROPE)
    sin32 = jnp.concatenate([-sin, sin], axis=2).reshape(n_lat, MLA_ROPE)
    cos32 = jnp.concatenate([jnp.ones((n_ctx, MLA_ROPE), F32), cos32], axis=0)
    sin32 = jnp.concatenate([jnp.zeros((n_ctx, MLA_ROPE), F32), sin32], axis=0)
    lt = n_ctx + n_lat
    ca = jnp.concatenate([jnp.ones((lt, MLA_NOPE), F32), cos32, jnp.zeros((lt, 32), F32)], axis=1)
    sb = jnp.concatenate([jnp.zeros((lt, MLA_NOPE), F32), sin32, jnp.zeros((lt, 32), F32)], axis=1)
    return ca, sb


def _rope_partner():
    idx = np.arange(MLA_ROPE).reshape(2, 2, ROPE_FREQS)
    return idx[:, ::-1, :].reshape(-1)


IN_SIZES = (MLA_Q_LORA, MLA_KV_LORA + MLA_ROPE, REC_W, REC_W, REC_W, REC_W, REC_W,
            3 * REC_W, REC_W, 2 * REC_HEADS, 2 * REC_HEADS, N_BRANCH * D_MODEL)
IN_OFFS = tuple(int(o) for o in np.cumsum((0,) + IN_SIZES))
REPACK_ROWS = 128


def _repack_kernel(w_ref, main_ref, aux_ref):
    rows = w_ref.shape[1]
    seg = lambda i: w_ref[0, :, IN_OFFS[i]:IN_OFFS[i + 1]]
    zeros = lambda n: jnp.zeros((rows, n), F32)
    k_rope = w_ref[0, :, IN_OFFS[1] + MLA_KV_LORA:IN_OFFS[2]]
    h = ROPE_FREQS
    k_partner = jnp.concatenate([k_rope[:, h:2 * h], k_rope[:, :h], k_rope[:, 3 * h:], k_rope[:, 2 * h:3 * h]], axis=1)
    main_ref[0] = jnp.concatenate([seg(11), seg(2), seg(3), seg(6), seg(8)], axis=1).astype(BF16)
    aux_ref[0] = jnp.concatenate([seg(0), w_ref[0, :, IN_OFFS[1]:IN_OFFS[1] + MLA_KV_LORA],
                                  zeros(MLA_NOPE), k_rope, zeros(32),
                                  zeros(MLA_NOPE), k_partner, zeros(32), zeros(ZX_F - ZA_WIDTH),
                                  seg(4), seg(5),
                                  seg(7), seg(9), seg(10), zeros(256 - 4 * REC_HEADS)], axis=1).astype(BF16)


def _repack_w_in(w_in):
    depth, d, width = w_in.shape
    widths = (N_BRANCH * D_MODEL + 4 * REC_W, ZX_WIDTH)
    return pl.pallas_call(
        _repack_kernel,
        grid=(depth, d // REPACK_ROWS),
        in_specs=[pl.BlockSpec((1, REPACK_ROWS, width), lambda l, r: (l, r, 0))],
        out_specs=[pl.BlockSpec((1, REPACK_ROWS, w), lambda l, r: (l, r, 0)) for w in widths],
        out_shape=[jax.ShapeDtypeStruct((depth, d, w), BF16) for w in widths],
        compiler_params=_cp(("arbitrary", "arbitrary")),
        name="repack_w_in",
    )(w_in)


def _pack_mla(w_q_b, w_kv_b):
    partner = _rope_partner()
    r = w_q_b.shape[0]
    qb = w_q_b.reshape(r, MLA_HEADS, MLA_NOPE + MLA_ROPE)
    zq = lambda n: jnp.zeros((r, MLA_HEADS, n), w_q_b.dtype)
    wq1 = jnp.concatenate([qb, zq(32)], axis=2).reshape(r, MLA_HEADS * 128)
    wq2 = jnp.concatenate([zq(MLA_NOPE), qb[:, :, MLA_NOPE:][:, :, partner], zq(32)], axis=2).reshape(r, MLA_HEADS * 128)
    rk = w_kv_b.shape[0]
    kvb = w_kv_b.reshape(rk, MLA_HEADS, MLA_NOPE + MLA_V)
    wk = jnp.concatenate([kvb[:, :, :MLA_NOPE], jnp.zeros((rk, MLA_HEADS, 64), w_kv_b.dtype)], axis=2)
    wk = wk.reshape(rk, MLA_HEADS * 128)
    wv = jnp.concatenate([kvb[:, :, MLA_NOPE:], jnp.zeros((rk, MLA_HEADS, 128 - MLA_V), w_kv_b.dtype)], axis=2)
    wv = wv.reshape(rk, MLA_HEADS * 128)
    bf = lambda a: a.astype(BF16)
    return bf(wq1), bf(wq2), bf(wk), bf(wv)


def kernel(x, c, ctx, c_ctx, w_mod, b_mod, w_in, q_a_norm, w_q_b, kv_a_norm, w_kv_b, hg_lb_logits, hg_norm,
           gdn_conv, gdn_a_log, gdn_dt_bias, gdn_norm, w_branch, w_out, ln1_g, ln1_b, ln2_g, ln2_b,
           w_router, router_bias, w_gu, w_down, w_sh_gu, w_sh_down):
    batch, n_lat, d = x.shape
    n_ctx = ctx.shape[1]
    assert n_ctx == TOK_TILE and n_lat % TOK_TILE == 0 and batch < MOD_ROWS and d == D_MODEL
    lt = n_ctx + n_lat

    c_all = jnp.zeros((MOD_ROWS, d), F32).at[:batch].set(c).at[batch].set(c_ctx)
    mod = _mod_all(c_all, w_mod, b_mod).reshape(DEPTH, MOD_ROWS, 1, 6 * d)
    ca, sb = _rope_tables(n_ctx, n_lat)
    lb_soft = jax.nn.softmax(hg_lb_logits.astype(F32), axis=0)
    lower = (jnp.cumsum(lb_soft, axis=0) - lb_soft[0]).reshape(DEPTH, 2, 1, REC_W)
    row = lambda a: a.reshape(1, -1)

    w_main, w_aux = _repack_w_in(w_in)
    x_all = jnp.concatenate([ctx, x], axis=1)
    h = _modulate(x_all, mod[0])
    for l in range(DEPTH):
        wq1, wq2, wk, wv = _pack_mla(w_q_b[l], w_kv_b[l])
        mod_l = mod[l]
        z_main = _proj(h, w_main, l, 2 * REC_W, BF16)
        z_f = z_mla = z_gdn = _proj(h, w_aux, l, ZX_WIDTH // 3)

        q, k, v = _mla_prep(z_mla, row(q_a_norm[l]), row(kv_a_norm[l]), wq1, wq2, wk, wv, ca, sb)
        mla_o = _attention(q, k, v, n_ctx)

        hg_f, hg_b = _hgrn2(z_main, z_f, lower[l])

        qkv = _gdn_prep(z_gdn, gdn_conv[l], n_ctx)
        par = jnp.zeros((8, 128), F32)
        par = par.at[0, :2 * REC_HEADS].set(gdn_a_log[l].reshape(-1))
        par = par.at[1, :2 * REC_HEADS].set(gdn_dt_bias[l].reshape(-1))
        gd_f, gd_b = _gdn(qkv, z_gdn, par)

        x1, h2, logits_t = _merge(mla_o, hg_f, hg_b, gd_f, gd_b, z_main, x_all, mod_l,
                                  row(hg_norm[l]), row(gdn_norm[l]), w_branch[l].astype(BF16),
                                  w_out[l].astype(BF16), row(ln1_g[l]), row(ln1_b[l]), w_router[l].T)
        out = _moe_sparse(h2, logits_t,
                          router_bias[l].reshape(N_GROUPS, N_EXPERTS // N_GROUPS, 1),
                          w_gu, w_down, l,
                          w_sh_gu[l].astype(BF16), w_sh_down[l].astype(BF16),
                          x1, mod_l, row(ln2_g[l]), row(ln2_b[l]), mod[l + 1] if l + 1 < DEPTH else None)
        x_all, h = out if l + 1 < DEPTH else (out, None)
    return x_all[:, n_ctx:, :]
```

```python
import functools
import math

import numpy as np
import jax
import jax.numpy as jnp
from jax import lax
from jax.experimental import pallas as pl
from jax.experimental.pallas import tpu as pltpu
from jax.experimental.pallas import tpu_sc as plsc

F32 = jnp.float32
BF16 = jnp.bfloat16
HIGHEST = lax.Precision.HIGHEST
LOG2E = math.log2(math.e)

D_MODEL = 1024
DEPTH = 4
GRID_W = 64
MLA_HEADS = 8
MLA_Q_LORA = 384
MLA_KV_LORA = 256
MLA_NOPE = 64
MLA_ROPE = 32
MLA_V = 64
MLA_SCALE = (MLA_NOPE + MLA_ROPE) ** -0.5
ROPE_BASE = 10000.0
ROPE_FREQS = MLA_ROPE // 4
REC_HEADS = 4
REC_D = 128
REC_W = REC_HEADS * REC_D
CONV_K = 5
N_BRANCH = 3
BRANCH_W = 512
N_EXPERTS = 64
TOP_K = 8
N_GROUPS = 8
TOPK_GROUPS = 4
EXPERT_FF = 256
SHARED_FF = 256
ROUTED_SCALE = 2.5
DEEPNORM_ALPHA = (2 * DEPTH) ** 0.25
LN_EPS = 1e-6
RMS_EPS = 1e-6

TOK_TILE = 256
CHUNK = 64
MOD_ROWS = 16
VMEM_LIMIT = 56 * 1024 * 1024

ZM_Q, ZM_I, ZM_HGATE, ZM_GGATE = (3 * D_MODEL // REC_W + k for k in range(4))
ZA_WIDTH = MLA_Q_LORA + MLA_KV_LORA + 256
ZG_CONV = 3 * REC_W
ZG_WIDTH = ZG_CONV + 256
ZX_F = 1024
ZX_GDN = ZX_F + 2 * REC_W
ZX_WIDTH = ZX_GDN + ZG_WIDTH


def _cp(sem, vmem=VMEM_LIMIT):
    return pltpu.CompilerParams(dimension_semantics=sem, vmem_limit_bytes=vmem)


def _dot(a, b, precision=None):
    return jnp.dot(a, b, preferred_element_type=F32, precision=precision)


def _dot_nt(a, b, precision=None):
    return lax.dot_general(a, b, (((1,), (1,)), ((), ())), preferred_element_type=F32, precision=precision)


def _sigmoid(x):
    return 1.0 / (1.0 + jnp.exp(-x))


def _silu(x):
    return x * _sigmoid(x)


ROW_PARTS = 2
SC_ROW = D_MODEL // (2 * ROW_PARTS)
SC_WINDOW = 128


def _pack_parts(x):
    q = x.shape[1] // (2 * ROW_PARTS)
    bits = lambda a: lax.bitcast_convert_type(a.astype(BF16).astype(F32), jnp.uint32)
    parts = []
    for p in range(ROW_PARTS):
        hi = bits(x[:, p * q:(p + 1) * q])
        lo = bits(x[:, (ROW_PARTS + p) * q:(ROW_PARTS + p + 1) * q])
        parts.append(lax.bitcast_convert_type(hi | lax.shift_right_logical(lo, jnp.uint32(16)), jnp.int32))
    return parts


def _unpack_parts(parts, dtype=BF16):
    his, los = [], []
    for p in parts:
        u = lax.bitcast_convert_type(p, jnp.uint32)
        his.append(lax.bitcast_convert_type(u & jnp.uint32(0xFFFF0000), F32).astype(dtype))
        los.append(lax.bitcast_convert_type(lax.shift_left(u, jnp.uint32(16)), F32).astype(dtype))
    return jnp.concatenate(his + los, axis=1)


def _mod_kernel(c_ref, w_ref, b_ref, o_ref):
    s = _silu(c_ref[...])
    o_ref[0] = _dot(s, w_ref[0], HIGHEST) + b_ref[0]


def _mod_all(c_all, w_mod, b_mod):
    tn = 1024
    n = w_mod.shape[-1]
    return pl.pallas_call(
        _mod_kernel,
        grid=(DEPTH, n // tn),
        in_specs=[
            pl.BlockSpec((MOD_ROWS, D_MODEL), lambda l, j: (0, 0)),
            pl.BlockSpec((1, D_MODEL, tn), lambda l, j: (l, 0, j)),
            pl.BlockSpec((1, 1, tn), lambda l, j: (l, 0, j)),
        ],
        out_specs=pl.BlockSpec((1, MOD_ROWS, tn), lambda l, j: (l, 0, j)),
        out_shape=jax.ShapeDtypeStruct((DEPTH, MOD_ROWS, n), F32),
        compiler_params=_cp(("arbitrary", "arbitrary")),
        name="mod_all",
    )(c_all, w_mod, b_mod.reshape(DEPTH, 1, n))


def _mod_spec(batch, k):
    return pl.BlockSpec((1, 1, D_MODEL), lambda b, t: (jnp.where(t == 0, batch, b), 0, k))


def _modulate_kernel(x_ref, sh_ref, sc_ref, o_ref):
    o_ref[0] = (x_ref[0] * (1.0 + sc_ref[0]) + sh_ref[0]).astype(o_ref.dtype)


def _modulate(x_all, mod_l):
    b, lt, d = x_all.shape
    return pl.pallas_call(
        _modulate_kernel,
        grid=(b, lt // TOK_TILE),
        in_specs=[
            pl.BlockSpec((1, TOK_TILE, d), lambda i, t: (i, t, 0)),
            _mod_spec(b, 0),
            _mod_spec(b, 1),
        ],
        out_specs=pl.BlockSpec((1, TOK_TILE, d), lambda i, t: (i, t, 0)),
        out_shape=jax.ShapeDtypeStruct((b, lt, d), BF16),
        compiler_params=_cp(("arbitrary", "arbitrary")),
        name="modulate",
    )(x_all, mod_l, mod_l)


PROJ_ROWS = 256


def _proj_kernel(h_ref, w_ref, o_ref, *, rows):
    def body(r, carry):
        sl = pl.ds(pl.multiple_of(r * rows, rows), rows)
        o_ref[0, sl, :] = _dot(h_ref[0, sl, :], w_ref[...]).astype(o_ref.dtype)
        return carry

    lax.fori_loop(0, h_ref.shape[1] // rows, body, 0)


def _proj(h, w, layer, tn, dtype=F32):
    b, lt, d = h.shape
    n = w.shape[2]
    return pl.pallas_call(
        functools.partial(_proj_kernel, rows=PROJ_ROWS),
        grid=(b, n // tn),
        in_specs=[
            pl.BlockSpec((1, lt, d), lambda i, j: (i, 0, 0)),
            pl.BlockSpec((None, d, tn), lambda i, j: (layer, 0, j)),
        ],
        out_specs=pl.BlockSpec((1, lt, tn), lambda i, j: (i, 0, j)),
        out_shape=jax.ShapeDtypeStruct((b, lt, n), dtype),
        compiler_params=_cp(("arbitrary", "arbitrary")),
        name="proj",
    )(h, w)


def _rms(x, g):
    return x * lax.rsqrt(jnp.mean(x * x, axis=-1, keepdims=True) + RMS_EPS) * g


def _mla_prep_kernel(z_ref, qg_ref, kg_ref, wq1_ref, wq2_ref, wk_ref, wv_ref, ca_ref, sb_ref,
                     q_ref, k_ref, v_ref):
    z = z_ref[0]
    ca = ca_ref[...]
    sb = sb_ref[...]
    qn = _rms(z[:, :MLA_Q_LORA], qg_ref[...]).astype(BF16)
    qa = _dot(qn, wq1_ref[...])
    qb = _dot(qn, wq2_ref[...])
    kvn = _rms(z[:, MLA_Q_LORA:MLA_Q_LORA + MLA_KV_LORA], kg_ref[...]).astype(BF16)
    kn = _dot(kvn, wk_ref[...])
    v = _dot(kvn, wv_ref[...])
    lane = lax.broadcasted_iota(jnp.int32, v.shape, 1)
    v_ref[0] = jnp.where((lane & 127) == MLA_V, 1.0, v).astype(v_ref.dtype)
    o = MLA_Q_LORA + MLA_KV_LORA
    kr = z[:, o:o + 128] * ca + z[:, o + 128:o + 256] * sb
    for h in range(MLA_HEADS):
        sl = slice(h * 128, (h + 1) * 128)
        q_ref[0, :, sl] = ((qa[:, sl] * ca + qb[:, sl] * sb) * (MLA_SCALE * LOG2E)).astype(q_ref.dtype)
        k_ref[0, :, sl] = (kn[:, sl] + kr).astype(k_ref.dtype)


def _mla_prep(z_mla, qg, kg, wq1, wq2, wk, wv, ca, sb):
    b, lt, _ = z_mla.shape
    full = lambda a: pl.BlockSpec(a.shape, lambda i, t: (0,) * a.ndim)
    tok = lambda w: pl.BlockSpec((1, TOK_TILE, w), lambda i, t: (i, t, 0))
    return pl.pallas_call(
        _mla_prep_kernel,
        grid=(b, lt // TOK_TILE),
        in_specs=[tok(ZX_F), full(qg), full(kg), full(wq1), full(wq2), full(wk), full(wv),
                  pl.BlockSpec((TOK_TILE, 128), lambda i, t: (t, 0)),
                  pl.BlockSpec((TOK_TILE, 128), lambda i, t: (t, 0))],
        out_specs=[tok(MLA_HEADS * 128)] * 3,
        out_shape=[jax.ShapeDtypeStruct((b, lt, MLA_HEADS * 128), BF16)] * 3,
        compiler_params=_cp(("arbitrary", "arbitrary")),
        name="mla_prep",
    )(z_mla, qg, kg, wq1, wq2, wk, wv, ca, sb)


ATTN_KEY_CHUNK = 768


def _attn_kernel(q_ref, k_ref, v_ref, o_ref, *, n_ctx):
    def attend(nk):
        kc = min(nk, ATTN_KEY_CHUNK)
        cols = [slice(h * 128, (h + 1) * 128) for h in range(2)]
        s = [[_dot_nt(q_ref[0, :, cols[h]], k_ref[0, c:c + kc, cols[h]]) for c in range(0, nk, kc)]
             for h in range(2)]
        m = [functools.reduce(jnp.maximum, [jnp.max(sc, axis=1, keepdims=True) for sc in s[h]]) for h in range(2)]
        o = [jnp.zeros((TOK_TILE, 128), F32) for _ in range(2)]
        for j, c in enumerate(range(0, nk, kc)):
            for h in range(2):
                p = jnp.exp2(s[h][j] - m[h]).astype(BF16)
                o[h] = o[h] + _dot(p, v_ref[0, c:c + kc, cols[h]])
        outs = [o[h][:, :MLA_V] / o[h][:, MLA_V:MLA_V + 1] for h in range(2)]
        o_ref[0] = jnp.concatenate(outs, axis=1).astype(o_ref.dtype)

    @pl.when(pl.program_id(2) == 0)
    def _():
        attend(n_ctx)

    @pl.when(pl.program_id(2) > 0)
    def _():
        attend(k_ref.shape[1])


def _attention(q, k, v, n_ctx):
    b, lt, _ = q.shape
    return pl.pallas_call(
        functools.partial(_attn_kernel, n_ctx=n_ctx),
        grid=(b, MLA_HEADS // 2, lt // TOK_TILE),
        in_specs=[
            pl.BlockSpec((1, TOK_TILE, 256), lambda i, h, t: (i, t, h)),
            pl.BlockSpec((1, lt, 256), lambda i, h, t: (i, 0, h)),
            pl.BlockSpec((1, lt, 256), lambda i, h, t: (i, 0, h)),
        ],
        out_specs=pl.BlockSpec((1, TOK_TILE, 2 * MLA_V), lambda i, h, t: (i, t, h)),
        out_shape=jax.ShapeDtypeStruct((b, lt, MLA_HEADS * MLA_V), BF16),
        compiler_params=_cp(("arbitrary", "arbitrary", "arbitrary")),
        name="mla_attention",
    )(q, k, v)


def _rev_tile(s, n):
    return jnp.where(s == 0, 0, n - s)


def _time_index(shape, dim, reverse):
    i = lax.broadcasted_iota(jnp.int32, shape, dim)
    return (shape[dim] - 1 - i) if reverse else i


def _blk(i, size):
    return lax.shift_right_logical(i, jnp.int32(int(math.log2(size))))


def _gla_chunks(items):
    c_len = items[0]["q"].shape[0]
    for it in items:
        g2 = it["g"] * LOG2E
        it["c"] = g2
        it["t"] = g2
        it["a"] = jnp.zeros((c_len, c_len), F32)
    hb = 1
    while hb < c_len:
        masks = {}
        for rev in {it["reverse"] for it in items}:
            ri = _time_index((c_len, c_len), 0, rev)
            ci = _time_index((c_len, c_len), 1, rev)
            pair = jnp.logical_and(_blk(ri, hb) == _blk(ci, hb) + 1, (_blk(ri, hb) & 1) == 1)
            odd = (_blk(_time_index((c_len, REC_D), 0, rev), hb) & 1) == 1
            masks[rev] = (pair, odd)
        for it in items:
            rev = it["reverse"]
            pair, odd = masks[rev]
            c, t = it["c"], it["t"]
            qt = (it["q"] * jnp.exp2(c)).astype(BF16)
            kt = (it["k"] * jnp.exp2(t - c)).astype(BF16)
            it["a"] = it["a"] + jnp.where(pair, _dot_nt(qt, kt), 0.0)
            t_lo = pltpu.roll(t, hb, 0)
            t_hi = pltpu.roll(t, c_len - hb, 0)
            prev, nxt = (t_hi, t_lo) if rev else (t_lo, t_hi)
            it["c"] = c + jnp.where(odd, prev, 0.0)
            it["t"] = t + jnp.where(odd, prev, nxt)
        hb *= 2
    outs = []
    for it in items:
        q, k, v, c, t = it["q"], it["k"], it["v"], it["c"], it["t"]
        st = it["st_ref"][...]
        qk = jnp.sum(q * k, axis=1, keepdims=True)
        outs.append(_dot_nt((q * jnp.exp2(c)).astype(BF16), st.astype(BF16))
                    + _dot(it["a"].astype(BF16), v.astype(BF16)) + qk * v)
        kd = (k * jnp.exp2(t - c)).astype(BF16)
        it["st_ref"][...] = st * jnp.exp2(t[0:1, :]) + _dot(v.T.astype(BF16), kd)
    return outs


def _hgrn2_kernel(qf_ref, vf_ref, ff_ref, qb_ref, vb_ref, fb_ref, lb_ref, of_ref, ob_ref, st_ref):
    @pl.when(pl.program_id(1) == 0)
    def _():
        st_ref[...] = jnp.zeros(st_ref.shape, F32)

    n_chunks = TOK_TILE // CHUNK

    def body(ci, carry):
        items, dests = [], []
        for d, (q_ref, v_ref, f_ref, o_ref) in enumerate(((qf_ref, vf_ref, ff_ref, of_ref),
                                                           (qb_ref, vb_ref, fb_ref, ob_ref))):
            cc = ci if d == 0 else n_chunks - 1 - ci
            rows = pl.ds(pl.multiple_of(cc * CHUNK, CHUNK), CHUNK)
            for h in range(REC_HEADS):
                cols = slice(h * REC_D, (h + 1) * REC_D)
                lb = lb_ref[d, :, cols]
                f = lb + (1.0 - lb) * _sigmoid(f_ref[0, rows, cols])
                items.append(dict(q=_silu(q_ref[0, rows, cols].astype(F32)), k=1.0 - f,
                                  v=v_ref[0, rows, cols].astype(F32),
                                  g=jnp.log(f), st_ref=st_ref.at[d, h], reverse=(d == 1)))
                dests.append((o_ref, rows, cols))
        for (o_ref, rows, cols), o in zip(dests, _gla_chunks(items)):
            o_ref[0, rows, cols] = o
        return carry

    lax.fori_loop(0, n_chunks, body, 0)


def _hgrn2(z_main, z_f, lb):
    b, lt, _ = z_main.shape
    n = lt // TOK_TILE
    fwd = lambda k: pl.BlockSpec((1, TOK_TILE, REC_W), lambda i, s: (i, s, k))
    bwd = lambda k: pl.BlockSpec((1, TOK_TILE, REC_W), lambda i, s: (i, _rev_tile(s, n), k))
    return pl.pallas_call(
        _hgrn2_kernel,
        grid=(b, n),
        in_specs=[fwd(ZM_Q), fwd(ZM_I), fwd(ZX_F // REC_W), bwd(ZM_Q), bwd(ZM_I), bwd(ZX_F // REC_W + 1),
                  pl.BlockSpec((2, 1, REC_W), lambda i, s: (0, 0, 0))],
        out_specs=[pl.BlockSpec((1, TOK_TILE, REC_W), lambda i, s: (i, s, 0)),
                   pl.BlockSpec((1, TOK_TILE, REC_W), lambda i, s: (i, _rev_tile(s, n), 0))],
        out_shape=[jax.ShapeDtypeStruct((b, lt, REC_W), F32)] * 2,
        scratch_shapes=[pltpu.VMEM((2, REC_HEADS, REC_D, REC_D), F32)],
        compiler_params=_cp(("arbitrary", "arbitrary")),
        name="hgrn2_scan",
    )(z_main, z_main, z_f, z_main, z_main, z_f, lb)


CONV_GAP = 8


def _gdn_prep_kernel(x_ref, w_ref, o_ref, pad_ref, *, n_ctx):
    lt = x_ref.shape[1]
    segs = ((0, n_ctx, CONV_GAP), (n_ctx, lt, 2 * CONV_GAP))
    gap = jnp.zeros((CONV_GAP, REC_D), F32)
    pad_ref[0:CONV_GAP, :] = gap
    pad_ref[n_ctx + CONV_GAP:n_ctx + 2 * CONV_GAP, :] = gap
    pad_ref[lt + 2 * CONV_GAP:lt + 3 * CONV_GAP, :] = gap
    for a, b, sh in segs:
        pad_ref[a + sh:b + sh, :] = x_ref[0, a:b, :]
    is_qk = pl.program_id(1) < 2 * REC_HEADS
    for a, b, sh in segs:
        acc = None
        for kk in range(CONV_K):
            s0 = a + sh + kk - CONV_K // 2
            tap = pad_ref[s0:s0 + (b - a), :] * w_ref[kk:kk + 1, :]
            acc = tap if acc is None else acc + tap
        y = _silu(acc)
        inv = lax.rsqrt(jnp.sum(y * y, axis=1, keepdims=True) + 1e-6)
        o_ref[0, a:b, :] = y * jnp.where(is_qk, inv, 1.0)


def _gdn_prep(z_gdn, conv_w, n_ctx):
    b, lt, _ = z_gdn.shape
    return pl.pallas_call(
        functools.partial(_gdn_prep_kernel, n_ctx=n_ctx),
        grid=(b, ZG_CONV // REC_D),
        in_specs=[pl.BlockSpec((1, lt, REC_D), lambda i, j: (i, 0, ZX_GDN // REC_D + j)),
                  pl.BlockSpec((CONV_K, REC_D), lambda i, j: (0, j))],
        out_specs=pl.BlockSpec((1, lt, REC_D), lambda i, j: (i, 0, j)),
        out_shape=jax.ShapeDtypeStruct((b, lt, ZG_CONV), F32),
        scratch_shapes=[pltpu.VMEM((lt + 3 * CONV_GAP, REC_D), F32)],
        compiler_params=_cp(("arbitrary", "arbitrary")),
        name="gdn_prep",
    )(z_gdn, conv_w)


def _gdn_tile(items):
    t_len = items[0]["q"].shape[0]
    n_chunks = t_len // CHUNK
    for it in items:
        ri = _time_index((t_len, t_len), 0, it["reverse"])
        ci = _time_index((t_len, t_len), 1, it["reverse"])
        it["ri"], it["ci"] = ri, ci
        same = _blk(ri, CHUNK) == _blk(ci, CHUNK)
        it["dec"] = jnp.where(jnp.logical_and(same, ri >= ci),
                              jnp.exp2(jnp.minimum(it["gcol"] - it["grow"], 0.0)), 0.0)
        it["kb"] = it["k"] * it["beta"]
        it["kbf"] = it["k"].astype(BF16)
    for it in items:
        ri, ci = it["ri"], it["ci"]
        lm = jnp.where(ri > ci, _dot_nt(it["kb"].astype(BF16), it["kbf"]) * it["dec"], 0.0)
        it["lm"] = lm
        it["tinv"] = jnp.where(ri == ci, 1.0, 0.0) - jnp.where(_blk(ri, 2) == _blk(ci, 2), lm, 0.0)
    hb = 2
    while hb < CHUNK:
        for it in items:
            ri, ci = it["ri"], it["ci"]
            off = jnp.where(_blk(ri, 2 * hb) == _blk(ci, 2 * hb),
                            jnp.where(_blk(ri, hb) == _blk(ci, hb), 0.0, it["lm"]), 0.0)
            it["tb"] = it["tinv"].astype(BF16)
            it["to"] = _dot(it["tb"], off.astype(BF16)).astype(BF16)
        for it in items:
            it["tinv"] = it["tinv"] - _dot(it["to"], it["tb"])
        hb *= 2
    for it in items:
        eg = jnp.exp2(it["gcol"])
        tb = it["tinv"].astype(BF16)
        uw = _dot(tb, jnp.concatenate([(it["v"] * it["beta"]).astype(BF16), (it["kb"] * eg).astype(BF16)], axis=1))
        it["u"] = uw[:, :REC_D]
        it["w"] = uw[:, REC_D:].astype(BF16)
        qs = it["q"] * (REC_D ** -0.5)
        it["aqk"] = (_dot_nt(qs.astype(BF16), it["kbf"]) * it["dec"]).astype(BF16)
        it["qd"] = (qs * eg).astype(BF16)
        kd = it["k"] * jnp.exp2(it["gtot"] - it["gcol"])
        it["kdt"] = [kd[c * CHUNK:(c + 1) * CHUNK].T.astype(BF16) for c in range(n_chunks)]
        it["s"] = it["s_ref"][...]
        it["o"], it["vn"] = [None] * n_chunks, [None] * n_chunks
    for step in range(n_chunks):
        for it in items:
            c = n_chunks - 1 - step if it["reverse"] else step
            it["c"] = c
            rows = slice(c * CHUNK, (c + 1) * CHUNK)
            it["sb"] = it["s"].astype(BF16)
            it["vn"][c] = (it["u"][rows] - _dot(it["w"][rows], it["sb"])).astype(BF16)
        for it in items:
            c = it["c"]
            rows = slice(c * CHUNK, (c + 1) * CHUNK)
            it["o"][c] = _dot(it["qd"][rows], it["sb"])
            it["s"] = (it["s"] * jnp.exp2(it["gtot"][c * CHUNK:c * CHUNK + 1])
                       + _dot(it["kdt"][c], it["vn"][c]))
    outs = []
    for it in items:
        it["s_ref"][...] = it["s"]
        outs.append(jnp.concatenate(it["o"], axis=0) + _dot(it["aqk"], jnp.concatenate(it["vn"], axis=0)))
    return outs


def _gdn_kernel(qf_ref, kf_ref, vf_ref, gf_ref, qb_ref, kb_ref, vb_ref, gb_ref, par_ref, of_ref, ob_ref, s_ref):
    @pl.when(pl.program_id(1) == 0)
    def _():
        s_ref[...] = jnp.zeros(s_ref.shape, F32)

    neg_a = -jnp.exp(par_ref[0:1, :])
    dt_bias = par_ref[1:2, :]
    r2 = lax.broadcasted_iota(jnp.int32, (TOK_TILE, TOK_TILE), 0)
    c2 = lax.broadcasted_iota(jnp.int32, (TOK_TILE, TOK_TILE), 1)
    same = _blk(r2, CHUNK) == _blk(c2, CHUNK)
    items, dests = [], []
    for d, (q_ref, k_ref, v_ref, g_ref, o_ref) in enumerate(((qf_ref, kf_ref, vf_ref, gf_ref, of_ref),
                                                              (qb_ref, kb_ref, vb_ref, gb_ref, ob_ref))):
        ab = g_ref[0]
        xa = ab + dt_bias
        g = (neg_a * LOG2E) * (jnp.maximum(xa, 0.0) + jnp.log(1.0 + jnp.exp(-jnp.abs(xa))))
        tri = jnp.where(jnp.logical_and(same, (r2 <= c2) if d == 1 else (r2 >= c2)), 1.0, 0.0)
        g_hi = g.astype(BF16)
        r1 = g - g_hi.astype(F32)
        g_mid = r1.astype(BF16)
        g_lo = (r1 - g_mid.astype(F32)).astype(BF16)
        sums = lambda m: _dot(m, g_hi) + (_dot(m, g_mid) + _dot(m, g_lo))
        gc = sums(tri.astype(BF16))
        gtot = sums(jnp.where(same, 1.0, 0.0).astype(BF16))
        gct = gc.T
        beta = _sigmoid(ab)
        for h in range(REC_HEADS):
            cols = slice(h * REC_D, (h + 1) * REC_D)
            ln = d * REC_HEADS + h
            items.append(dict(
                q=q_ref[0, :, cols], k=k_ref[0, :, cols], v=v_ref[0, :, cols],
                gcol=gc[:, ln:ln + 1], grow=gct[ln:ln + 1, :], gtot=gtot[:, ln:ln + 1],
                beta=beta[:, 2 * REC_HEADS + ln:2 * REC_HEADS + ln + 1], s_ref=s_ref.at[d, h],
                reverse=(d == 1)))
            dests.append((o_ref, cols))
    for (o_ref, cols), o in zip(dests, _gdn_tile(items)):
        o_ref[0, :, cols] = o


def _gdn(qkv, z_gdn, par):
    b, lt, _ = qkv.shape
    n = lt // TOK_TILE
    fwd = lambda k: pl.BlockSpec((1, TOK_TILE, REC_W), lambda i, s: (i, s, k))
    bwd = lambda k: pl.BlockSpec((1, TOK_TILE, REC_W), lambda i, s: (i, _rev_tile(s, n), k))
    gcol = (ZX_GDN + ZG_CONV) // 128
    return pl.pallas_call(
        _gdn_kernel,
        grid=(b, n),
        in_specs=[fwd(0), fwd(1), fwd(2), pl.BlockSpec((1, TOK_TILE, 128), lambda i, s: (i, s, gcol)),
                  bwd(0), bwd(1), bwd(2), pl.BlockSpec((1, TOK_TILE, 128), lambda i, s: (i, _rev_tile(s, n), gcol)),
                  pl.BlockSpec((8, 128), lambda i, s: (0, 0))],
        out_specs=[pl.BlockSpec((1, TOK_TILE, REC_W), lambda i, s: (i, s, 0)),
                   pl.BlockSpec((1, TOK_TILE, REC_W), lambda i, s: (i, _rev_tile(s, n), 0))],
        out_shape=[jax.ShapeDtypeStruct((b, lt, REC_W), F32)] * 2,
        scratch_shapes=[pltpu.VMEM((2, REC_HEADS, REC_D, REC_D), F32)],
        compiler_params=_cp(("arbitrary", "arbitrary")),
        name="gdn_scan",
    )(qkv, qkv, qkv, z_gdn, qkv, qkv, qkv, z_gdn, par)


def _layer_norm(x, g, b):
    mu = jnp.mean(x, axis=-1, keepdims=True)
    xc = x - mu
    var = jnp.mean(xc * xc, axis=-1, keepdims=True)
    return xc * lax.rsqrt(var + LN_EPS) * g + b


def _head_norm_gate(o, gate, w):
    outs = []
    for h in range(REC_HEADS):
        cols = slice(h * REC_D, (h + 1) * REC_D)
        oh = o[:, cols]
        n = oh * lax.rsqrt(jnp.mean(oh * oh, axis=-1, keepdims=True) + RMS_EPS) * w
        outs.append(n * _silu(gate[:, cols].astype(F32)))
    return jnp.concatenate(outs, axis=1)


def _merge_kernel(mla_ref, hf_ref, hb_ref, gf_ref, gb_ref, gates_ref, hgate_ref, ggate_ref, x_ref,
                  m2_ref, m3_ref, m4_ref, hw_ref, gw_ref, wb_ref, wo_ref, lg_ref, lbias_ref, wr_ref,
                  x1_ref, h2_ref, lt_ref):
    hg = _head_norm_gate(hf_ref[0] + hb_ref[0], hgate_ref[0], hw_ref[...]).astype(BF16)
    gd = _head_norm_gate(gf_ref[0] + gb_ref[0], ggate_ref[0], gw_ref[...]).astype(BF16)
    y = jnp.zeros((TOK_TILE, D_MODEL), F32)
    for n, o in enumerate((mla_ref[0], hg, gd)):
        y = y + _sigmoid(gates_ref[0, :, n * D_MODEL:(n + 1) * D_MODEL].astype(F32)) * _dot(o, wb_ref[n])
    y = _dot(y.astype(BF16), wo_ref[...])
    x1 = _layer_norm(DEEPNORM_ALPHA * x_ref[0] + m2_ref[0] * y, lg_ref[...], lbias_ref[...])
    x1_ref[0] = x1
    h2 = x1 * (1.0 + m4_ref[0]) + m3_ref[0]
    for p, part in enumerate(_pack_parts(h2)):
        h2_ref[p, 0] = part
    wr = wr_ref[...]
    wr_hi = wr.astype(BF16)
    wr_lo = (wr - wr_hi.astype(F32)).astype(BF16)
    h_hi = h2.astype(BF16)
    h_lo = (h2 - h_hi.astype(F32)).astype(BF16)
    lt_ref[...] = _dot_nt(wr_hi, h_hi) + (_dot_nt(wr_hi, h_lo) + _dot_nt(wr_lo, h_hi))


def _merge(mla_o, hg_f, hg_b, gd_f, gd_b, z_main, x_all, mod_l, hg_w, gdn_w, wb, wo, ln_g, ln_b, wr_t):
    b, lt, d = x_all.shape
    n = lt // TOK_TILE
    tok = lambda w, k=0: pl.BlockSpec((1, TOK_TILE, w), lambda i, t: (i, t, k))
    full = lambda a: pl.BlockSpec(a.shape, lambda i, t: (0,) * a.ndim)
    return pl.pallas_call(
        _merge_kernel,
        grid=(b, n),
        in_specs=[tok(BRANCH_W), tok(REC_W), tok(REC_W), tok(REC_W), tok(REC_W),
                  tok(N_BRANCH * D_MODEL, 0), tok(REC_W, ZM_HGATE), tok(REC_W, ZM_GGATE),
                  tok(d), _mod_spec(b, 2), _mod_spec(b, 3), _mod_spec(b, 4),
                  full(hg_w), full(gdn_w), full(wb), full(wo), full(ln_g), full(ln_b), full(wr_t)],
        out_specs=[tok(d), pl.BlockSpec((ROW_PARTS, 1, TOK_TILE, SC_ROW), lambda i, t: (0, i, t, 0)),
                   pl.BlockSpec((N_EXPERTS, TOK_TILE), lambda i, t: (0, i * n + t))],
        out_shape=[jax.ShapeDtypeStruct((b, lt, d), F32),
                   jax.ShapeDtypeStruct((ROW_PARTS, b, lt, SC_ROW), jnp.int32),
                   jax.ShapeDtypeStruct((N_EXPERTS, b * lt), F32)],
        compiler_params=_cp(("arbitrary", "arbitrary")),
        name="merge",
    )(mla_o, hg_f, hg_b, gd_f, gd_b, z_main, z_main, z_main, x_all, mod_l, mod_l, mod_l,
      hg_w, gdn_w, wb, wo, ln_g, ln_b, wr_t)


def _first_max(x, idx, axes):
    m = x
    for ax in axes:
        m = jnp.max(m, axis=ax, keepdims=True)
    first = jnp.where(x == m, idx, jnp.int32(2 ** 30))
    for ax in axes:
        first = jnp.min(first, axis=ax, keepdims=True)
    return m, first


def _route_kernel(lt_ref, bias_ref, e_ref, r_ref, w_ref, cnt_ref, run_ref):
    @pl.when(pl.program_id(0) == 0)
    def _():
        run_ref[...] = jnp.zeros(run_ref.shape, F32)

    n_tok = lt_ref.shape[1]
    per = N_EXPERTS // N_GROUPS
    scores = _sigmoid(lt_ref[...]).reshape(N_GROUPS, per, n_tok)
    sel = scores + bias_ref[...]
    ig = lax.broadcasted_iota(jnp.int32, sel.shape, 0)
    ij = lax.broadcasted_iota(jnp.int32, sel.shape, 1)
    top1, a1 = _first_max(sel, ij, (1,))
    top2 = jnp.max(jnp.where(ij == a1, -jnp.inf, sel), axis=1, keepdims=True)
    grp = top1 + top2
    igg = lax.broadcasted_iota(jnp.int32, grp.shape, 0)
    gsel = jnp.zeros(grp.shape, F32)
    for _ in range(TOPK_GROUPS):
        _, a = _first_max(grp, igg, (0,))
        hit = igg == a
        gsel = jnp.where(hit, 1.0, gsel)
        grp = jnp.where(hit, -jnp.inf, grp)
    cur = jnp.where(gsel > 0.5, sel, -jnp.inf)
    ie = ig * per + ij
    esel = jnp.zeros(cur.shape, F32)
    for _ in range(TOP_K):
        _, a = _first_max(cur, ie, (1, 0))
        hit = ie == a
        esel = jnp.where(hit, 1.0, esel)
        cur = jnp.where(hit, -jnp.inf, cur)
    w = scores * esel
    tot = jnp.sum(jnp.sum(w, axis=1, keepdims=True), axis=0, keepdims=True)
    w = (w / tot * ROUTED_SCALE).reshape(N_EXPERTS, n_tok)
    m = esel.reshape(N_EXPERTS, n_tok)
    mb = m.astype(BF16)
    ti = lax.broadcasted_iota(jnp.int32, (n_tok, n_tok), 0)
    tj = lax.broadcasted_iota(jnp.int32, (n_tok, n_tok), 1)
    upto = _dot(mb, jnp.where(ti <= tj, 1.0, 0.0).astype(BF16))
    ei = lax.broadcasted_iota(jnp.int32, (N_EXPERTS, N_EXPERTS), 0)
    ej = lax.broadcasted_iota(jnp.int32, (N_EXPERTS, N_EXPERTS), 1)
    lower = _dot(jnp.where(ej < ei, 1.0, 0.0).astype(BF16), mb)
    run = run_ref[:, 0:1]
    rank = run + upto - 1.0
    run_ref[...] = jnp.broadcast_to(run + upto[:, n_tok - 1:n_tok], run_ref.shape)
    cnt_ref[...] = run_ref[...]
    eid = lax.broadcasted_iota(jnp.int32, m.shape, 0).astype(F32)
    rows_e, rows_r, rows_w = [], [], []
    for k in range(TOP_K):
        pick = jnp.where(lower == float(k), m, 0.0)
        rows_e.append(jnp.sum(pick * eid, axis=0, keepdims=True))
        rows_r.append(jnp.sum(pick * rank, axis=0, keepdims=True))
        rows_w.append(jnp.sum(pick * w, axis=0, keepdims=True))
    e_ref[...] = jnp.concatenate(rows_e, axis=0).astype(jnp.int32)
    r_ref[...] = jnp.concatenate(rows_r, axis=0).astype(jnp.int32)
    w_ref[...] = jnp.concatenate(rows_w, axis=0).T


ROUTE_TILE = 512


def _route(logits_t, bias):
    n_tok = logits_t.shape[1]
    tt = ROUTE_TILE
    return pl.pallas_call(
        _route_kernel,
        grid=(n_tok // tt,),
        in_specs=[pl.BlockSpec((N_EXPERTS, tt), lambda i: (0, i)),
                  pl.BlockSpec((N_GROUPS, N_EXPERTS // N_GROUPS, 1), lambda i: (0, 0, 0))],
        out_specs=[pl.BlockSpec((TOP_K, tt), lambda i: (0, i)),
                   pl.BlockSpec((TOP_K, tt), lambda i: (0, i)),
                   pl.BlockSpec((tt, TOP_K), lambda i: (i, 0)),
                   pl.BlockSpec((N_EXPERTS, 128), lambda i: (0, 0))],
        out_shape=[jax.ShapeDtypeStruct((TOP_K, n_tok), jnp.int32),
                   jax.ShapeDtypeStruct((TOP_K, n_tok), jnp.int32),
                   jax.ShapeDtypeStruct((n_tok, TOP_K), F32),
                   jax.ShapeDtypeStruct((N_EXPERTS, 128), F32)],
        scratch_shapes=[pltpu.VMEM((N_EXPERTS, 128), F32)],
        compiler_params=_cp(("arbitrary",)),
        name="route",
    )(logits_t, bias)


def _slot_kernel(start_ref, e_ref, r_ref, p_ref, *, n_rows):
    e = e_ref[...]
    pos = r_ref[...]
    for j in range(N_EXPERTS):
        pos = pos + jnp.where(e == j, start_ref[j], 0)
    for p in range(ROW_PARTS):
        p_ref[p] = pos + p * n_rows


def _slots(starts, e_k, r_k, n_rows):
    n_tok = e_k.shape[1]
    tt = ROUTE_TILE
    spec = pl.BlockSpec((TOP_K, tt), lambda i, s: (0, i))
    return pl.pallas_call(
        functools.partial(_slot_kernel, n_rows=n_rows),
        grid_spec=pltpu.PrefetchScalarGridSpec(
            num_scalar_prefetch=1, grid=(n_tok // tt,), in_specs=[spec, spec],
            out_specs=pl.BlockSpec((ROW_PARTS, TOP_K, tt), lambda i, s: (0, 0, i))),
        out_shape=jax.ShapeDtypeStruct((ROW_PARTS, TOP_K, n_tok), jnp.int32),
        compiler_params=_cp(("arbitrary",)),
        name="moe_slots",
    )(starts, e_k, r_k)


def _sc_mesh():
    return plsc.VectorSubcoreMesh(core_axis_name="c", subcore_axis_name="s")


def _sc_scatter(x, idx, n_out):
    parts, reps, t = idx.shape
    nbt = t // SC_WINDOW

    @functools.partial(pl.kernel, out_type=jax.ShapeDtypeStruct((n_out, SC_ROW), x.dtype), mesh=_sc_mesh())
    def k(x_hbm, i_hbm, o_hbm):
        def body(x_vmem, i_vmem):
            for r in range(reps):
                pltpu.sync_copy(x_vmem, o_hbm.at[i_vmem.at[0, r]])

        pltpu.emit_pipeline(
            body,
            grid=(parts * nbt,),
            in_specs=[pl.BlockSpec((SC_WINDOW, SC_ROW), lambda i: (i, 0)),
                      pl.BlockSpec((1, reps, SC_WINDOW), lambda i: (i // nbt, 0, i % nbt))],
            out_specs=[],
            core_axis_name=("c", "s"),
            dimension_semantics=(pltpu.PARALLEL,),
        )(x_hbm, i_hbm)

    return k(x, idx)


def _sc_gather(table, idx):
    n = idx.shape[0]

    @functools.partial(pl.kernel, out_type=jax.ShapeDtypeStruct((n, SC_ROW), table.dtype), mesh=_sc_mesh())
    def k(x_hbm, i_hbm, o_hbm):
        def body(i_vmem, o_vmem):
            pltpu.sync_copy(x_hbm.at[i_vmem.at[0]], o_vmem)

        pltpu.emit_pipeline(
            body,
            grid=(n // SC_WINDOW,),
            in_specs=[pl.BlockSpec((1, SC_WINDOW), lambda i: (0, i))],
            out_specs=[pl.BlockSpec((SC_WINDOW, SC_ROW), lambda i: (i, 0))],
            core_axis_name=("c", "s"),
            dimension_semantics=(pltpu.PARALLEL,),
        )(i_hbm, o_hbm)

    return k(table, idx.reshape(1, n))


MOE_TM = 512


def _expert_kernel(te_ref, tr_ref, x_ref, wgu_ref, wd_ref, y_ref, wgu_bf, wd_bf):
    i = pl.program_id(0)
    valid = tr_ref[i]

    @pl.when(jnp.logical_or(i == 0, te_ref[i] != te_ref[jnp.maximum(i - 1, 0)]))
    def _():
        wgu_bf[...] = wgu_ref[0].astype(BF16)
        wd_bf[...] = wd_ref[0].astype(BF16)

    @pl.when(valid > 0)
    def _():
        live = lax.broadcasted_iota(jnp.int32, x_ref.shape[1:], 0) < valid
        x = _unpack_parts([jnp.where(live, x_ref[p], 0) for p in range(ROW_PARTS)])
        gu = _dot(x, wgu_bf[...])
        act = _silu(gu[:, :EXPERT_FF]) * gu[:, EXPERT_FF:]
        for p, part in enumerate(_pack_parts(_dot(act.astype(BF16), wd_bf[...]))):
            y_ref[p] = part

    @pl.when(valid == 0)
    def _():
        y_ref[...] = jnp.zeros(y_ref.shape, y_ref.dtype)


def _experts(tile_expert, tile_rows, xs, wgu, wd, layer):
    _, n_rows, _ = xs.shape
    d = wgu.shape[2]
    rows = pl.BlockSpec((ROW_PARTS, MOE_TM, SC_ROW), lambda i, te, tr: (0, i, 0))
    return pl.pallas_call(
        _expert_kernel,
        grid_spec=pltpu.PrefetchScalarGridSpec(
            num_scalar_prefetch=2,
            grid=(n_rows // MOE_TM,),
            in_specs=[rows,
                      pl.BlockSpec((None, 1, d, 2 * EXPERT_FF), lambda i, te, tr: (layer, te[i], 0, 0)),
                      pl.BlockSpec((None, 1, EXPERT_FF, d), lambda i, te, tr: (layer, te[i], 0, 0))],
            out_specs=rows,
            scratch_shapes=[pltpu.VMEM((d, 2 * EXPERT_FF), BF16), pltpu.VMEM((EXPERT_FF, d), BF16)]),
        out_shape=jax.ShapeDtypeStruct(xs.shape, jnp.int32),
        compiler_params=_cp(("arbitrary",)),
        name="moe_experts",
    )(tile_expert, tile_rows, xs, wgu, wd)


def _combine_kernel(g_ref, w_ref, h_ref, x_ref, m5_ref, wsgu_ref, wsd_ref, lg_ref, lb_ref, *rest):
    h = _unpack_parts([h_ref[p, 0] for p in range(ROW_PARTS)])
    gu = _dot(h, wsgu_ref[...])
    act = _silu(gu[:, :SHARED_FF]) * gu[:, SHARED_FF:]
    f = _dot(act.astype(BF16), wsd_ref[...])
    w = w_ref[...]
    for k in range(TOP_K):
        f = f + w[:, k:k + 1] * _unpack_parts([g_ref[p, k] for p in range(ROW_PARTS)], F32)
    x_new = _layer_norm(DEEPNORM_ALPHA * x_ref[0] + m5_ref[0] * f, lg_ref[...], lb_ref[...])
    if len(rest) == 1:
        rest[0][0] = x_new
    else:
        sh_ref, sc_ref, o_ref, hn_ref = rest
        o_ref[0] = x_new
        hn_ref[0] = (x_new * (1.0 + sc_ref[0]) + sh_ref[0]).astype(hn_ref.dtype)


def _combine(g, w_k, h2p, x1, mod_l, wsgu, wsd, ln_g, ln_b, mod_next):
    b, lt, d = x1.shape
    n = lt // TOK_TILE
    tok = pl.BlockSpec((1, TOK_TILE, d), lambda i, t: (i, t, 0))
    full = lambda a: pl.BlockSpec(a.shape, lambda i, t: (0,) * a.ndim)
    in_specs = [pl.BlockSpec((ROW_PARTS, TOP_K, TOK_TILE, SC_ROW), lambda i, t: (0, 0, i * n + t, 0)),
                pl.BlockSpec((TOK_TILE, TOP_K), lambda i, t: (i * n + t, 0)),
                pl.BlockSpec((ROW_PARTS, 1, TOK_TILE, SC_ROW), lambda i, t: (0, i, t, 0)),
                tok, _mod_spec(b, 5), full(wsgu), full(wsd), full(ln_g), full(ln_b)]
    args = [g, w_k, h2p, x1, mod_l, wsgu, wsd, ln_g, ln_b]
    out_specs, out_shape = tok, jax.ShapeDtypeStruct((b, lt, d), F32)
    if mod_next is not None:
        in_specs += [_mod_spec(b, 0), _mod_spec(b, 1)]
        args += [mod_next, mod_next]
        out_specs, out_shape = [tok, tok], [out_shape, jax.ShapeDtypeStruct((b, lt, d), BF16)]
    return pl.pallas_call(
        _combine_kernel,
        grid=(b, n),
        in_specs=in_specs,
        out_specs=out_specs,
        out_shape=out_shape,
        compiler_params=_cp(("arbitrary", "arbitrary")),
        name="moe_combine",
    )(*args)


def _moe_sparse(h2p, logits_t, bias, wgu, wd, layer, wsgu, wsd, x1, mod_l, ln_g, ln_b, mod_next):
    n_tok = h2p.shape[1] * h2p.shape[2]
    e_k, r_k, w_k, counts = _route(logits_t, bias)
    counts = counts[:, 0].astype(jnp.int32)
    padded = (counts + MOE_TM - 1) // MOE_TM * MOE_TM
    ends = jnp.cumsum(padded)
    starts = ends - padded
    n_rows = (n_tok * TOP_K + N_EXPERTS * (MOE_TM - 1)) // MOE_TM * MOE_TM
    tile0 = jnp.arange(n_rows // MOE_TM, dtype=jnp.int32) * MOE_TM
    tile_expert = jnp.minimum(jnp.sum(tile0[:, None] >= ends[None, :], axis=1), N_EXPERTS - 1).astype(jnp.int32)
    inside = jnp.logical_and(tile0[:, None] >= starts[None, :], tile0[:, None] < ends[None, :])
    left = jnp.clip(counts[None, :] - (tile0[:, None] - starts[None, :]), 0, MOE_TM)
    tile_rows = jnp.sum(jnp.where(inside, left, 0), axis=1).astype(jnp.int32)
    idx = _slots(starts.astype(jnp.int32), e_k, r_k, n_rows)
    xs = _sc_scatter(h2p.reshape(ROW_PARTS * n_tok, SC_ROW), idx, ROW_PARTS * n_rows)
    ys = _experts(tile_expert, tile_rows, xs.reshape(ROW_PARTS, n_rows, SC_ROW), wgu, wd, layer)
    g = _sc_gather(ys.reshape(ROW_PARTS * n_rows, SC_ROW), idx.reshape(-1))
    g = g.reshape(ROW_PARTS, TOP_K, n_tok, SC_ROW)
    return _combine(g, w_k, h2p, x1, mod_l, wsgu, wsd, ln_g, ln_b, mod_next)


def _rope_tables(n_ctx, n_lat):
    rows = n_lat // GRID_W
    row = jnp.broadcast_to(jnp.arange(rows, dtype=F32)[:, None], (rows, GRID_W)).reshape(-1)
    col = jnp.broadcast_to(jnp.arange(GRID_W, dtype=F32)[None, :], (rows, GRID_W)).reshape(-1)
    inv = ROPE_BASE ** (-jnp.arange(ROPE_FREQS, dtype=F32) / ROPE_FREQS)
    ang = jnp.stack([row[:, None] * inv, col[:, None] * inv], axis=1)
    cos = jnp.cos(ang)[:, :, None, :]
    sin = jnp.sin(ang)[:, :, None, :]
    cos32 = jnp.broadcast_to(cos, (n_lat, 2, 2, ROPE_FREQS)).reshape(n_lat, MLA_ROPE)
    sin32 = jnp.concatenate([-sin, sin], axis=2).reshape(n_lat, MLA_ROPE)
    cos32 = jnp.concatenate([jnp.ones((n_ctx, MLA_ROPE), F32), cos32], axis=0)
    sin32 = jnp.concatenate([jnp.zeros((n_ctx, MLA_ROPE), F32), sin32], axis=0)
    lt = n_ctx + n_lat
    ca = jnp.concatenate([jnp.ones((lt, MLA_NOPE), F32), cos32, jnp.zeros((lt, 32), F32)], axis=1)
    sb = jnp.concatenate([jnp.zeros((lt, MLA_NOPE), F32), sin32, jnp.zeros((lt, 32), F32)], axis=1)
    return ca, sb


def _rope_partner():
    idx = np.arange(MLA_ROPE).reshape(2, 2, ROPE_FREQS)
    return idx[:, ::-1, :].reshape(-1)


IN_SIZES = (MLA_Q_LORA, MLA_KV_LORA + MLA_ROPE, REC_W, REC_W, REC_W, REC_W, REC_W,
            3 * REC_W, REC_W, 2 * REC_HEADS, 2 * REC_HEADS, N_BRANCH * D_MODEL)
IN_OFFS = tuple(int(o) for o in np.cumsum((0,) + IN_SIZES))
REPACK_ROWS = 128


def _repack_kernel(w_ref, main_ref, aux_ref):
    rows = w_ref.shape[1]
    seg = lambda i: w_ref[0, :, IN_OFFS[i]:IN_OFFS[i + 1]]
    zeros = lambda n: jnp.zeros((rows, n), F32)
    k_rope = w_ref[0, :, IN_OFFS[1] + MLA_KV_LORA:IN_OFFS[2]]
    h = ROPE_FREQS
    k_partner = jnp.concatenate([k_rope[:, h:2 * h], k_rope[:, :h], k_rope[:, 3 * h:], k_rope[:, 2 * h:3 * h]], axis=1)
    main_ref[0] = jnp.concatenate([seg(11), seg(2), seg(3), seg(6), seg(8)], axis=1).astype(BF16)
    aux_ref[0] = jnp.concatenate([seg(0), w_ref[0, :, IN_OFFS[1]:IN_OFFS[1] + MLA_KV_LORA],
                                  zeros(MLA_NOPE), k_rope, zeros(32),
                                  zeros(MLA_NOPE), k_partner, zeros(32), zeros(ZX_F - ZA_WIDTH),
                                  seg(4), seg(5),
                                  seg(7), seg(9), seg(10), zeros(256 - 4 * REC_HEADS)], axis=1).astype(BF16)


def _repack_w_in(w_in):
    depth, d, width = w_in.shape
    widths = (N_BRANCH * D_MODEL + 4 * REC_W, ZX_WIDTH)
    return pl.pallas_call(
        _repack_kernel,
        grid=(depth, d // REPACK_ROWS),
        in_specs=[pl.BlockSpec((1, REPACK_ROWS, width), lambda l, r: (l, r, 0))],
        out_specs=[pl.BlockSpec((1, REPACK_ROWS, w), lambda l, r: (l, r, 0)) for w in widths],
        out_shape=[jax.ShapeDtypeStruct((depth, d, w), BF16) for w in widths],
        compiler_params=_cp(("arbitrary", "arbitrary")),
        name="repack_w_in",
    )(w_in)


def _pack_mla(w_q_b, w_kv_b):
    partner = _rope_partner()
    r = w_q_b.shape[0]
    qb = w_q_b.reshape(r, MLA_HEADS, MLA_NOPE + MLA_ROPE)
    zq = lambda n: jnp.zeros((r, MLA_HEADS, n), w_q_b.dtype)
    wq1 = jnp.concatenate([qb, zq(32)], axis=2).reshape(r, MLA_HEADS * 128)
    wq2 = jnp.concatenate([zq(MLA_NOPE), qb[:, :, MLA_NOPE:][:, :, partner], zq(32)], axis=2).reshape(r, MLA_HEADS * 128)
    rk = w_kv_b.shape[0]
    kvb = w_kv_b.reshape(rk, MLA_HEADS, MLA_NOPE + MLA_V)
    wk = jnp.concatenate([kvb[:, :, :MLA_NOPE], jnp.zeros((rk, MLA_HEADS, 64), w_kv_b.dtype)], axis=2)
    wk = wk.reshape(rk, MLA_HEADS * 128)
    wv = jnp.concatenate([kvb[:, :, MLA_NOPE:], jnp.zeros((rk, MLA_HEADS, 128 - MLA_V), w_kv_b.dtype)], axis=2)
    wv = wv.reshape(rk, MLA_HEADS * 128)
    bf = lambda a: a.astype(BF16)
    return bf(wq1), bf(wq2), bf(wk), bf(wv)


def kernel(x, c, ctx, c_ctx, w_mod, b_mod, w_in, q_a_norm, w_q_b, kv_a_norm, w_kv_b, hg_lb_logits, hg_norm,
           gdn_conv, gdn_a_log, gdn_dt_bias, gdn_norm, w_branch, w_out, ln1_g, ln1_b, ln2_g, ln2_b,
           w_router, router_bias, w_gu, w_down, w_sh_gu, w_sh_down):
    batch, n_lat, d = x.shape
    n_ctx = ctx.shape[1]
    assert n_ctx == TOK_TILE and n_lat % TOK_TILE == 0 and batch < MOD_ROWS and d == D_MODEL
    lt = n_ctx + n_lat

    c_all = jnp.zeros((MOD_ROWS, d), F32).at[:batch].set(c).at[batch].set(c_ctx)
    mod = _mod_all(c_all, w_mod, b_mod).reshape(DEPTH, MOD_ROWS, 1, 6 * d)
    ca, sb = _rope_tables(n_ctx, n_lat)
    lb_soft = jax.nn.softmax(hg_lb_logits.astype(F32), axis=0)
    lower = (jnp.cumsum(lb_soft, axis=0) - lb_soft[0]).reshape(DEPTH, 2, 1, REC_W)
    row = lambda a: a.reshape(1, -1)

    w_main, w_aux = _repack_w_in(w_in)
    x_all = jnp.concatenate([ctx, x], axis=1)
    h = _modulate(x_all, mod[0])
    for l in range(DEPTH):
        wq1, wq2, wk, wv = _pack_mla(w_q_b[l], w_kv_b[l])
        mod_l = mod[l]
        z_main = _proj(h, w_main, l, 2 * REC_W, BF16)
        z_f = z_mla = z_gdn = _proj(h, w_aux, l, ZX_WIDTH // 3)

        q, k, v = _mla_prep(z_mla, row(q_a_norm[l]), row(kv_a_norm[l]), wq1, wq2, wk, wv, ca, sb)
        mla_o = _attention(q, k, v, n_ctx)

        hg_f, hg_b = _hgrn2(z_main, z_f, lower[l])

        qkv = _gdn_prep(z_gdn, gdn_conv[l], n_ctx)
        par = jnp.zeros((8, 128), F32)
        par = par.at[0, :2 * REC_HEADS].set(gdn_a_log[l].reshape(-1))
        par = par.at[1, :2 * REC_HEADS].set(gdn_dt_bias[l].reshape(-1))
        gd_f, gd_b = _gdn(qkv, z_gdn, par)

        x1, h2, logits_t = _merge(mla_o, hg_f, hg_b, gd_f, gd_b, z_main, x_all, mod_l,
                                  row(hg_norm[l]), row(gdn_norm[l]), w_branch[l].astype(BF16),
                                  w_out[l].astype(BF16), row(ln1_g[l]), row(ln1_b[l]), w_router[l].T)
        out = _moe_sparse(h2, logits_t,
                          router_bias[l].reshape(N_GROUPS, N_EXPERTS // N_GROUPS, 1),
                          w_gu, w_down, l,
                          w_sh_gu[l].astype(BF16), w_sh_down[l].astype(BF16),
                          x1, mod_l, row(ln2_g[l]), row(ln2_b[l]), mod[l + 1] if l + 1 < DEPTH else None)
        x_all, h = out if l + 1 < DEPTH else (out, None)
    return x_all[:, n_ctx:, :]
```
